```python
import math
import jax
import jax.numpy as jnp
from jax import lax
import numpy as np

D_MODEL = 1024
BATCH = 8
SEQ = 2048
DEPTH = 4
DEC_BATCH = 32
DEC_SEQ = 1
PAST_LEN = 16384
PAGE_SIZE = 128

F32 = jnp.float32
MLA_HEADS = 8
MLA_Q_LORA = 384
MLA_KV_LORA = 256
MLA_NOPE = 64
MLA_ROPE = 32
MLA_V = 64
MLA_KV_DIM = MLA_KV_LORA + MLA_ROPE
MLA_Q_BLOCK = 128
ROPE_BASE = 10000.0
GLA_HEADS = 4
GLA_DK = 64
GLA_DV = 128
GLA_GATE_RANK = 16
GLA_TAU = 16.0
GLA_CHUNK = 64
DIL_PAIRS = ((128, 1), (512, 4), (2048, 16))
DIL_GROUPS = 3
DIL_HPG = 8
DIL_HD = 64
D_FF = -(-8 * D_MODEL // (3 * 256)) * 256
N_EVEN = (DEPTH + 1) // 2
N_ODD = DEPTH // 2
DN_ALPHA = (2.0 * DEPTH) ** 0.25
DN_BETA = (8.0 * DEPTH) ** -0.25
SPLIT_A = (MLA_Q_LORA, MLA_KV_DIM, GLA_HEADS * GLA_DK, GLA_HEADS * GLA_DK, GLA_HEADS * GLA_DV, GLA_GATE_RANK, GLA_HEADS * GLA_DV)
IN_A = MLA_Q_LORA + MLA_KV_DIM + 2 * GLA_HEADS * GLA_DK + 2 * GLA_HEADS * GLA_DV + GLA_GATE_RANK
MIX_A = MLA_HEADS * MLA_V + GLA_HEADS * GLA_DV
IN_C = 3 * DIL_GROUPS * DIL_HPG * DIL_HD
MIX_C = DIL_HPG * DIL_HD

kernel_name = 'hybrid_mla_gla_dilated_deepnorm_step'


def rms_norm(x, g, eps=1e-6):
    xf = x.astype(F32)
    y = xf * lax.rsqrt(jnp.mean(xf * xf, axis=-1, keepdims=True) + eps)
    return (y * g.astype(F32)).astype(x.dtype)


def layer_norm(x, g, b, eps=1e-5):
    xf = x.astype(F32)
    mu = jnp.mean(xf, axis=-1, keepdims=True)
    var = jnp.mean(jnp.square(xf - mu), axis=-1, keepdims=True)
    return ((xf - mu) * lax.rsqrt(var + eps) * g.astype(F32) + b.astype(F32)).astype(x.dtype)


def split_cols(h, sizes):
    return jnp.split(h, np.cumsum(sizes)[:-1].tolist(), axis=-1)


def rope(x, pos):
    half = x.shape[-1] // 2
    inv = ROPE_BASE ** (-jnp.arange(half, dtype=F32) / half)
    ang = pos.astype(F32)[:, None] * inv[None, :]
    cos, sin = jnp.cos(ang)[None, :, None, :], jnp.sin(ang)[None, :, None, :]
    x1, x2 = x[..., :half].astype(F32), x[..., half:].astype(F32)
    return jnp.concatenate([x1 * cos - x2 * sin, x2 * cos + x1 * sin], axis=-1).astype(x.dtype)


def alibi_slopes():
    n = DIL_GROUPS * DIL_HPG
    h = jnp.arange(1, n + 1, dtype=F32)
    return (2.0 ** (-8.0 * h / n)).reshape(DIL_GROUPS, DIL_HPG)


def mla_attend(q_lat, q_pe, kv, q_pos):
    c_kv, k_pe = kv[..., :MLA_KV_LORA], kv[..., MLA_KV_LORA:]
    s = jnp.einsum('bqhc,btc->bhqt', q_lat, c_kv) + jnp.einsum('bqhr,btr->bhqt', q_pe, k_pe)
    s = s.astype(F32) * (MLA_NOPE + MLA_ROPE) ** -0.5
    k_pos = jnp.arange(kv.shape[1])
    s = jnp.where(k_pos[None, :] <= q_pos[:, None], s, -jnp.inf)
    p = jax.nn.softmax(s, axis=-1).astype(kv.dtype)
    return jnp.einsum('bhqt,btc->bqhc', p, c_kv)


def gla_chunked(q, k, v, log_a, s0, chunk):
    b_sz, l_len, h, _ = q.shape
    n = l_len // chunk

    def to_chunks(t):
        return t.reshape(b_sz, n, chunk, h, t.shape[-1]).transpose(1, 0, 3, 2, 4)

    mask = jnp.tril(jnp.ones((chunk, chunk), dtype=bool))

    def step(s, inp):
        qi, ki, vi, gi = inp
        cum = jnp.cumsum(gi.astype(F32), axis=2)
        diff = cum[:, :, :, None, :] - cum[:, :, None, :, :]
        decay = jnp.exp(jnp.where(mask[:, :, None], diff, -jnp.inf))
        attn = jnp.einsum('bhtk,bhsk,bhtsk->bhts', qi, ki, decay)
        o = jnp.einsum('bhts,bhsv->bhtv', attn, vi) + jnp.einsum('bhtk,bhkv->bhtv', qi * jnp.exp(cum), s)
        last = cum[:, :, -1:, :]
        s_new = jnp.exp(last[:, :, 0, :, None]) * s + jnp.einsum('bhsk,bhsv->bhkv', ki * jnp.exp(last - cum), vi)
        return s_new.astype(F32), o.astype(F32)

    s_fin, o = lax.scan(step, s0, (to_chunks(q), to_chunks(k), to_chunks(v), to_chunks(log_a)))
    o = o.transpose(1, 0, 3, 2, 4).reshape(b_sz, l_len, h, v.shape[-1])
    return o.astype(v.dtype), s_fin


def even_mixer(x, pos, mla_past, s0, w_in, q_norm, w_uq, kv_norm, w_uk, w_uv, w_gate2, b_gate, g_norm, w_out):
    b_sz, l_len, _ = x.shape
    c_q, kv_a, g_q, g_k, g_v, g_lr, g_r = split_cols(x @ w_in, SPLIT_A)
    q = (rms_norm(c_q, q_norm) @ w_uq).reshape(b_sz, l_len, MLA_HEADS, MLA_NOPE + MLA_ROPE)
    q_pe = rope(q[..., MLA_NOPE:], pos)
    q_lat = jnp.einsum('blhd,chd->blhc', q[..., :MLA_NOPE], w_uk)
    c_kv = rms_norm(kv_a[..., :MLA_KV_LORA], kv_norm)
    k_pe = rope(kv_a[:, :, None, MLA_KV_LORA:], pos)[:, :, 0]
    rows = jnp.concatenate([c_kv, k_pe], axis=-1)
    keys = rows if mla_past is None else jnp.concatenate([mla_past.astype(rows.dtype), rows], axis=1)
    qb = MLA_Q_BLOCK if l_len % MLA_Q_BLOCK == 0 else l_len
    nb = l_len // qb

    def blk(t):
        return jnp.swapaxes(t.reshape((b_sz, nb, qb) + t.shape[2:]), 0, 1)

    lat = lax.map(lambda a: mla_attend(a[0], a[1], keys, a[2]), (blk(q_lat), blk(q_pe), pos.reshape(nb, qb)))
    lat = jnp.swapaxes(lat, 0, 1).reshape(b_sz, l_len, MLA_HEADS, MLA_KV_LORA)
    mla_out = jnp.einsum('blhc,chv->blhv', lat, w_uv).reshape(b_sz, l_len, MLA_HEADS * MLA_V)
    gq = g_q.reshape(b_sz, l_len, GLA_HEADS, GLA_DK) * GLA_DK ** -0.5
    gk = g_k.reshape(b_sz, l_len, GLA_HEADS, GLA_DK)
    gv = g_v.reshape(b_sz, l_len, GLA_HEADS, GLA_DV)
    log_a = jax.nn.log_sigmoid((g_lr @ w_gate2 + b_gate).astype(F32)).reshape(b_sz, l_len, GLA_HEADS, GLA_DK) / GLA_TAU
    o, s_fin = gla_chunked(gq, gk, gv, log_a, s0.astype(F32), math.gcd(l_len, GLA_CHUNK))
    o = rms_norm(o, g_norm) * jax.nn.silu(g_r.reshape(b_sz, l_len, GLA_HEADS, GLA_DV))
    out = jnp.concatenate([mla_out, o.reshape(b_sz, l_len, GLA_HEADS * GLA_DV)], axis=-1) @ w_out
    return out, rows, s_fin


def dilated_band_attention(q, k, v, window, dil, slopes):
    b_sz, s_len, h, e = q.shape
    n = window // dil
    ls = s_len // dil
    nb = -(-ls // n)
    pad = nb * n - ls

    def residues(t, front):
        t = t.reshape(b_sz, ls, dil, h, e).transpose(0, 2, 1, 3, 4)
        return jnp.pad(t, ((0, 0), (0, 0), (front, pad), (0, 0), (0, 0)))

    def band(t):
        t = residues(t, n).reshape(b_sz, dil, nb + 1, n, h, e)
        return jnp.concatenate([t[:, :, :-1], t[:, :, 1:]], axis=3)

    qb = residues(q, 0).reshape(b_sz, dil, nb, n, h, e)
    kb, vb = band(k), band(v)
    steps = jnp.arange(n)[:, None] + n - jnp.arange(2 * n)[None, :]
    key_idx = jnp.arange(nb)[:, None] * n + jnp.arange(2 * n)[None, :] - n
    valid = ((steps >= 0) & (steps <= n))[None] & (key_idx >= 0)[:, None, :]
    bias = -slopes.astype(F32)[:, None, None] * (steps * dil).astype(F32)
    s = jnp.einsum('brnqhe,brnkhe->brnhqk', qb, kb).astype(F32) * e ** -0.5 + bias
    s = jnp.where(valid[:, None], s, -jnp.inf)
    lse = jax.nn.logsumexp(s, axis=-1)
    p = jnp.exp(s - lse[..., None]).astype(v.dtype)
    o = jnp.einsum('brnhqk,brnkhe->brnqhe', p, vb)

    def unres(t):
        t = t.reshape((b_sz, dil, nb * n) + t.shape[4:])[:, :, :ls]
        return jnp.swapaxes(t, 1, 2).reshape((b_sz, s_len) + t.shape[3:])

    return unres(o), unres(jnp.swapaxes(lse, 3, 4))


def dilated_gather_attention(q, k, v, buf, window, dil, slopes):
    b_sz, l_len, h, e = q.shape
    nbuf = buf.shape[1]
    kv = jnp.concatenate([buf.astype(k.dtype), jnp.stack([k, v], axis=2)], axis=1)
    j = jnp.arange(window // dil + 1)
    idx = nbuf + jnp.arange(l_len)[:, None] - j[None, :] * dil
    sel = kv[:, jnp.maximum(idx, 0)]
    s = jnp.einsum('blhe,bljhe->blhj', q, sel[:, :, :, 0]).astype(F32) * e ** -0.5
    s = s - slopes.astype(F32)[:, None] * (j * dil).astype(F32)
    s = jnp.where((idx >= 0)[None, :, None, :], s, -jnp.inf)
    lse = jax.nn.logsumexp(s, axis=-1)
    p = jnp.exp(s - lse[..., None]).astype(v.dtype)
    return jnp.einsum('blhj,bljhe->blhe', p, sel[:, :, :, 1]), lse


def odd_mixer(x, bufs, w_in, w_out):
    b_sz, l_len, _ = x.shape
    qkv = (x @ w_in).reshape(b_sz, l_len, 3, DIL_GROUPS, DIL_HPG, DIL_HD)
    slopes = alibi_slopes()
    outs, lses, rows = [], [], []
    for g, (window, dil) in enumerate(DIL_PAIRS):
        q, k, v = qkv[:, :, 0, g], qkv[:, :, 1, g], qkv[:, :, 2, g]
        if bufs is None:
            o, lse = dilated_band_attention(q, k, v, window, dil, slopes[g])
            keep = min(window, l_len)
            rows.append(jnp.stack([k[:, l_len - keep:], v[:, l_len - keep:]], axis=2))
        else:
            o, lse = dilated_gather_attention(q, k, v, bufs[g], window, dil, slopes[g])
            rows.append(jnp.stack([k, v], axis=2))
        outs.append(o)
        lses.append(lse)
    wts = jax.nn.softmax(jnp.stack(lses), axis=0)
    o = jnp.einsum('gblh,gblhe->blhe', wts, jnp.stack(outs).astype(F32)).astype(x.dtype)
    return o.reshape(b_sz, l_len, MIX_C) @ w_out, rows


def swiglu(x, w_in, w_out):
    gate, up = jnp.split(x @ w_in, 2, axis=-1)
    return (jax.nn.silu(gate) * up) @ w_out


def setup_inputs(seed: int = 0) -> dict:
    key = jax.random.key(seed)
    ks = iter(jax.random.split(key, 32))

    def nrm(shape, scale):
        return jax.random.normal(next(ks), shape, F32) * scale

    n_pages = PAST_LEN // PAGE_SIZE
    used = DEC_BATCH * n_pages
    n_pool = used + (used + 3) // 4
    page_table = jax.random.permutation(next(ks), n_pool)[:used].reshape(DEC_BATCH, n_pages).astype(jnp.int32)
    bufl = [min(w, PAST_LEN) for w, _ in DIL_PAIRS]
    return {
        'x_prompt': nrm((BATCH, SEQ, D_MODEL), 1.0),
        'x_sample': nrm((DEC_BATCH, DEC_SEQ, D_MODEL), 1.0),
        'cache_mla': nrm((N_EVEN, n_pool, PAGE_SIZE, MLA_KV_DIM), 1.0),
        'state_gla': nrm((N_EVEN, DEC_BATCH, GLA_HEADS, GLA_DK, GLA_DV), 0.1),
        'cache_dil_w128': nrm((N_ODD, DEC_BATCH, bufl[0], 2, DIL_HPG, DIL_HD), 1.0),
        'cache_dil_w512': nrm((N_ODD, DEC_BATCH, bufl[1], 2, DIL_HPG, DIL_HD), 1.0),
        'cache_dil_w2048': nrm((N_ODD, DEC_BATCH, bufl[2], 2, DIL_HPG, DIL_HD), 1.0),
        'page_table': page_table,
        'w_in_a': nrm((N_EVEN, D_MODEL, IN_A), D_MODEL ** -0.5),
        'mla_q_norm': 1.0 + nrm((N_EVEN, MLA_Q_LORA), 0.02),
        'mla_w_uq': nrm((N_EVEN, MLA_Q_LORA, MLA_HEADS * (MLA_NOPE + MLA_ROPE)), MLA_Q_LORA ** -0.5),
        'mla_kv_norm': 1.0 + nrm((N_EVEN, MLA_KV_LORA), 0.02),
        'mla_w_uk': nrm((N_EVEN, MLA_KV_LORA, MLA_HEADS, MLA_NOPE), MLA_KV_LORA ** -0.5),
        'mla_w_uv': nrm((N_EVEN, MLA_KV_LORA, MLA_HEADS, MLA_V), MLA_KV_LORA ** -0.5),
        'gla_w_gate2': nrm((N_EVEN, GLA_GATE_RANK, GLA_HEADS * GLA_DK), GLA_GATE_RANK ** -0.5),
        'gla_b_gate': nrm((N_EVEN, GLA_HEADS * GLA_DK), 0.1),
        'gla_norm': 1.0 + nrm((N_EVEN, GLA_DV), 0.02),
        'w_out_a': nrm((N_EVEN, MIX_A, D_MODEL), MIX_A ** -0.5 * DN_BETA),
        'w_in_c': nrm((N_ODD, D_MODEL, IN_C), D_MODEL ** -0.5),
        'w_out_c': nrm((N_ODD, MIX_C, D_MODEL), MIX_C ** -0.5 * DN_BETA),
        'ffn_w_in': nrm((DEPTH, D_MODEL, 2 * D_FF), D_MODEL ** -0.5),
        'ffn_w_out': nrm((DEPTH, D_FF, D_MODEL), D_FF ** -0.5 * DN_BETA),
        'ln_g': 1.0 + nrm((DEPTH, 2, D_MODEL), 0.02),
        'ln_b': nrm((DEPTH, 2, D_MODEL), 0.02),
    }


def reference(x_prompt, x_sample, cache_mla, state_gla, cache_dil_w128, cache_dil_w512, cache_dil_w2048, page_table,
              w_in_a, mla_q_norm, mla_w_uq, mla_kv_norm, mla_w_uk, mla_w_uv, gla_w_gate2, gla_b_gate, gla_norm, w_out_a,
              w_in_c, w_out_c, ffn_w_in, ffn_w_out, ln_g, ln_b):
    dil_caches = (cache_dil_w128, cache_dil_w512, cache_dil_w2048)

    def run(x, sample):
        b_sz, l_len, _ = x.shape
        pos = (PAST_LEN if sample else 0) + jnp.arange(l_len, dtype=jnp.int32)
        mla_rows, gla_states = [], []
        dil_rows = [[] for _ in DIL_PAIRS]
        for layer in range(DEPTH):
            i = layer // 2
            if layer % 2 == 0:
                if sample:
                    past = cache_mla[i][page_table].reshape(b_sz, -1, MLA_KV_DIM)
                    s0 = state_gla[i]
                else:
                    past = None
                    s0 = jnp.zeros((b_sz, GLA_HEADS, GLA_DK, GLA_DV), F32)
                mix, rows, s_fin = even_mixer(x, pos, past, s0, w_in_a[i], mla_q_norm[i], mla_w_uq[i], mla_kv_norm[i],
                                              mla_w_uk[i], mla_w_uv[i], gla_w_gate2[i], gla_b_gate[i], gla_norm[i],
                                              w_out_a[i])
                mla_rows.append(rows)
                gla_states.append(s_fin)
            else:
                bufs = tuple(c[i] for c in dil_caches) if sample else None
                mix, rows = odd_mixer(x, bufs, w_in_c[i], w_out_c[i])
                for g in range(DIL_GROUPS):
                    dil_rows[g].append(rows[g])
            x = layer_norm(DN_ALPHA * x + mix, ln_g[layer, 0], ln_b[layer, 0])
            x = layer_norm(DN_ALPHA * x + swiglu(x, ffn_w_in[layer], ffn_w_out[layer]), ln_g[layer, 1], ln_b[layer, 1])
        return (x, jnp.stack(mla_rows), jnp.stack(gla_states), jnp.stack(dil_rows[0]), jnp.stack(dil_rows[1]),
                jnp.stack(dil_rows[2]))

    y_p, mla_p, gla_p, d128_p, d512_p, d2048_p = run(x_prompt, False)
    y_s, mla_s, gla_s, d128_s, d512_s, d2048_s = run(x_sample, True)
    return (y_p, y_s, mla_p, mla_s, gla_p, gla_s, d128_p, d128_s, d512_p, d512_s, d2048_p, d2048_s)
```

```python
import functools
import math

import numpy as np
import jax
import jax.numpy as jnp
from jax import lax
from jax.experimental import pallas as pl
from jax.experimental.pallas import tpu as pltpu

F32 = jnp.float32
BF16 = jnp.bfloat16

D_MODEL = 1024
DEPTH = 4
PAST_LEN = 16384
PAGE_SIZE = 128
MLA_HEADS = 8
MLA_Q_LORA = 384
MLA_KV_LORA = 256
MLA_NOPE = 64
MLA_ROPE = 32
MLA_V = 64
MLA_KV_DIM = MLA_KV_LORA + MLA_ROPE
ROPE_BASE = 10000.0
GLA_HEADS = 4
GLA_DK = 64
GLA_DV = 128
GLA_GATE_RANK = 16
GLA_TAU = 16.0
GLA_CHUNK = 64
DIL_PAIRS = ((128, 1), (512, 4), (2048, 16))
DIL_GROUPS = 3
DIL_HPG = 8
DIL_HD = 64
DIL_STEPS = 128
D_FF = -(-8 * D_MODEL // (3 * 256)) * 256
DN_ALPHA = (2.0 * DEPTH) ** 0.25
IN_A_PAD = 2304
MIX_C = DIL_HPG * DIL_HD
IN_C = 3 * DIL_GROUPS * MIX_C
MLA_SCALE = (MLA_NOPE + MLA_ROPE) ** -0.5

V7X_VMEM_BYTES = 64 * 1024 * 1024
VMEM_LIMIT = 56 * 1024 * 1024
LANES = 128

NEG_INF = float("-inf")


def _cparams(sem, limit=None):
    return pltpu.CompilerParams(dimension_semantics=sem, vmem_limit_bytes=limit)


def _resident(shape):
    nd = len(shape)
    return pl.BlockSpec(shape, lambda *_: (0,) * nd, pipeline_mode=pl.Buffered(1))


def _layer_norm(z, g, b):
    mu = jnp.mean(z, axis=-1, keepdims=True)
    zc = z - mu
    var = jnp.mean(zc * zc, axis=-1, keepdims=True)
    return zc * lax.rsqrt(var + 1e-5) * g + b


def _rms_norm(z, g):
    return z * lax.rsqrt(jnp.mean(z * z, axis=-1, keepdims=True) + 1e-6) * g


def _silu(z):
    return z * (1.0 / (1.0 + jnp.exp(-z)))


def _dot(a, b):
    return jnp.dot(a, b, preferred_element_type=F32)


def _dot_nt(a, b):
    return lax.dot_general(a, b, (((1,), (1,)), ((), ())), preferred_element_type=F32)


def _dot_tn(a, b):
    return lax.dot_general(a, b, (((0,), (0,)), ((), ())), preferred_element_type=F32)


def _proj_kernel(x_ref, w_ref, o_ref):
    o_ref[...] = _dot(x_ref[...].astype(BF16), w_ref[...]).astype(o_ref.dtype)


def _proj(x, w, tm, out_dtype=F32):
    m, k = x.shape
    n = w.shape[1]
    tm = min(tm, m)
    return pl.pallas_call(
        _proj_kernel,
        grid=(m // tm,),
        in_specs=[pl.BlockSpec((tm, k), lambda i: (i, 0)), _resident((k, n))],
        out_specs=pl.BlockSpec((tm, n), lambda i: (i, 0)),
        out_shape=jax.ShapeDtypeStruct((m, n), out_dtype),
        compiler_params=_cparams(("parallel",), VMEM_LIMIT),
        name="proj",
    )(x, w)


def _mix_ln_kernel(*refs, n_parts):
    parts = refs[:n_parts]
    w_ref, x_ref, g_ref, b_ref, o_ref = refs[n_parts:]
    a = jnp.concatenate([p[...].astype(BF16) for p in parts], axis=-1) if n_parts > 1 else parts[0][...].astype(BF16)
    z = DN_ALPHA * x_ref[...] + _dot(a, w_ref[...])
    o_ref[...] = _layer_norm(z, g_ref[...], b_ref[...])


def _mix_ln(parts, w, x, g, b, tm):
    m = x.shape[0]
    tm = min(tm, m)
    kern = functools.partial(_mix_ln_kernel, n_parts=len(parts))
    return pl.pallas_call(
        kern,
        grid=(m // tm,),
        in_specs=[pl.BlockSpec((tm, p.shape[1]), lambda i: (i, 0)) for p in parts]
        + [_resident(w.shape), pl.BlockSpec((tm, D_MODEL), lambda i: (i, 0)), _resident(g.shape), _resident(b.shape)],
        out_specs=pl.BlockSpec((tm, D_MODEL), lambda i: (i, 0)),
        out_shape=jax.ShapeDtypeStruct((m, D_MODEL), F32),
        compiler_params=_cparams(("parallel",), VMEM_LIMIT),
        name="mix_ln",
    )(*parts, w, x, g, b)


FF_CHUNK = 1408


def _ffn_kernel(x_ref, wi_ref, wo_ref, g_ref, b_ref, o_ref, act_ref):
    x = x_ref[...]
    xb = x.astype(BF16)
    for c in range(D_FF // FF_CHUNK):
        lo = c * FF_CHUNK
        gate = _dot(xb, wi_ref[:, lo:lo + FF_CHUNK])
        up = _dot(xb, wi_ref[:, D_FF + lo:D_FF + lo + FF_CHUNK])
        act_ref[:, lo:lo + FF_CHUNK] = (_silu(gate) * up).astype(BF16)
    z = DN_ALPHA * x + _dot(act_ref[...], wo_ref[...])
    o_ref[...] = _layer_norm(z, g_ref[...], b_ref[...])


def _ffn(x, wi, wo, g, b, tm):
    m = x.shape[0]
    tm = min(tm, m)
    return pl.pallas_call(
        _ffn_kernel,
        grid=(m // tm,),
        in_specs=[pl.BlockSpec((tm, D_MODEL), lambda i: (i, 0)), _resident(wi.shape), _resident(wo.shape),
                  _resident(g.shape), _resident(b.shape)],
        out_specs=pl.BlockSpec((tm, D_MODEL), lambda i: (i, 0)),
        out_shape=jax.ShapeDtypeStruct((m, D_MODEL), F32),
        scratch_shapes=[pltpu.VMEM((tm, D_FF), BF16)],
        compiler_params=_cparams(("parallel",), VMEM_LIMIT),
        name="ffn",
    )(x, wi, wo, g, b)


def _rope_tables(pos):
    half = MLA_ROPE // 2
    inv = ROPE_BASE ** (-jnp.arange(half, dtype=F32) / half)
    ang = pos.astype(F32)[:, None] * inv[None, :]
    cos, sin = jnp.cos(ang), jnp.sin(ang)
    n = pos.shape[0]
    z = lambda w: jnp.zeros((n, w), F32)
    cc = jnp.concatenate([cos, cos], axis=1)
    ss = jnp.concatenate([-sin, sin], axis=1)
    cq = jnp.concatenate([jnp.ones((n, MLA_NOPE), F32), cc, z(32)], axis=1)
    sq = jnp.concatenate([z(MLA_NOPE), ss, z(32)], axis=1)
    ck = jnp.concatenate([cc, z(96)], axis=1)
    sk = jnp.concatenate([ss, z(96)], axis=1)
    return cq, sq, ck, sk


def _roll_lanes_left(x, k):
    return pltpu.roll(x, x.shape[-1] - k, axis=x.ndim - 1)


def _mla_pre_kernel(h_ref, qn_ref, kvn_ref, wq_ref, wuk_ref, wuv_ref, cq_ref, sq_ref, ck_ref, sk_ref,
                    rows_ref, q_ref, k_ref, v_ref):
    h = h_ref[...]
    qn = _rms_norm(h[:, :MLA_Q_LORA], qn_ref[...]).astype(BF16)
    q = _dot(qn, wq_ref[...])
    cq = jnp.concatenate([cq_ref[...]] * MLA_HEADS, axis=1)
    sq = jnp.concatenate([sq_ref[...]] * MLA_HEADS, axis=1)
    q_ref[...] = (q * cq + _roll_lanes_left(q, 32) * sq).astype(BF16)
    ckv = _rms_norm(h[:, MLA_Q_LORA:MLA_Q_LORA + MLA_KV_LORA], kvn_ref[...])
    blk = h[:, 640:768]
    kpe = blk * ck_ref[...] + _roll_lanes_left(blk, 32) * sk_ref[...]
    rows_ref[:, :MLA_KV_LORA] = ckv
    rows_ref[:, MLA_KV_LORA:] = kpe[:, :MLA_ROPE]
    ckvb = ckv.astype(BF16)
    kpe_mid = pltpu.roll(kpe, 64, axis=1)
    k_ref[...] = (_dot(ckvb, wuk_ref[...]) + jnp.concatenate([kpe_mid] * MLA_HEADS, axis=1)).astype(BF16)
    v_ref[...] = _dot(ckvb, wuv_ref[...]).astype(BF16)


def _mla_pre(h, qn, kvn, wq, wuk, wuv, tabs, seq, tm):
    m = h.shape[0]
    nblk = seq // tm
    tab = pl.BlockSpec((tm, LANES), lambda i: (i % nblk, 0))
    row = lambda w: pl.BlockSpec((tm, w), lambda i: (i, 0))
    return pl.pallas_call(
        _mla_pre_kernel,
        grid=(m // tm,),
        in_specs=[pl.BlockSpec((tm, 768), lambda i: (i, 0)), _resident(qn.shape), _resident(kvn.shape),
                  _resident(wq.shape), _resident(wuk.shape), _resident(wuv.shape), tab, tab, tab, tab],
        out_specs=[row(MLA_KV_DIM), row(1024), row(1024), row(512)],
        out_shape=[jax.ShapeDtypeStruct((m, MLA_KV_DIM), F32), jax.ShapeDtypeStruct((m, 1024), BF16),
                   jax.ShapeDtypeStruct((m, 1024), BF16), jax.ShapeDtypeStruct((m, 512), BF16)],
        compiler_params=_cparams(("parallel",), VMEM_LIMIT),
        name="mla_pre",
    )(h, qn, kvn, wq, wuk, wuv, *tabs)


MLA_TQ = 256


def _mla_attn_kernel(q_ref, k_ref, v_ref, o_ref):
    qi = pl.program_id(2)
    tq = MLA_TQ
    lane = lax.broadcasted_iota(jnp.int32, (tq, LANES), 1)
    row = lax.broadcasted_iota(jnp.int32, (tq, tq), 0)
    col = lax.broadcasted_iota(jnp.int32, (tq, tq), 1)
    outs = []
    for j in range(2):
        qh = q_ref[:, j * LANES:(j + 1) * LANES]

        def scores(kb):
            kk = k_ref[pl.ds(pl.multiple_of(kb * tq, tq), tq), j * LANES:(j + 1) * LANES]
            return _dot_nt(qh, kk) * MLA_SCALE

        def update(kb, s, carry):
            m, l, acc = carry
            m_new = jnp.maximum(m, jnp.max(s, axis=-1, keepdims=True))
            alpha = jnp.exp(m - m_new)
            p = jnp.exp(s - m_new)
            vv = v_ref[pl.ds(pl.multiple_of(kb * tq, tq), tq), :]
            return (m_new, alpha * l + jnp.sum(p, axis=-1, keepdims=True),
                    alpha * acc + _dot(p.astype(BF16), vv))

        init = (jnp.full((tq, 1), NEG_INF, F32), jnp.zeros((tq, 1), F32), jnp.zeros((tq, LANES), F32))
        carry = lax.fori_loop(0, qi, lambda kb, c: update(kb, scores(kb), c), init)
        s = jnp.where(row >= col, scores(qi), NEG_INF)
        _, l, acc = update(qi, s, carry)
        outs.append(acc / l)
    o_ref[...] = jnp.where(lane < MLA_V, outs[0], outs[1]).astype(BF16)


def _mla_attn(q, k, v, batch, seq):
    nq = seq // MLA_TQ
    return pl.pallas_call(
        _mla_attn_kernel,
        grid=(batch, MLA_HEADS // 2, nq),
        in_specs=[pl.BlockSpec((MLA_TQ, 256), lambda b, p, i: (b * nq + i, p)),
                  pl.BlockSpec((seq, 256), lambda b, p, i: (b, p)),
                  pl.BlockSpec((seq, LANES), lambda b, p, i: (b, p))],
        out_specs=pl.BlockSpec((MLA_TQ, LANES), lambda b, p, i: (b * nq + i, p)),
        out_shape=jax.ShapeDtypeStruct((batch * seq, MLA_HEADS * MLA_V), BF16),
        compiler_params=_cparams(("parallel", "parallel", "arbitrary"), VMEM_LIMIT),
        name="mla_attn",
    )(q, k, v)


GLA_TB = 256
GLA_SLOTS = 12
HK = GLA_HEADS * GLA_DK
HV = GLA_HEADS * GLA_DV


def _gla_constants():
    c = GLA_CHUNK
    t = np.arange(c)[:, None]
    s = np.arange(c)[None, :]
    slots = []
    zero = np.zeros((c, c), bool)
    for p in (1, 2, 3):
        slots.append((t // 16 == p) & (s < 16 * p))
    slots.append(zero)
    for p in (1, 2, 3):
        slots.append((t // 16 == s // 16) & ((t % 16) // 4 == p) & (s % 16 < 4 * p))
    slots.append(zero)
    for d in (1, 2, 3):
        slots.append((t == s + d) & (t // 4 == s // 4))
    slots.append(t == s)
    mask = np.concatenate(slots, axis=1).astype(np.float32)
    assert (sum(slots) == (t >= s)).all()
    mask = np.tile(mask, (GLA_HEADS, 1))
    tri = (t >= s).astype(np.float32)
    hm = np.kron(np.eye(GLA_HEADS), np.ones((c, GLA_DK))).astype(np.float32)
    bd = np.kron(np.eye(GLA_HEADS), np.ones((GLA_DV, GLA_DK))).astype(np.float32)
    return (jnp.asarray(tri, BF16), jnp.asarray(hm, F32), jnp.asarray(mask, F32), jnp.asarray(bd, F32))


def _log_sigmoid(x):
    return jnp.minimum(x, 0.0) - jnp.log1p(jnp.exp(-jnp.abs(x)))


def _split3(x):
    hi = x.astype(BF16)
    r1 = x - hi.astype(F32)
    mid = r1.astype(BF16)
    lo = (r1 - mid.astype(F32)).astype(BF16)
    return hi, mid, lo


def _bc_rows(row, n):
    return jnp.broadcast_to(row, (n, row.shape[1]))


def _gla_kernel(blk_ref, q_ref, v_ref, r_ref, k_ref, wg_ref, bg_ref, gn_ref, tri_ref, hm_ref, mask_ref, bd_ref,
                o_ref, st_ref, st_scr):
    c = GLA_CHUNK

    @pl.when(pl.program_id(1) == 0)
    def _():
        st_scr[...] = jnp.zeros_like(st_scr)

    rows = lax.broadcasted_iota(jnp.int32, (c, HK), 0)
    rm4 = rows & 3

    def chunk(ci, _):
        sl = pl.ds(pl.multiple_of(ci * c, c), c)
        pre = _dot(blk_ref[sl, :].astype(BF16), wg_ref[...]) + bg_ref[...]
        g = _log_sigmoid(pre) * (1.0 / GLA_TAU)
        cum3 = _dot(tri_ref[...], jnp.concatenate(_split3(g), axis=1))
        cum = cum3[:, :HK] + cum3[:, HK:2 * HK] + cum3[:, 2 * HK:]
        q = q_ref[sl, :] * (GLA_DK ** -0.5)
        k = k_ref[sl, :]
        vb = v_ref[sl, :].astype(BF16)

        def kvar(e):
            return (k * jnp.exp(jnp.minimum(e, 0.0))).astype(BF16)

        zeros_k = jnp.zeros((c, HK), BF16)
        bnd = [cum[16 * p - 1:16 * p, :] for p in (1, 2, 3)]
        base1 = jnp.concatenate([jnp.zeros((16, HK), F32)] + [_bc_rows(b, 16) for b in bnd], axis=0)
        q1 = q * jnp.exp(cum - base1)
        k1 = jnp.concatenate([kvar(_bc_rows(b, c) - cum) for b in bnd] + [zeros_k], axis=0)
        sh = [pltpu.roll(cum, d, axis=0) for d in (1, 2, 3, 4)]
        base2 = jnp.where(rm4 == 0, sh[0], jnp.where(rm4 == 1, sh[1], jnp.where(rm4 == 2, sh[2], sh[3])))
        base2 = jnp.where(rows < 4, 0.0, base2)
        q2 = q * jnp.exp(cum - base2)
        k2 = []
        for p in (1, 2, 3):
            bk = jnp.concatenate([_bc_rows(cum[16 * i + 4 * p - 1:16 * i + 4 * p, :], 16) for i in range(4)], axis=0)
            k2.append(kvar(bk - cum))
        k2 = jnp.concatenate(k2 + [zeros_k], axis=0)
        k3 = jnp.concatenate([kvar(pltpu.roll(cum, c - d, axis=0) - cum) for d in (1, 2, 3)] + [k.astype(BF16)],
                             axis=0)
        hm = hm_ref[...]

        def stack(x):
            return (jnp.concatenate([x] * GLA_HEADS, axis=0) * hm).astype(BF16)

        p_all = jnp.concatenate([_dot_nt(stack(q1), k1), _dot_nt(stack(q2), k2), _dot_nt(stack(q), k3)], axis=1)
        pm = (p_all * mask_ref[...]).astype(BF16)
        st = st_scr[...]
        o = _dot_nt((q * jnp.exp(cum)).astype(BF16), st.astype(BF16))
        intra = []
        for h in range(GLA_HEADS):
            vh = vb[:, h * GLA_DV:(h + 1) * GLA_DV]
            intra.append(_dot(pm[h * c:(h + 1) * c, :], jnp.concatenate([vh] * GLA_SLOTS, axis=0)))
        o = o + jnp.concatenate(intra, axis=1)
        last = cum[c - 1:c, :]
        st_scr[...] = st * jnp.exp(last) + _dot_tn(vb, kvar(_bc_rows(last, c) - cum)) * bd_ref[...]
        r = r_ref[sl, :]
        outs = []
        for h in range(GLA_HEADS):
            lanes = slice(h * GLA_DV, (h + 1) * GLA_DV)
            outs.append(_rms_norm(o[:, lanes], gn_ref[...]) * _silu(r[:, lanes]))
        o_ref[sl, :] = jnp.concatenate(outs, axis=1).astype(BF16)
        return 0

    lax.fori_loop(0, GLA_TB // c, chunk, 0)

    @pl.when(pl.program_id(1) == pl.num_programs(1) - 1)
    def _():
        for h in range(GLA_HEADS):
            blk = st_scr[h * GLA_DV:(h + 1) * GLA_DV, :]
            st_ref[0, h] = blk.T[h * GLA_DK:(h + 1) * GLA_DK, :]


def _gla(h, wg, bg, gn, consts, batch, seq):
    nt = seq // GLA_TB
    tri, hm, mask, bd = consts
    rowblk = lambda w, cb: pl.BlockSpec((GLA_TB, w), lambda b, t: (b * nt + t, cb))
    return pl.pallas_call(
        _gla_kernel,
        grid=(batch, nt),
        in_specs=[rowblk(LANES, 5), rowblk(HK, 3), rowblk(HV, 2), rowblk(HV, 3), rowblk(HK, 8),
                  _resident(wg.shape), _resident(bg.shape), _resident(gn.shape),
                  _resident(tri.shape), _resident(hm.shape), _resident(mask.shape), _resident(bd.shape)],
        out_specs=[pl.BlockSpec((GLA_TB, HV), lambda b, t: (b * nt + t, 0)),
                   pl.BlockSpec((1, GLA_HEADS, GLA_DK, GLA_DV), lambda b, t: (b, 0, 0, 0))],
        out_shape=[jax.ShapeDtypeStruct((batch * seq, HV), BF16),
                   jax.ShapeDtypeStruct((batch, GLA_HEADS, GLA_DK, GLA_DV), F32)],
        scratch_shapes=[pltpu.VMEM((HV, HK), F32)],
        compiler_params=_cparams(("parallel", "arbitrary"), VMEM_LIMIT),
        name="gla",
    )(h, h, h, h, h, wg, bg, gn, tri, hm, mask, bd)


def _alibi_slope(group, head):
    return 2.0 ** (-8.0 * (group * DIL_HPG + head + 1) / (DIL_GROUPS * DIL_HPG))


def _dil_attn_kernel(q_ref, kc_ref, vc_ref, *rest, group, dil, has_prev):
    if has_prev:
        kp_ref, vp_ref, o_ref, lse_ref = rest
    else:
        o_ref, lse_ref = rest
    n = DIL_STEPS
    nb = pl.program_id(2)
    nk = 2 * n if has_prev else n
    qi = lax.broadcasted_iota(jnp.int32, (n, nk), 0)
    ki = lax.broadcasted_iota(jnp.int32, (n, nk), 1)
    steps = qi + (n if has_prev else 0) - ki
    valid = (steps >= 0) & (steps <= n)
    if has_prev:
        valid = valid & ((ki >= n) | (nb > 0))
    dist = (steps * dil).astype(F32)
    lane = lax.broadcasted_iota(jnp.int32, (n, LANES), 1)
    low = lane < DIL_HD
    for j in range(DIL_HPG // 2):
        cols = slice(j * LANES, (j + 1) * LANES)
        q2 = q_ref[0, :, cols]
        if has_prev:
            k2 = jnp.concatenate([kp_ref[0, :, cols], kc_ref[0, :, cols]], axis=0).astype(BF16)
            v2 = jnp.concatenate([vp_ref[0, :, cols], vc_ref[0, :, cols]], axis=0).astype(BF16)
        else:
            k2 = kc_ref[0, :, cols].astype(BF16)
            v2 = vc_ref[0, :, cols].astype(BF16)
        outs, lses = [], []
        for par in range(2):
            qh = jnp.where(low if par == 0 else ~low, q2, 0.0).astype(BF16)
            s = _dot_nt(qh, k2) * (DIL_HD ** -0.5) - _alibi_slope(group, 2 * j + par) * dist
            s = jnp.where(valid, s, NEG_INF)
            m = jnp.max(s, axis=-1, keepdims=True)
            p = jnp.exp(s - m)
            l = jnp.sum(p, axis=-1, keepdims=True)
            outs.append(_dot(p.astype(BF16), v2) / l)
            lses.append(jnp.broadcast_to(m + jnp.log(l), (n, LANES)))
        o_ref[0, :, cols] = jnp.where(low, outs[0], outs[1])
        lse_ref[0, :, cols] = jnp.where(low, lses[0], lses[1])


def _dil_attn(qkv, group, batch, seq):
    window, dil = DIL_PAIRS[group]
    n = DIL_STEPS
    ls = seq // dil
    nblk = ls // n
    has_prev = nblk > 1
    ncb = IN_C // MIX_C
    x = qkv.reshape(batch, ls, dil * IN_C)
    cur = lambda cb: pl.BlockSpec((1, n, MIX_C), lambda b, r, i: (b, i, r * ncb + cb))
    prev = lambda cb: pl.BlockSpec((1, n, MIX_C), lambda b, r, i: (b, jnp.maximum(i - 1, 0), r * ncb + cb))
    kcb, vcb = 3 + 2 * group, 4 + 2 * group
    in_specs = [cur(group), cur(kcb), cur(vcb)]
    args = [x, x, x]
    if has_prev:
        in_specs += [prev(kcb), prev(vcb)]
        args += [x, x]
    out = pl.BlockSpec((1, n, MIX_C), lambda b, r, i: (b, i, r))
    o, lse = pl.pallas_call(
        functools.partial(_dil_attn_kernel, group=group, dil=dil, has_prev=has_prev),
        grid=(batch, dil, nblk),
        in_specs=in_specs,
        out_specs=[out, out],
        out_shape=[jax.ShapeDtypeStruct((batch, ls, dil * MIX_C), F32)] * 2,
        compiler_params=_cparams(("parallel", "parallel", "arbitrary"), VMEM_LIMIT),
        name="dil_attn%d" % group,
    )(*args)
    return o.reshape(batch * seq, MIX_C), lse.reshape(batch * seq, MIX_C)


def _dil_mix_ln_kernel(o1, o2, o3, l1, l2, l3, w_ref, x_ref, g_ref, b_ref, o_ref):
    ls = [l1[...], l2[...], l3[...]]
    mx = jnp.maximum(jnp.maximum(ls[0], ls[1]), ls[2])
    es = [jnp.exp(l - mx) for l in ls]
    mix = (es[0] * o1[...] + es[1] * o2[...] + es[2] * o3[...]) / (es[0] + es[1] + es[2])
    z = DN_ALPHA * x_ref[...] + _dot(mix.astype(BF16), w_ref[...])
    o_ref[...] = _layer_norm(z, g_ref[...], b_ref[...])


def _dil_mix_ln(os_, lses, w, x, g, b, tm):
    m = x.shape[0]
    row = lambda wd: pl.BlockSpec((tm, wd), lambda i: (i, 0))
    return pl.pallas_call(
        _dil_mix_ln_kernel,
        grid=(m // tm,),
        in_specs=[row(MIX_C)] * 6 + [_resident(w.shape), row(D_MODEL), _resident(g.shape), _resident(b.shape)],
        out_specs=row(D_MODEL),
        out_shape=jax.ShapeDtypeStruct((m, D_MODEL), F32),
        compiler_params=_cparams(("parallel",), VMEM_LIMIT),
        name="dil_mix_ln",
    )(*os_, *lses, w, x, g, b)


def _mla_pre_s_kernel(h_ref, qn_ref, kvn_ref, wqn_ref, wqp_ref, wukt_ref, ck_ref, sk_ref, rows_ref, q_ref):
    h = h_ref[...]
    qn = _rms_norm(h[:, :MLA_Q_LORA], qn_ref[...]).astype(BF16)
    qnope = _dot(qn, wqn_ref[...]).astype(BF16)
    qp = _dot(qn, wqp_ref[...])
    cq = jnp.concatenate([ck_ref[...]] * MLA_HEADS, axis=1)
    sq = jnp.concatenate([sk_ref[...]] * MLA_HEADS, axis=1)
    qp = qp * cq + _roll_lanes_left(qp, 32) * sq
    ckv = _rms_norm(h[:, MLA_Q_LORA:MLA_Q_LORA + MLA_KV_LORA], kvn_ref[...])
    blk = h[:, 640:768]
    kpe = blk * ck_ref[...] + _roll_lanes_left(blk, 32) * sk_ref[...]
    rows_ref[:, :MLA_KV_LORA] = ckv
    rows_ref[:, MLA_KV_LORA:] = kpe[:, :MLA_ROPE]
    for hd in range(MLA_HEADS):
        cols = slice(hd * LANES, (hd + 1) * LANES)
        q_ref[hd, :, :MLA_KV_LORA] = _dot(qnope[:, cols], wukt_ref[hd])
        q_ref[hd, :, MLA_KV_LORA:] = qp[:, hd * LANES:hd * LANES + MLA_ROPE]


def _mla_pre_s(h, qn, kvn, wqn, wqp, wukt, tabs):
    m = h.shape[0]
    full = lambda a: _resident(a.shape)
    return pl.pallas_call(
        _mla_pre_s_kernel,
        grid=(1,),
        in_specs=[pl.BlockSpec((m, 768), lambda i: (0, 0)), full(qn), full(kvn), full(wqn), full(wqp), full(wukt)]
        + [full(t) for t in tabs[2:]],
        out_specs=[pl.BlockSpec((m, MLA_KV_DIM), lambda i: (0, 0)),
                   pl.BlockSpec((MLA_HEADS, m, MLA_KV_DIM), lambda i: (0, 0, 0))],
        out_shape=[jax.ShapeDtypeStruct((m, MLA_KV_DIM), F32), jax.ShapeDtypeStruct((MLA_HEADS, m, MLA_KV_DIM), F32)],
        compiler_params=_cparams(("arbitrary",), VMEM_LIMIT),
        name="mla_pre_s",
    )(h, qn, kvn, wqn, wqp, wukt, *tabs[2:])


DEC_PAGES = 16


def _mla_dec_kernel(pt_ref, q_ref, new_ref, *rest):
    pages = rest[:DEC_PAGES]
    o_ref, m_scr, l_scr, acc_scr = rest[DEC_PAGES:]
    step = pl.program_id(1)
    q = q_ref[0]
    qb = q.astype(BF16)

    @pl.when(step == 0)
    def _():
        new = new_ref[0]
        m_scr[...] = jnp.sum(q * new, axis=-1, keepdims=True) * MLA_SCALE
        l_scr[...] = jnp.ones_like(l_scr)
        acc_scr[...] = jnp.broadcast_to(new[:, :MLA_KV_LORA], acc_scr.shape)

    kvs = [p[0].astype(BF16) for p in pages]
    s = jnp.concatenate([_dot_nt(qb, kv) for kv in kvs], axis=1) * MLA_SCALE
    m_old = m_scr[...]
    m_new = jnp.maximum(m_old, jnp.max(s, axis=-1, keepdims=True))
    alpha = jnp.exp(m_old - m_new)
    p = jnp.exp(s - m_new)
    pb = p.astype(BF16)
    acc = alpha * acc_scr[...]
    for j, kv in enumerate(kvs):
        acc = acc + _dot(pb[:, j * PAGE_SIZE:(j + 1) * PAGE_SIZE], kv[:, :MLA_KV_LORA])
    m_scr[...] = m_new
    l_scr[...] = alpha * l_scr[...] + jnp.sum(p, axis=-1, keepdims=True)
    acc_scr[...] = acc

    @pl.when(step == pl.num_programs(1) - 1)
    def _():
        o_ref[0] = acc_scr[...] / l_scr[...]


def _mla_dec(page_table, q_abs, rows_new, cache):
    batch = q_abs.shape[0]
    n_pages = page_table.shape[1]
    steps = n_pages // DEC_PAGES
    pt = page_table.reshape(-1)

    def page_spec(j):
        return pl.BlockSpec((1, PAGE_SIZE, MLA_KV_DIM),
                            lambda b, s, pt_ref: (pt_ref[b * n_pages + s * DEC_PAGES + j], 0, 0))

    grid_spec = pltpu.PrefetchScalarGridSpec(
        num_scalar_prefetch=1,
        grid=(batch, steps),
        in_specs=[pl.BlockSpec((1, MLA_HEADS, MLA_KV_DIM), lambda b, s, pt_ref: (b, 0, 0)),
                  pl.BlockSpec((1, 1, MLA_KV_DIM), lambda b, s, pt_ref: (b, 0, 0))]
        + [page_spec(j) for j in range(DEC_PAGES)],
        out_specs=pl.BlockSpec((1, MLA_HEADS, MLA_KV_LORA), lambda b, s, pt_ref: (b, 0, 0)),
        scratch_shapes=[pltpu.VMEM((MLA_HEADS, 1), F32), pltpu.VMEM((MLA_HEADS, 1), F32),
                        pltpu.VMEM((MLA_HEADS, MLA_KV_LORA), F32)],
    )
    return pl.pallas_call(
        _mla_dec_kernel,
        grid_spec=grid_spec,
        out_shape=jax.ShapeDtypeStruct((batch, MLA_HEADS, MLA_KV_LORA), F32),
        compiler_params=_cparams(("parallel", "arbitrary"), VMEM_LIMIT),
        name="mla_dec",
    )(pt, q_abs, rows_new.reshape(batch, 1, MLA_KV_DIM), *([cache] * DEC_PAGES))


def _gla_dec_kernel(blk_ref, q_ref, v_ref, r_ref, k_ref, s_ref, wg_ref, bg_ref, gn_ref, o_ref, so_ref):
    pre = _dot(blk_ref[0].astype(BF16), wg_ref[...]) + bg_ref[...]
    a = jnp.exp(_log_sigmoid(pre) * (1.0 / GLA_TAU))
    q = q_ref[0] * (GLA_DK ** -0.5)
    k = k_ref[0]
    v = v_ref[0]
    r = r_ref[0]
    eye = (lax.broadcasted_iota(jnp.int32, (GLA_DK, GLA_DK), 0)
           == lax.broadcasted_iota(jnp.int32, (GLA_DK, GLA_DK), 1))
    outs = []
    for h in range(GLA_HEADS):
        kl = slice(h * GLA_DK, (h + 1) * GLA_DK)
        vl = slice(h * GLA_DV, (h + 1) * GLA_DV)
        diag = lambda row: jnp.where(eye, jnp.broadcast_to(row, (GLA_DK, GLA_DK)), 0.0)
        lhs = jnp.concatenate([diag(a[:, kl]), diag(k[:, kl])], axis=1)
        rhs = jnp.concatenate([s_ref[0, h], jnp.broadcast_to(v[:, vl], (GLA_DK, GLA_DV))], axis=0)
        s_new = jnp.dot(lhs, rhs, preferred_element_type=F32, precision=lax.Precision.HIGHEST)
        so_ref[0, h] = s_new
        o = jnp.dot(jnp.broadcast_to(q[:, kl], (8, GLA_DK)), s_new, preferred_element_type=F32,
                    precision=lax.Precision.HIGHEST)[:1]
        outs.append(_rms_norm(o, gn_ref[...]) * _silu(r[:, vl]))
    o_ref[0] = jnp.concatenate(outs, axis=1)


def _gla_dec(h, state, wg, bg, gn):
    batch = h.shape[0]
    x = h.reshape(batch, 1, IN_A_PAD)
    rowblk = lambda w, cb: pl.BlockSpec((1, 1, w), lambda b: (b, 0, cb))
    st = pl.BlockSpec((1, GLA_HEADS, GLA_DK, GLA_DV), lambda b: (b, 0, 0, 0))
    o, s_new = pl.pallas_call(
        _gla_dec_kernel,
        grid=(batch,),
        in_specs=[rowblk(LANES, 5), rowblk(HK, 3), rowblk(HV, 2), rowblk(HV, 3), rowblk(HK, 8), st,
                  _resident(wg.shape), _resident(bg.shape), _resident(gn.shape)],
        out_specs=[pl.BlockSpec((1, 1, HV), lambda b: (b, 0, 0)), st],
        out_shape=[jax.ShapeDtypeStruct((batch, 1, HV), F32), jax.ShapeDtypeStruct(state.shape, F32)],
        compiler_params=_cparams(("parallel",), VMEM_LIMIT),
        name="gla_dec",
    )(x, x, x, x, x, state, wg, bg, gn)
    return o.reshape(batch, HV), s_new


def _dil_dec_kernel(x_ref, c1_ref, c2_ref, c3_ref, o_ref):
    n = DIL_STEPS
    hrow = lax.broadcasted_iota(jnp.int32, (DIL_HPG, MIX_C), 0)
    hlane = lax.broadcasted_iota(jnp.int32, (DIL_HPG, MIX_C), 1) // DIL_HD
    hmask = hrow == hlane
    head = lax.broadcasted_iota(jnp.int32, (DIL_HPG, n), 0).astype(F32)
    back = (n - lax.broadcasted_iota(jnp.int32, (DIL_HPG, n), 1)).astype(F32)
    outs, lses = [], []
    for g, c_ref in enumerate((c1_ref, c2_ref, c3_ref)):
        dil = DIL_PAIRS[g][1]
        q = x_ref[0, :, g * MIX_C:(g + 1) * MIX_C]
        base = IN_C // 3 + 2 * g * MIX_C
        k_new = x_ref[0, :, base:base + MIX_C]
        v_new = x_ref[0, :, base + MIX_C:base + 2 * MIX_C]
        qrows = jnp.where(hmask, jnp.broadcast_to(q, (DIL_HPG, MIX_C)), 0.0)
        kc = c_ref[0, :, :MIX_C].astype(BF16)
        vc = c_ref[0, :, MIX_C:2 * MIX_C].astype(BF16)
        slope = jnp.exp((g * DIL_HPG + head + 1.0) * (-8.0 * math.log(2.0) / (DIL_GROUPS * DIL_HPG)))
        s = _dot_nt(qrows.astype(BF16), kc) * (DIL_HD ** -0.5) - slope * (back * dil)
        s_new = jnp.sum(qrows * k_new, axis=-1, keepdims=True) * (DIL_HD ** -0.5)
        m = jnp.maximum(jnp.max(s, axis=-1, keepdims=True), s_new)
        p = jnp.exp(s - m)
        p_new = jnp.exp(s_new - m)
        l = jnp.sum(p, axis=-1, keepdims=True) + p_new
        o8 = (_dot(p.astype(BF16), vc) + p_new * v_new) / l
        outs.append(jnp.sum(jnp.where(hmask, o8, 0.0), axis=0, keepdims=True))
        lse = jnp.broadcast_to(m + jnp.log(l), (DIL_HPG, MIX_C))
        lses.append(jnp.sum(jnp.where(hmask, lse, 0.0), axis=0, keepdims=True))
    mx = jnp.maximum(jnp.maximum(lses[0], lses[1]), lses[2])
    es = [jnp.exp(l - mx) for l in lses]
    o_ref[0] = (es[0] * outs[0] + es[1] * outs[1] + es[2] * outs[2]) / (es[0] + es[1] + es[2])


def _dil_dec(qkv, caches):
    batch = qkv.shape[0]
    x = qkv.reshape(batch, 1, IN_C)
    specs, args = [pl.BlockSpec((1, 1, IN_C), lambda b: (b, 0, 0))], [x]
    for (window, dil), c in zip(DIL_PAIRS, caches):
        assert c.shape[1] == window
        args.append(c.reshape(batch, window // dil, dil * 2 * MIX_C))
        specs.append(pl.BlockSpec((1, DIL_STEPS, 2 * MIX_C), lambda b: (b, 0, 0)))
    o = pl.pallas_call(
        _dil_dec_kernel,
        grid=(batch,),
        in_specs=specs,
        out_specs=pl.BlockSpec((1, 1, MIX_C), lambda b: (b, 0, 0)),
        out_shape=jax.ShapeDtypeStruct((batch, 1, MIX_C), F32),
        compiler_params=_cparams(("parallel",), VMEM_LIMIT),
        name="dil_dec",
    )(*args)
    return o.reshape(batch, MIX_C)


def _even_in_cols():
    o_cq, o_kv, o_gq, o_gk, o_gv, o_lr, o_gr = 0, 384, 672, 928, 1184, 1696, 1712
    r = np.arange
    kpe = o_kv + MLA_KV_LORA
    idx = np.concatenate([r(o_cq, o_cq + 384), r(o_kv, o_kv + 256), r(kpe, kpe + 32), r(kpe + 16, kpe + 32),
                          r(kpe, kpe + 16), r(o_lr, o_lr + 16), np.full(48, -1), r(o_gq, o_gq + 256),
                          r(o_gv, o_gv + 512), r(o_gr, o_gr + 512), r(o_gk, o_gk + 256)])
    assert idx.shape[0] == IN_A_PAD
    return idx


def _gather_cols(w, idx):
    return jnp.where(jnp.asarray(idx >= 0)[None, :], jnp.take(w, jnp.asarray(np.maximum(idx, 0)), axis=1), 0.0)


def _uq_cols(kind):
    idx = []
    for h in range(MLA_HEADS):
        b = h * (MLA_NOPE + MLA_ROPE)
        nope = np.arange(b, b + 64)
        x1, x2 = np.arange(b + 64, b + 80), np.arange(b + 80, b + 96)
        if kind == "full":
            idx += [nope, x1, x2, x2, x1]
        elif kind == "nope":
            idx += [nope, np.full(64, -1)]
        else:
            idx += [x1, x2, x2, x1, np.full(64, -1)]
    return np.concatenate(idx)


def _odd_in_cols():
    blk = lambda which, g: np.arange((which * DIL_GROUPS + g) * MIX_C, (which * DIL_GROUPS + g + 1) * MIX_C)
    return np.concatenate([blk(0, g) for g in range(DIL_GROUPS)]
                          + [np.concatenate([blk(1, g), blk(2, g)]) for g in range(DIL_GROUPS)])


TM_DENSE = 512
TM_WIDE = 256


def kernel(x_prompt, x_sample, cache_mla, state_gla, cache_dil_w128, cache_dil_w512, cache_dil_w2048, page_table,
           w_in_a, mla_q_norm, mla_w_uq, mla_kv_norm, mla_w_uk, mla_w_uv, gla_w_gate2, gla_b_gate, gla_norm, w_out_a,
           w_in_c, w_out_c, ffn_w_in, ffn_w_out, ln_g, ln_b):
    batch, seq, _ = x_prompt.shape
    dbatch = x_sample.shape[0]
    dil_caches = (cache_dil_w128, cache_dil_w512, cache_dil_w2048)
    gla_consts = _gla_constants()
    tabs_p = _rope_tables(jnp.arange(seq, dtype=jnp.int32))
    tabs_s = _rope_tables(jnp.full((dbatch,), PAST_LEN, jnp.int32))
    even_cols, odd_cols = _even_in_cols(), _odd_in_cols()
    row2 = lambda v: v.reshape(1, -1).astype(F32)

    xp = x_prompt.reshape(batch * seq, D_MODEL)
    xs = x_sample.reshape(dbatch, D_MODEL)
    mla_p, mla_s, gla_p, gla_s = [], [], [], []
    dil_p = [[] for _ in DIL_PAIRS]
    dil_s = [[] for _ in DIL_PAIRS]

    for layer in range(DEPTH):
        i = layer // 2
        g0, b0, g1, b1 = (row2(ln_g[layer, 0]), row2(ln_b[layer, 0]), row2(ln_g[layer, 1]), row2(ln_b[layer, 1]))
        if layer % 2 == 0:
            w_in = _gather_cols(w_in_a[i], even_cols).astype(BF16)
            wq_full = _gather_cols(mla_w_uq[i], _uq_cols("full")).astype(BF16)
            wq_nope = _gather_cols(mla_w_uq[i], _uq_cols("nope")).astype(BF16)
            wq_rope = _gather_cols(mla_w_uq[i], _uq_cols("rope")).astype(BF16)
            wuk_pad = jnp.pad(mla_w_uk[i], ((0, 0), (0, 0), (0, LANES - MLA_NOPE))).reshape(MLA_KV_LORA, -1).astype(BF16)
            wuk_t = jnp.pad(jnp.transpose(mla_w_uk[i], (1, 2, 0)), ((0, 0), (0, LANES - MLA_NOPE), (0, 0))).astype(BF16)
            wuv = mla_w_uv[i].reshape(MLA_KV_LORA, -1).astype(BF16)
            wuv_bd = (jnp.eye(MLA_HEADS, dtype=F32)[:, None, :, None]
                      * jnp.transpose(mla_w_uv[i], (1, 0, 2))[:, :, None, :]).reshape(
                          MLA_HEADS * MLA_KV_LORA, MLA_HEADS * MLA_V).astype(BF16)
            wg = jnp.pad(gla_w_gate2[i], ((64, LANES - 64 - GLA_GATE_RANK), (0, 0))).astype(BF16)
            bg, gn = row2(gla_b_gate[i]), row2(gla_norm[i])
            qn, kvn = row2(mla_q_norm[i]), row2(mla_kv_norm[i])
            w_out = w_out_a[i].astype(BF16)

            h = _proj(xp, w_in, TM_DENSE)
            rows, qf, kf, vf = _mla_pre(h, qn, kvn, wq_full, wuk_pad, wuv, tabs_p, seq, TM_DENSE)
            mla_out = _mla_attn(qf, kf, vf, batch, seq)
            gla_out, s_fin = _gla(h, wg, bg, gn, gla_consts, batch, seq)
            mla_p.append(rows.reshape(batch, seq, MLA_KV_DIM))
            gla_p.append(s_fin)
            xp = _mix_ln([mla_out, gla_out], w_out, xp, g0, b0, TM_DENSE)

            h = _proj(xs, w_in, TM_DENSE)
            rows, q_abs = _mla_pre_s(h, qn, kvn, wq_nope, wq_rope, wuk_t, tabs_s)
            lat = _mla_dec(page_table, jnp.transpose(q_abs, (1, 0, 2)), rows, cache_mla[i])
            mla_out = _proj(lat.reshape(dbatch, MLA_HEADS * MLA_KV_LORA), wuv_bd, TM_DENSE)
            gla_out, s_fin = _gla_dec(h, state_gla[i], wg, bg, gn)
            mla_s.append(rows.reshape(dbatch, 1, MLA_KV_DIM))
            gla_s.append(s_fin)
            xs = _mix_ln([mla_out, gla_out], w_out, xs, g0, b0, TM_DENSE)
        else:
            w_in = jnp.take(w_in_c[i], jnp.asarray(odd_cols), axis=1).astype(BF16)
            w_out = w_out_c[i].astype(BF16)
            kv0 = IN_C // 3

            qkv = _proj(xp, w_in, TM_WIDE)
            os_, lses = zip(*[_dil_attn(qkv, g, batch, seq) for g in range(DIL_GROUPS)])
            q3 = qkv.reshape(batch, seq, IN_C)
            for g, (window, _) in enumerate(DIL_PAIRS):
                keep = min(window, seq)
                dil_p[g].append(q3[:, seq - keep:, kv0 + 2 * g * MIX_C:kv0 + 2 * (g + 1) * MIX_C]
                                .reshape(batch, keep, 2, DIL_HPG, DIL_HD))
            xp = _dil_mix_ln(os_, lses, w_out, xp, g0, b0, TM_DENSE)

            qkv = _proj(xs, w_in, TM_WIDE)
            mix = _dil_dec(qkv, [c[i] for c in dil_caches])
            for g in range(DIL_GROUPS):
                dil_s[g].append(qkv[:, kv0 + 2 * g * MIX_C:kv0 + 2 * (g + 1) * MIX_C]
                                .reshape(dbatch, 1, 2, DIL_HPG, DIL_HD))
            xs = _mix_ln([mix], w_out, xs, g0, b0, TM_DENSE)

        wi, wo = ffn_w_in[layer].astype(BF16), ffn_w_out[layer].astype(BF16)
        xp = _ffn(xp, wi, wo, g1, b1, TM_DENSE)
        xs = _ffn(xs, wi, wo, g1, b1, TM_DENSE)

    st = jnp.stack
    return (xp.reshape(batch, seq, D_MODEL), xs.reshape(dbatch, 1, D_MODEL), st(mla_p), st(mla_s), st(gla_p), st(gla_s),
            st(dil_p[0]), st(dil_s[0]), st(dil_p[1]), st(dil_s[1]), st(dil_p[2]), st(dil_s[2]))
```

```python
import functools
import math

import numpy as np
import jax
import jax.numpy as jnp
from jax import lax
from jax.experimental import pallas as pl
from jax.experimental.pallas import tpu as pltpu

F32 = jnp.float32
BF16 = jnp.bfloat16

D_MODEL = 1024
DEPTH = 4
PAST_LEN = 16384
PAGE_SIZE = 128
MLA_HEADS = 8
MLA_Q_LORA = 384
MLA_KV_LORA = 256
MLA_NOPE = 64
MLA_ROPE = 32
MLA_V = 64
MLA_KV_DIM = MLA_KV_LORA + MLA_ROPE
ROPE_BASE = 10000.0
GLA_HEADS = 4
GLA_DK = 64
GLA_DV = 128
GLA_GATE_RANK = 16
GLA_TAU = 16.0
GLA_CHUNK = 64
DIL_PAIRS = ((128, 1), (512, 4), (2048, 16))
DIL_GROUPS = 3
DIL_HPG = 8
DIL_HD = 64
DIL_STEPS = 128
D_FF = -(-8 * D_MODEL // (3 * 256)) * 256
DN_ALPHA = (2.0 * DEPTH) ** 0.25
IN_A_PAD = 2304
MIX_C = DIL_HPG * DIL_HD
IN_C = 3 * DIL_GROUPS * MIX_C
MLA_SCALE = (MLA_NOPE + MLA_ROPE) ** -0.5

V7X_VMEM_BYTES = 64 * 1024 * 1024
VMEM_LIMIT = 56 * 1024 * 1024
LANES = 128

NEG_INF = float("-inf")


def _cparams(sem, limit=None):
    return pltpu.CompilerParams(dimension_semantics=sem, vmem_limit_bytes=limit)


def _resident(shape):
    nd = len(shape)
    return pl.BlockSpec(shape, lambda *_: (0,) * nd, pipeline_mode=pl.Buffered(1))


def _layer_norm(z, g, b):
    mu = jnp.mean(z, axis=-1, keepdims=True)
    zc = z - mu
    var = jnp.mean(zc * zc, axis=-1, keepdims=True)
    return zc * lax.rsqrt(var + 1e-5) * g + b


def _rms_norm(z, g):
    return z * lax.rsqrt(jnp.mean(z * z, axis=-1, keepdims=True) + 1e-6) * g


def _silu(z):
    return z * (1.0 / (1.0 + jnp.exp(-z)))


def _dot(a, b):
    return jnp.dot(a, b, preferred_element_type=F32)


def _dot_nt(a, b):
    return lax.dot_general(a, b, (((1,), (1,)), ((), ())), preferred_element_type=F32)


def _dot_tn(a, b):
    return lax.dot_general(a, b, (((0,), (0,)), ((), ())), preferred_element_type=F32)


def _proj_kernel(x_ref, w_ref, o_ref):
    o_ref[...] = _dot(x_ref[...].astype(BF16), w_ref[...]).astype(o_ref.dtype)


def _proj(x, w, tm, out_dtype=F32):
    m, k = x.shape
    n = w.shape[1]
    tm = min(tm, m)
    return pl.pallas_call(
        _proj_kernel,
        grid=(m // tm,),
        in_specs=[pl.BlockSpec((tm, k), lambda i: (i, 0)), _resident((k, n))],
        out_specs=pl.BlockSpec((tm, n), lambda i: (i, 0)),
        out_shape=jax.ShapeDtypeStruct((m, n), out_dtype),
        compiler_params=_cparams(("parallel",), VMEM_LIMIT),
        name="proj",
    )(x, w)


def _mix_ln_kernel(*refs, n_parts):
    parts = refs[:n_parts]
    w_ref, x_ref, g_ref, b_ref, o_ref = refs[n_parts:]
    a = jnp.concatenate([p[...].astype(BF16) for p in parts], axis=-1) if n_parts > 1 else parts[0][...].astype(BF16)
    z = DN_ALPHA * x_ref[...] + _dot(a, w_ref[...])
    o_ref[...] = _layer_norm(z, g_ref[...], b_ref[...])


def _mix_ln(parts, w, x, g, b, tm):
    m = x.shape[0]
    tm = min(tm, m)
    kern = functools.partial(_mix_ln_kernel, n_parts=len(parts))
    return pl.pallas_call(
        kern,
        grid=(m // tm,),
        in_specs=[pl.BlockSpec((tm, p.shape[1]), lambda i: (i, 0)) for p in parts]
        + [_resident(w.shape), pl.BlockSpec((tm, D_MODEL), lambda i: (i, 0)), _resident(g.shape), _resident(b.shape)],
        out_specs=pl.BlockSpec((tm, D_MODEL), lambda i: (i, 0)),
        out_shape=jax.ShapeDtypeStruct((m, D_MODEL), F32),
        compiler_params=_cparams(("parallel",), VMEM_LIMIT),
        name="mix_ln",
    )(*parts, w, x, g, b)


FF_CHUNK = 1408


def _ffn_kernel(x_ref, wi_ref, wo_ref, g_ref, b_ref, o_ref, act_ref):
    x = x_ref[...]
    xb = x.astype(BF16)
    for c in range(D_FF // FF_CHUNK):
        lo = c * FF_CHUNK
        gate = _dot(xb, wi_ref[:, lo:lo + FF_CHUNK])
        up = _dot(xb, wi_ref[:, D_FF + lo:D_FF + lo + FF_CHUNK])
        act_ref[:, lo:lo + FF_CHUNK] = (_silu(gate) * up).astype(BF16)
    z = DN_ALPHA * x + _dot(act_ref[...], wo_ref[...])
    o_ref[...] = _layer_norm(z, g_ref[...], b_ref[...])


def _ffn(x, wi, wo, g, b, tm):
    m = x.shape[0]
    tm = min(tm, m)
    return pl.pallas_call(
        _ffn_kernel,
        grid=(m // tm,),
        in_specs=[pl.BlockSpec((tm, D_MODEL), lambda i: (i, 0)), _resident(wi.shape), _resident(wo.shape),
                  _resident(g.shape), _resident(b.shape)],
        out_specs=pl.BlockSpec((tm, D_MODEL), lambda i: (i, 0)),
        out_shape=jax.ShapeDtypeStruct((m, D_MODEL), F32),
        scratch_shapes=[pltpu.VMEM((tm, D_FF), BF16)],
        compiler_params=_cparams(("parallel",), VMEM_LIMIT),
        name="ffn",
    )(x, wi, wo, g, b)


def _rope_tables(pos):
    half = MLA_ROPE // 2
    inv = ROPE_BASE ** (-jnp.arange(half, dtype=F32) / half)
    ang = pos.astype(F32)[:, None] * inv[None, :]
    cos, sin = jnp.cos(ang), jnp.sin(ang)
    n = pos.shape[0]
    z = lambda w: jnp.zeros((n, w), F32)
    cc = jnp.concatenate([cos, cos], axis=1)
    ss = jnp.concatenate([-sin, sin], axis=1)
    cq = jnp.concatenate([jnp.ones((n, MLA_NOPE), F32), cc, z(32)], axis=1)
    sq = jnp.concatenate([z(MLA_NOPE), ss, z(32)], axis=1)
    ck = jnp.concatenate([cc, z(96)], axis=1)
    sk = jnp.concatenate([ss, z(96)], axis=1)
    return cq, sq, ck, sk


def _roll_lanes_left(x, k):
    return pltpu.roll(x, x.shape[-1] - k, axis=x.ndim - 1)


def _mla_pre_kernel(h_ref, qn_ref, kvn_ref, wq_ref, wuk_ref, wuvt_ref, cq_ref, sq_ref, ck_ref, sk_ref,
                    rows_ref, q_ref, k_ref, vt_ref):
    h = h_ref[...]
    qn = _rms_norm(h[:, :MLA_Q_LORA], qn_ref[...]).astype(BF16)
    q = _dot(qn, wq_ref[...])
    cq = jnp.concatenate([cq_ref[...]] * MLA_HEADS, axis=1)
    sq = jnp.concatenate([sq_ref[...]] * MLA_HEADS, axis=1)
    q_ref[...] = (q * cq + _roll_lanes_left(q, 32) * sq).astype(BF16)
    ckv = _rms_norm(h[:, MLA_Q_LORA:MLA_Q_LORA + MLA_KV_LORA], kvn_ref[...])
    blk = h[:, 640:768]
    kpe = blk * ck_ref[...] + _roll_lanes_left(blk, 32) * sk_ref[...]
    rows_ref[:, :MLA_KV_LORA] = ckv
    rows_ref[:, MLA_KV_LORA:] = kpe[:, :MLA_ROPE]
    ckvb = ckv.astype(BF16)
    kpe_mid = pltpu.roll(kpe, 64, axis=1)
    k_ref[...] = (_dot(ckvb, wuk_ref[...]) + jnp.concatenate([kpe_mid] * MLA_HEADS, axis=1)).astype(BF16)
    vt_ref[0] = _dot_nt(wuvt_ref[...], ckvb).astype(BF16)


MLA_TQ = 512


def _mla_pre(h, qn, kvn, wq, wuk, wuvt, tabs, seq):
    m = h.shape[0]
    tm = MLA_TQ
    nblk = seq // tm
    tab = pl.BlockSpec((tm, LANES), lambda i: (i % nblk, 0))
    row = lambda w: pl.BlockSpec((tm, w), lambda i: (i, 0))
    hv = MLA_HEADS * MLA_V
    return pl.pallas_call(
        _mla_pre_kernel,
        grid=(m // tm,),
        in_specs=[pl.BlockSpec((tm, 768), lambda i: (i, 0)), _resident(qn.shape), _resident(kvn.shape),
                  _resident(wq.shape), _resident(wuk.shape), _resident(wuvt.shape), tab, tab, tab, tab],
        out_specs=[row(MLA_KV_DIM), row(1024), row(1024), pl.BlockSpec((1, hv, tm), lambda i: (i, 0, 0))],
        out_shape=[jax.ShapeDtypeStruct((m, MLA_KV_DIM), F32), jax.ShapeDtypeStruct((m, 1024), BF16),
                   jax.ShapeDtypeStruct((m, 1024), BF16), jax.ShapeDtypeStruct((m // tm, hv, tm), BF16)],
        compiler_params=_cparams(("parallel",), VMEM_LIMIT),
        name="mla_pre",
    )(h, qn, kvn, wq, wuk, wuvt, *tabs)


def _mla_attn_kernel(q_ref, k_ref, vt_ref, o_ref):
    qi = pl.program_id(2)
    t = MLA_TQ
    qs = [q_ref[:, j * LANES:(j + 1) * LANES] for j in range(2)]
    krow = lax.broadcasted_iota(jnp.int32, (t, t), 0)
    qcol = lax.broadcasted_iota(jnp.int32, (t, t), 1)

    def update(kb, carry, diag):
        out = []
        for j in range(2):
            m, l, acc = carry[j]
            kk = k_ref[pl.ds(pl.multiple_of(kb * t, t), t), j * LANES:(j + 1) * LANES]
            s = _dot_nt(kk, qs[j]) * MLA_SCALE
            if diag:
                s = jnp.where(krow <= qcol, s, NEG_INF)
            m_new = jnp.maximum(m, jnp.max(s, axis=0, keepdims=True))
            alpha = jnp.exp(m - m_new)
            p = jnp.exp(s - m_new)
            vt = vt_ref[kb, j * MLA_V:(j + 1) * MLA_V, :]
            out.append((m_new, alpha * l + jnp.sum(p, axis=0, keepdims=True),
                        alpha * acc + _dot(vt, p.astype(BF16))))
        return tuple(out)

    init = tuple((jnp.full((1, t), NEG_INF, F32), jnp.zeros((1, t), F32), jnp.zeros((MLA_V, t), F32))
                 for _ in range(2))
    carry = lax.fori_loop(0, qi, lambda kb, c: update(kb, c, False), init)
    carry = update(qi, carry, True)
    ot = jnp.concatenate([acc / l for _, l, acc in carry], axis=0)
    o_ref[...] = ot.T.astype(BF16)


def _mla_attn(q, k, vt, batch, seq):
    nq = seq // MLA_TQ
    vt4 = vt.reshape(batch, nq, MLA_HEADS * MLA_V, MLA_TQ)
    return pl.pallas_call(
        _mla_attn_kernel,
        grid=(batch, MLA_HEADS // 2, nq),
        in_specs=[pl.BlockSpec((MLA_TQ, 256), lambda b, p, i: (b * nq + i, p)),
                  pl.BlockSpec((seq, 256), lambda b, p, i: (b, p)),
                  pl.BlockSpec((None, nq, 2 * MLA_V, MLA_TQ), lambda b, p, i: (b, 0, p, 0))],
        out_specs=pl.BlockSpec((MLA_TQ, LANES), lambda b, p, i: (b * nq + i, p)),
        out_shape=jax.ShapeDtypeStruct((batch * seq, MLA_HEADS * MLA_V), BF16),
        compiler_params=_cparams(("parallel", "parallel", "arbitrary"), VMEM_LIMIT),
        name="mla_attn",
    )(q, k, vt4)


GLA_TB = 256
GLA_SLOTS = 12
HK = GLA_HEADS * GLA_DK
HV = GLA_HEADS * GLA_DV


def _gla_constants():
    c = GLA_CHUNK
    t = np.arange(c)[:, None]
    s = np.arange(c)[None, :]
    slots = []
    zero = np.zeros((c, c), bool)
    for p in (1, 2, 3):
        slots.append((t // 16 == p) & (s < 16 * p))
    slots.append(zero)
    for p in (1, 2, 3):
        slots.append((t // 16 == s // 16) & ((t % 16) // 4 == p) & (s % 16 < 4 * p))
    slots.append(zero)
    for d in (1, 2, 3):
        slots.append((t == s + d) & (t // 4 == s // 4))
    slots.append(t == s)
    mask = np.concatenate(slots, axis=1).astype(np.float32)
    assert (sum(slots) == (t >= s)).all()
    mask = np.tile(mask, (GLA_HEADS, 1))
    tri = (t >= s).astype(np.float32)
    hm = np.kron(np.eye(GLA_HEADS), np.ones((c, GLA_DK))).astype(np.float32)
    bd = np.kron(np.eye(GLA_HEADS), np.ones((GLA_DV, GLA_DK))).astype(np.float32)
    return (jnp.asarray(tri, BF16), jnp.asarray(hm, F32), jnp.asarray(mask, F32), jnp.asarray(bd, F32))


def _log_sigmoid(x):
    return jnp.minimum(x, 0.0) - jnp.log1p(jnp.exp(-jnp.abs(x)))


def _split3(x):
    hi = x.astype(BF16)
    r1 = x - hi.astype(F32)
    mid = r1.astype(BF16)
    lo = (r1 - mid.astype(F32)).astype(BF16)
    return hi, mid, lo


def _bc_rows(row, n):
    return jnp.broadcast_to(row, (n, row.shape[1]))


def _gla_kernel(blk_ref, q_ref, v_ref, r_ref, k_ref, wg_ref, bg_ref, gn_ref, tri_ref, hm_ref, mask_ref, bd_ref,
                o_ref, st_ref, st_scr):
    c = GLA_CHUNK

    @pl.when(pl.program_id(1) == 0)
    def _():
        st_scr[...] = jnp.zeros_like(st_scr)

    rows = lax.broadcasted_iota(jnp.int32, (c, HK), 0)
    rm4 = rows & 3

    def chunk(ci, _):
        sl = pl.ds(pl.multiple_of(ci * c, c), c)
        pre = _dot(blk_ref[sl, :].astype(BF16), wg_ref[...]) + bg_ref[...]
        g = _log_sigmoid(pre) * (1.0 / GLA_TAU)
        cum3 = _dot(tri_ref[...], jnp.concatenate(_split3(g), axis=1))
        cum = cum3[:, :HK] + cum3[:, HK:2 * HK] + cum3[:, 2 * HK:]
        q = q_ref[sl, :] * (GLA_DK ** -0.5)
        k = k_ref[sl, :]
        vb = v_ref[sl, :].astype(BF16)

        def kvar(e):
            return (k * jnp.exp(jnp.minimum(e, 0.0))).astype(BF16)

        zeros_k = jnp.zeros((c, HK), BF16)
        bnd = [cum[16 * p - 1:16 * p, :] for p in (1, 2, 3)]
        base1 = jnp.concatenate([jnp.zeros((16, HK), F32)] + [_bc_rows(b, 16) for b in bnd], axis=0)
        q1 = q * jnp.exp(cum - base1)
        k1 = jnp.concatenate([kvar(_bc_rows(b, c) - cum) for b in bnd] + [zeros_k], axis=0)
        sh = [pltpu.roll(cum, d, axis=0) for d in (1, 2, 3, 4)]
        base2 = jnp.where(rm4 == 0, sh[0], jnp.where(rm4 == 1, sh[1], jnp.where(rm4 == 2, sh[2], sh[3])))
        base2 = jnp.where(rows < 4, 0.0, base2)
        q2 = q * jnp.exp(cum - base2)
        k2 = []
        for p in (1, 2, 3):
            bk = jnp.concatenate([_bc_rows(cum[16 * i + 4 * p - 1:16 * i + 4 * p, :], 16) for i in range(4)], axis=0)
            k2.append(kvar(bk - cum))
        k2 = jnp.concatenate(k2 + [zeros_k], axis=0)
        k3 = jnp.concatenate([kvar(pltpu.roll(cum, c - d, axis=0) - cum) for d in (1, 2, 3)] + [k.astype(BF16)],
                             axis=0)
        hm = hm_ref[...]

        def stack(x):
            return (jnp.concatenate([x] * GLA_HEADS, axis=0) * hm).astype(BF16)

        p_all = jnp.concatenate([_dot_nt(stack(q1), k1), _dot_nt(stack(q2), k2), _dot_nt(stack(q), k3)], axis=1)
        pm = (p_all * mask_ref[...]).astype(BF16)
        st = st_scr[...]
        o = _dot_nt((q * jnp.exp(cum)).astype(BF16), st.astype(BF16))
        intra = []
        for h in range(GLA_HEADS):
            vh = vb[:, h * GLA_DV:(h + 1) * GLA_DV]
            intra.append(_dot(pm[h * c:(h + 1) * c, :], jnp.concatenate([vh] * GLA_SLOTS, axis=0)))
        o = o + jnp.concatenate(intra, axis=1)
        last = cum[c - 1:c, :]
        st_scr[...] = st * jnp.exp(last) + _dot_tn(vb, kvar(_bc_rows(last, c) - cum)) * bd_ref[...]
        r = r_ref[sl, :]
        outs = []
        for h in range(GLA_HEADS):
            lanes = slice(h * GLA_DV, (h + 1) * GLA_DV)
            outs.append(_rms_norm(o[:, lanes], gn_ref[...]) * _silu(r[:, lanes]))
        o_ref[sl, :] = jnp.concatenate(outs, axis=1).astype(BF16)
        return 0

    lax.fori_loop(0, GLA_TB // c, chunk, 0)

    @pl.when(pl.program_id(1) == pl.num_programs(1) - 1)
    def _():
        for h in range(GLA_HEADS):
            blk = st_scr[h * GLA_DV:(h + 1) * GLA_DV, :]
            st_ref[0, h] = blk.T[h * GLA_DK:(h + 1) * GLA_DK, :]


def _gla(h, wg, bg, gn, consts, batch, seq):
    nt = seq // GLA_TB
    tri, hm, mask, bd = consts
    rowblk = lambda w, cb: pl.BlockSpec((GLA_TB, w), lambda b, t: (b * nt + t, cb))
    return pl.pallas_call(
        _gla_kernel,
        grid=(batch, nt),
        in_specs=[rowblk(LANES, 5), rowblk(HK, 3), rowblk(HV, 2), rowblk(HV, 3), rowblk(HK, 8),
                  _resident(wg.shape), _resident(bg.shape), _resident(gn.shape),
                  _resident(tri.shape), _resident(hm.shape), _resident(mask.shape), _resident(bd.shape)],
        out_specs=[pl.BlockSpec((GLA_TB, HV), lambda b, t: (b * nt + t, 0)),
                   pl.BlockSpec((1, GLA_HEADS, GLA_DK, GLA_DV), lambda b, t: (b, 0, 0, 0))],
        out_shape=[jax.ShapeDtypeStruct((batch * seq, HV), BF16),
                   jax.ShapeDtypeStruct((batch, GLA_HEADS, GLA_DK, GLA_DV), F32)],
        scratch_shapes=[pltpu.VMEM((HV, HK), F32)],
        compiler_params=_cparams(("parallel", "arbitrary"), VMEM_LIMIT),
        name="gla",
    )(h, h, h, h, h, wg, bg, gn, tri, hm, mask, bd)


def _alibi_slope(group, head):
    return 2.0 ** (-8.0 * (group * DIL_HPG + head + 1) / (DIL_GROUPS * DIL_HPG))


DIL_TM = 256
QKV_G = 3 * MIX_C


def _proj_dil_kernel(x_ref, w_ref, kv_ref, r1_ref, r2_ref, r3_ref, acc_ref):
    res = _dot(x_ref[...].astype(BF16), w_ref[...])
    for g in range(DIL_GROUPS):
        kv_ref[:, 2 * g * MIX_C:2 * (g + 1) * MIX_C] = res[:, g * QKV_G + MIX_C:(g + 1) * QKV_G]
    r1_ref[...] = res[:, :QKV_G].astype(BF16)
    nb = QKV_G // LANES
    for c in range(2 * nb):
        acc_ref[c] = res[:, QKV_G + c * LANES:QKV_G + (c + 1) * LANES]
    for g, r_ref in ((1, r2_ref), (2, r3_ref)):
        dil = DIL_PAIRS[g][1]
        for r in range(dil):
            for c in range(nb):
                r_ref[0, r, :, c * LANES:(c + 1) * LANES] = (
                    acc_ref[(g - 1) * nb + c, pl.ds(r, DIL_TM // dil, stride=dil), :].astype(BF16))


def _proj_dil(x, w):
    m, k = x.shape
    tm = DIL_TM
    nt = m // tm
    res = lambda dil: ((nt, dil, tm // dil, QKV_G), pl.BlockSpec((1, dil, tm // dil, QKV_G), lambda i: (i, 0, 0, 0)))
    (s2, b2), (s3, b3) = res(DIL_PAIRS[1][1]), res(DIL_PAIRS[2][1])
    return pl.pallas_call(
        _proj_dil_kernel,
        grid=(nt,),
        in_specs=[pl.BlockSpec((tm, k), lambda i: (i, 0)), _resident(w.shape)],
        out_specs=[pl.BlockSpec((tm, 2 * MIX_C * DIL_GROUPS), lambda i: (i, 0)),
                   pl.BlockSpec((tm, QKV_G), lambda i: (i, 0)), b2, b3],
        out_shape=[jax.ShapeDtypeStruct((m, 2 * MIX_C * DIL_GROUPS), F32), jax.ShapeDtypeStruct((m, QKV_G), BF16),
                   jax.ShapeDtypeStruct(s2, BF16), jax.ShapeDtypeStruct(s3, BF16)],
        scratch_shapes=[pltpu.VMEM((2 * QKV_G // LANES, tm, LANES), F32)],
        compiler_params=_cparams(("parallel",), VMEM_LIMIT),
        name="proj_dil",
    )(x, w)


def _dil_attn_kernel(q_ref, kc_ref, vc_ref, *rest, group, dil, has_prev):
    if has_prev:
        kp_ref, vp_ref, o_ref, lse_ref = rest
    else:
        o_ref, lse_ref = rest
    n = DIL_STEPS
    nb = pl.program_id(2)
    nk = 2 * n if has_prev else n
    qi = lax.broadcasted_iota(jnp.int32, (n, nk), 0)
    ki = lax.broadcasted_iota(jnp.int32, (n, nk), 1)
    steps = qi + (n if has_prev else 0) - ki
    valid = (steps >= 0) & (steps <= n)
    if has_prev:
        valid = valid & ((ki >= n) | (nb > 0))
    dist = (steps * dil).astype(F32)
    lane = lax.broadcasted_iota(jnp.int32, (n, LANES), 1)
    low = lane < DIL_HD
    rows = lambda ref: ref[...].reshape(n, MIX_C)
    q = rows(q_ref)
    k = jnp.concatenate([rows(kp_ref), rows(kc_ref)], axis=0) if has_prev else rows(kc_ref)
    v = jnp.concatenate([rows(vp_ref), rows(vc_ref)], axis=0) if has_prev else rows(vc_ref)
    o_all, lse_all = [], []
    for j in range(DIL_HPG // 2):
        cols = slice(j * LANES, (j + 1) * LANES)
        q2, k2, v2 = q[:, cols], k[:, cols], v[:, cols]
        outs, lses = [], []
        for par in range(2):
            qh = jnp.where(low if par == 0 else ~low, q2, jnp.zeros_like(q2))
            s = _dot_nt(qh, k2) * (DIL_HD ** -0.5) - _alibi_slope(group, 2 * j + par) * dist
            s = jnp.where(valid, s, NEG_INF)
            m = jnp.max(s, axis=-1, keepdims=True)
            p = jnp.exp(s - m)
            l = jnp.sum(p, axis=-1, keepdims=True)
            outs.append(_dot(p.astype(BF16), v2) / l)
            lses.append(jnp.broadcast_to(m + jnp.log(l), (n, LANES)))
        o_all.append(jnp.where(low, outs[0], outs[1]))
        lse_all.append(jnp.where(low, lses[0], lses[1]))
    o_ref[...] = jnp.concatenate(o_all, axis=1).reshape(o_ref.shape)
    lse_ref[...] = jnp.concatenate(lse_all, axis=1).reshape(lse_ref.shape)


def _dil_attn(qkv_r, group, batch, seq):
    _, dil = DIL_PAIRS[group]
    n = DIL_STEPS
    _, nt, _, rr, _ = qkv_r.shape
    tpb = n // rr
    nblk = nt // tpb
    has_prev = nblk > 1
    cur = lambda c: pl.BlockSpec((None, tpb, None, rr, MIX_C), lambda b, r, i: (b, i, r, 0, c))
    prev = lambda c: pl.BlockSpec((None, tpb, None, rr, MIX_C), lambda b, r, i: (b, jnp.maximum(i - 1, 0), r, 0, c))
    in_specs = [cur(0), cur(1), cur(2)] + ([prev(1), prev(2)] if has_prev else [])
    out = pl.BlockSpec((None, tpb, None, rr, MIX_C), lambda b, r, i: (b, i, r, 0, 0))
    return pl.pallas_call(
        functools.partial(_dil_attn_kernel, group=group, dil=dil, has_prev=has_prev),
        grid=(batch, dil, nblk),
        in_specs=in_specs,
        out_specs=[out, out],
        out_shape=[jax.ShapeDtypeStruct((batch, nt, dil, rr, MIX_C), F32)] * 2,
        compiler_params=_cparams(("parallel", "parallel", "arbitrary"), VMEM_LIMIT),
        name="dil_attn%d" % group,
    )(*([qkv_r] * len(in_specs)))


def _dil_mix_ln_kernel(o1, o2, o3, l1, l2, l3, w_ref, x_ref, g_ref, b_ref, o_ref, *scr):
    nb = MIX_C // LANES
    for src, dst, dil in ((o2, scr[0], DIL_PAIRS[1][1]), (l2, scr[1], DIL_PAIRS[1][1]),
                          (o3, scr[2], DIL_PAIRS[2][1]), (l3, scr[3], DIL_PAIRS[2][1])):
        for r in range(dil):
            for c in range(nb):
                dst[c, pl.ds(r, DIL_TM // dil, stride=dil), :] = src[0, r, :, c * LANES:(c + 1) * LANES]
    whole = lambda s: jnp.concatenate([s[c] for c in range(nb)], axis=1)
    os_ = [o1[...], whole(scr[0]), whole(scr[2])]
    ls = [l1[...], whole(scr[1]), whole(scr[3])]
    mx = jnp.maximum(jnp.maximum(ls[0], ls[1]), ls[2])
    es = [jnp.exp(l - mx) for l in ls]
    mix = (es[0] * os_[0] + es[1] * os_[1] + es[2] * os_[2]) / (es[0] + es[1] + es[2])
    z = DN_ALPHA * x_ref[...] + _dot(mix.astype(BF16), w_ref[...])
    o_ref[...] = _layer_norm(z, g_ref[...], b_ref[...])


def _dil_mix_ln(os_, lses, w, x, g, b):
    m = x.shape[0]
    tm = DIL_TM
    nt = m // tm
    row = lambda wd: pl.BlockSpec((tm, wd), lambda i: (i, 0))

    def res(a):
        dil = a.shape[2]
        a = a.reshape(nt, dil, tm // dil, MIX_C)
        return a, pl.BlockSpec((1, dil, tm // dil, MIX_C), lambda i: (i, 0, 0, 0))

    flat = lambda a: a.reshape(m, MIX_C)
    (o2, s2), (o3, s3), (l2, _), (l3, _) = res(os_[1]), res(os_[2]), res(lses[1]), res(lses[2])
    return pl.pallas_call(
        _dil_mix_ln_kernel,
        grid=(nt,),
        in_specs=[row(MIX_C), s2, s3, row(MIX_C), s2, s3,
                  _resident(w.shape), row(D_MODEL), _resident(g.shape), _resident(b.shape)],
        out_specs=row(D_MODEL),
        out_shape=jax.ShapeDtypeStruct((m, D_MODEL), F32),
        scratch_shapes=[pltpu.VMEM((MIX_C // LANES, tm, LANES), F32)] * 4,
        compiler_params=_cparams(("parallel",), VMEM_LIMIT),
        name="dil_mix_ln",
    )(flat(os_[0]), o2, o3, flat(lses[0]), l2, l3, w, x, g, b)


def _mla_pre_s_kernel(h_ref, qn_ref, kvn_ref, wqn_ref, wqp_ref, wukt_ref, ck_ref, sk_ref, rows_ref, q_ref):
    h = h_ref[...]
    qn = _rms_norm(h[:, :MLA_Q_LORA], qn_ref[...]).astype(BF16)
    qnope = _dot(qn, wqn_ref[...]).astype(BF16)
    qp = _dot(qn, wqp_ref[...])
    cq = jnp.concatenate([ck_ref[...]] * MLA_HEADS, axis=1)
    sq = jnp.concatenate([sk_ref[...]] * MLA_HEADS, axis=1)
    qp = qp * cq + _roll_lanes_left(qp, 32) * sq
    ckv = _rms_norm(h[:, MLA_Q_LORA:MLA_Q_LORA + MLA_KV_LORA], kvn_ref[...])
    blk = h[:, 640:768]
    kpe = blk * ck_ref[...] + _roll_lanes_left(blk, 32) * sk_ref[...]
    rows_ref[:, :MLA_KV_LORA] = ckv
    rows_ref[:, MLA_KV_LORA:] = kpe[:, :MLA_ROPE]
    for hd in range(MLA_HEADS):
        cols = slice(hd * LANES, (hd + 1) * LANES)
        q_ref[hd, :, :MLA_KV_LORA] = _dot(qnope[:, cols], wukt_ref[hd])
        q_ref[hd, :, MLA_KV_LORA:] = qp[:, hd * LANES:hd * LANES + MLA_ROPE]


def _mla_pre_s(h, qn, kvn, wqn, wqp, wukt, tabs):
    m = h.shape[0]
    full = lambda a: _resident(a.shape)
    return pl.pallas_call(
        _mla_pre_s_kernel,
        grid=(1,),
        in_specs=[pl.BlockSpec((m, 768), lambda i: (0, 0)), full(qn), full(kvn), full(wqn), full(wqp), full(wukt)]
        + [full(t) for t in tabs[2:]],
        out_specs=[pl.BlockSpec((m, MLA_KV_DIM), lambda i: (0, 0)),
                   pl.BlockSpec((MLA_HEADS, m, MLA_KV_DIM), lambda i: (0, 0, 0))],
        out_shape=[jax.ShapeDtypeStruct((m, MLA_KV_DIM), F32), jax.ShapeDtypeStruct((MLA_HEADS, m, MLA_KV_DIM), F32)],
        compiler_params=_cparams(("arbitrary",), VMEM_LIMIT),
        name="mla_pre_s",
    )(h, qn, kvn, wqn, wqp, wukt, *tabs[2:])


DEC_PAGES = 16


def _mla_dec_kernel(pt_ref, q_ref, new_ref, *rest):
    pages = rest[:DEC_PAGES]
    o_ref, m_scr, l_scr, acc_scr = rest[DEC_PAGES:]
    step = pl.program_id(1)
    q = q_ref[0]
    qb = q.astype(BF16)

    @pl.when(step == 0)
    def _():
        new = new_ref[0]
        m_scr[...] = jnp.sum(q * new, axis=-1, keepdims=True) * MLA_SCALE
        l_scr[...] = jnp.ones_like(l_scr)
        acc_scr[...] = jnp.broadcast_to(new[:, :MLA_KV_LORA], acc_scr.shape)

    kvs = [p[...].astype(BF16) for p in pages]
    s = jnp.concatenate([_dot_nt(qb, kv) for kv in kvs], axis=1) * MLA_SCALE
    m_old = m_scr[...]
    m_new = jnp.maximum(m_old, jnp.max(s, axis=-1, keepdims=True))
    alpha = jnp.exp(m_old - m_new)
    p = jnp.exp(s - m_new)
    pb = p.astype(BF16)
    acc = alpha * acc_scr[...]
    for j, kv in enumerate(kvs):
        acc = acc + _dot(pb[:, j * PAGE_SIZE:(j + 1) * PAGE_SIZE], kv[:, :MLA_KV_LORA])
    m_scr[...] = m_new
    l_scr[...] = alpha * l_scr[...] + jnp.sum(p, axis=-1, keepdims=True)
    acc_scr[...] = acc

    @pl.when(step == pl.num_programs(1) - 1)
    def _():
        o_ref[0] = acc_scr[...] / l_scr[...]


def _mla_dec(page_table, q_abs, rows_new, cache, layer):
    batch = q_abs.shape[0]
    n_pages = page_table.shape[1]
    steps = n_pages // DEC_PAGES
    pt = page_table.reshape(-1)

    def page_spec(j):
        return pl.BlockSpec((None, None, PAGE_SIZE, MLA_KV_DIM),
                            lambda b, s, pt_ref: (layer, pt_ref[b * n_pages + s * DEC_PAGES + j], 0, 0))

    grid_spec = pltpu.PrefetchScalarGridSpec(
        num_scalar_prefetch=1,
        grid=(batch, steps),
        in_specs=[pl.BlockSpec((1, MLA_HEADS, MLA_KV_DIM), lambda b, s, pt_ref: (b, 0, 0)),
                  pl.BlockSpec((1, 1, MLA_KV_DIM), lambda b, s, pt_ref: (b, 0, 0))]
        + [page_spec(j) for j in range(DEC_PAGES)],
        out_specs=pl.BlockSpec((1, MLA_HEADS, MLA_KV_LORA), lambda b, s, pt_ref: (b, 0, 0)),
        scratch_shapes=[pltpu.VMEM((MLA_HEADS, 1), F32), pltpu.VMEM((MLA_HEADS, 1), F32),
                        pltpu.VMEM((MLA_HEADS, MLA_KV_LORA), F32)],
    )
    return pl.pallas_call(
        _mla_dec_kernel,
        grid_spec=grid_spec,
        out_shape=jax.ShapeDtypeStruct((batch, MLA_HEADS, MLA_KV_LORA), F32),
        compiler_params=_cparams(("parallel", "arbitrary"), VMEM_LIMIT),
        name="mla_dec",
    )(pt, q_abs, rows_new.reshape(batch, 1, MLA_KV_DIM), *([cache] * DEC_PAGES))


def _gla_dec_kernel(blk_ref, q_ref, v_ref, r_ref, k_ref, s_ref, wg_ref, bg_ref, gn_ref, o_ref, so_ref):
    pre = _dot(blk_ref[0].astype(BF16), wg_ref[...]) + bg_ref[...]
    a = jnp.exp(_log_sigmoid(pre) * (1.0 / GLA_TAU))
    q = q_ref[0] * (GLA_DK ** -0.5)
    k = k_ref[0]
    v = v_ref[0]
    r = r_ref[0]
    eye = (lax.broadcasted_iota(jnp.int32, (GLA_DK, GLA_DK), 0)
           == lax.broadcasted_iota(jnp.int32, (GLA_DK, GLA_DK), 1))
    outs = []
    for h in range(GLA_HEADS):
        kl = slice(h * GLA_DK, (h + 1) * GLA_DK)
        vl = slice(h * GLA_DV, (h + 1) * GLA_DV)
        diag = lambda row: jnp.where(eye, jnp.broadcast_to(row, (GLA_DK, GLA_DK)), 0.0)
        lhs = jnp.concatenate([diag(a[:, kl]), diag(k[:, kl])], axis=1)
        rhs = jnp.concatenate([s_ref[0, h], jnp.broadcast_to(v[:, vl], (GLA_DK, GLA_DV))], axis=0)
        s_new = jnp.dot(lhs, rhs, preferred_element_type=F32, precision=lax.Precision.HIGHEST)
        so_ref[0, h] = s_new
        o = jnp.dot(jnp.broadcast_to(q[:, kl], (8, GLA_DK)), s_new, preferred_element_type=F32,
                    precision=lax.Precision.HIGHEST)[:1]
        outs.append(_rms_norm(o, gn_ref[...]) * _silu(r[:, vl]))
    o_ref[0] = jnp.concatenate(outs, axis=1)


def _gla_dec(h, state, layer, wg, bg, gn):
    batch = h.shape[0]
    x = h.reshape(batch, 1, IN_A_PAD)
    rowblk = lambda w, cb: pl.BlockSpec((1, 1, w), lambda b: (b, 0, cb))
    st_in = pl.BlockSpec((None, 1, GLA_HEADS, GLA_DK, GLA_DV), lambda b: (layer, b, 0, 0, 0))
    st = pl.BlockSpec((1, GLA_HEADS, GLA_DK, GLA_DV), lambda b: (b, 0, 0, 0))
    o, s_new = pl.pallas_call(
        _gla_dec_kernel,
        grid=(batch,),
        in_specs=[rowblk(LANES, 5), rowblk(HK, 3), rowblk(HV, 2), rowblk(HV, 3), rowblk(HK, 8), st_in,
                  _resident(wg.shape), _resident(bg.shape), _resident(gn.shape)],
        out_specs=[pl.BlockSpec((1, 1, HV), lambda b: (b, 0, 0)), st],
        out_shape=[jax.ShapeDtypeStruct((batch, 1, HV), F32), jax.ShapeDtypeStruct(state.shape[1:], F32)],
        compiler_params=_cparams(("parallel",), VMEM_LIMIT),
        name="gla_dec",
    )(x, x, x, x, x, state, wg, bg, gn)
    return o.reshape(batch, HV), s_new


def _dil_dec_kernel(x_ref, c1_ref, c2_ref, c3_ref, o_ref):
    n = DIL_STEPS
    shape = (n, DIL_HPG, 1)
    head = lax.broadcasted_iota(jnp.int32, shape, 1).astype(F32)
    back = (n - lax.broadcasted_iota(jnp.int32, shape, 0)).astype(F32)
    outs, lses = [], []
    for g, c_ref in enumerate((c1_ref, c2_ref, c3_ref)):
        dil = DIL_PAIRS[g][1]
        q, k_new, v_new = x_ref[0, 3 * g], x_ref[0, 3 * g + 1], x_ref[0, 3 * g + 2]
        kc, vc = c_ref[:, 0], c_ref[:, 1]
        slope = jnp.exp((g * DIL_HPG + head + 1.0) * (-8.0 * math.log(2.0) / (DIL_GROUPS * DIL_HPG)))
        s = jnp.sum(kc * q[None], axis=-1, keepdims=True) * (DIL_HD ** -0.5) - slope * (back * dil)
        s_new = jnp.sum(q * k_new, axis=-1, keepdims=True) * (DIL_HD ** -0.5)
        m = jnp.maximum(jnp.max(s, axis=0), s_new)
        p = jnp.exp(s - m[None])
        p_new = jnp.exp(s_new - m)
        l = jnp.sum(p, axis=0) + p_new
        outs.append((jnp.sum(p * vc, axis=0) + p_new * v_new) / l)
        lses.append(m + jnp.log(l))
    mx = jnp.maximum(jnp.maximum(lses[0], lses[1]), lses[2])
    es = [jnp.exp(l - mx) for l in lses]
    o_ref[0] = (es[0] * outs[0] + es[1] * outs[1] + es[2] * outs[2]) / (es[0] + es[1] + es[2])


def _dil_dec(qkv, caches, layer):
    batch = qkv.shape[0]
    specs = [pl.BlockSpec((1, 3 * DIL_GROUPS, DIL_HPG, DIL_HD), lambda b: (b, 0, 0, 0))]
    args = [qkv]
    for (window, dil), c in zip(DIL_PAIRS, caches):
        assert c.shape[2] == window
        args.append(c.reshape(c.shape[0], batch, window // dil, dil, 2, DIL_HPG, DIL_HD))
        specs.append(pl.BlockSpec((None, None, DIL_STEPS, None, 2, DIL_HPG, DIL_HD),
                                  lambda b: (layer, b, 0, 0, 0, 0, 0)))
    return pl.pallas_call(
        _dil_dec_kernel,
        grid=(batch,),
        in_specs=specs,
        out_specs=pl.BlockSpec((1, DIL_HPG, DIL_HD), lambda b: (b, 0, 0)),
        out_shape=jax.ShapeDtypeStruct((batch, DIL_HPG, DIL_HD), F32),
        compiler_params=_cparams(("parallel",), VMEM_LIMIT),
        name="dil_dec",
    )(*args)


def _even_in_cols():
    o_cq, o_kv, o_gq, o_gk, o_gv, o_lr, o_gr = 0, 384, 672, 928, 1184, 1696, 1712
    r = np.arange
    kpe = o_kv + MLA_KV_LORA
    idx = np.concatenate([r(o_cq, o_cq + 384), r(o_kv, o_kv + 256), r(kpe, kpe + 32), r(kpe + 16, kpe + 32),
                          r(kpe, kpe + 16), r(o_lr, o_lr + 16), np.full(48, -1), r(o_gq, o_gq + 256),
                          r(o_gv, o_gv + 512), r(o_gr, o_gr + 512), r(o_gk, o_gk + 256)])
    assert idx.shape[0] == IN_A_PAD
    return idx


def _gather_cols(w, idx):
    return jnp.where(jnp.asarray(idx >= 0)[None, :], jnp.take(w, jnp.asarray(np.maximum(idx, 0)), axis=1), 0.0)


def _uq_cols(kind):
    idx = []
    for h in range(MLA_HEADS):
        b = h * (MLA_NOPE + MLA_ROPE)
        nope = np.arange(b, b + 64)
        x1, x2 = np.arange(b + 64, b + 80), np.arange(b + 80, b + 96)
        if kind == "full":
            idx += [nope, x1, x2, x2, x1]
        elif kind == "nope":
            idx += [nope, np.full(64, -1)]
        else:
            idx += [x1, x2, x2, x1, np.full(64, -1)]
    return np.concatenate(idx)


def _odd_in_cols():
    blk = lambda which, g: np.arange((which * DIL_GROUPS + g) * MIX_C, (which * DIL_GROUPS + g + 1) * MIX_C)
    return np.concatenate([blk(which, g) for g in range(DIL_GROUPS) for which in range(3)])


TM_DENSE = 512
TM_WIDE = 256


def kernel(x_prompt, x_sample, cache_mla, state_gla, cache_dil_w128, cache_dil_w512, cache_dil_w2048, page_table,
           w_in_a, mla_q_norm, mla_w_uq, mla_kv_norm, mla_w_uk, mla_w_uv, gla_w_gate2, gla_b_gate, gla_norm, w_out_a,
           w_in_c, w_out_c, ffn_w_in, ffn_w_out, ln_g, ln_b):
    batch, seq, _ = x_prompt.shape
    dbatch = x_sample.shape[0]
    dil_caches = (cache_dil_w128, cache_dil_w512, cache_dil_w2048)
    gla_consts = _gla_constants()
    tabs_p = _rope_tables(jnp.arange(seq, dtype=jnp.int32))
    tabs_s = _rope_tables(jnp.full((dbatch,), PAST_LEN, jnp.int32))
    even_cols, odd_cols = _even_in_cols(), _odd_in_cols()
    row2 = lambda v: v.reshape(1, -1).astype(F32)

    xp = x_prompt.reshape(batch * seq, D_MODEL)
    xs = x_sample.reshape(dbatch, D_MODEL)
    mla_p, mla_s, gla_p, gla_s = [], [], [], []
    dil_p = [[] for _ in DIL_PAIRS]
    dil_s = [[] for _ in DIL_PAIRS]

    for layer in range(DEPTH):
        i = layer // 2
        g0, b0, g1, b1 = (row2(ln_g[layer, 0]), row2(ln_b[layer, 0]), row2(ln_g[layer, 1]), row2(ln_b[layer, 1]))
        if layer % 2 == 0:
            w_in = _gather_cols(w_in_a[i], even_cols).astype(BF16)
            wq_full = _gather_cols(mla_w_uq[i], _uq_cols("full")).astype(BF16)
            wq_nope = _gather_cols(mla_w_uq[i], _uq_cols("nope")).astype(BF16)
            wq_rope = _gather_cols(mla_w_uq[i], _uq_cols("rope")).astype(BF16)
            wuk_pad = jnp.pad(mla_w_uk[i], ((0, 0), (0, 0), (0, LANES - MLA_NOPE))).reshape(MLA_KV_LORA, -1).astype(BF16)
            wuk_t = jnp.pad(jnp.transpose(mla_w_uk[i], (1, 2, 0)), ((0, 0), (0, LANES - MLA_NOPE), (0, 0))).astype(BF16)
            wuv_t = mla_w_uv[i].reshape(MLA_KV_LORA, -1).T.astype(BF16)
            wuv_bd = (jnp.eye(MLA_HEADS, dtype=F32)[:, None, :, None]
                      * jnp.transpose(mla_w_uv[i], (1, 0, 2))[:, :, None, :]).reshape(
                          MLA_HEADS * MLA_KV_LORA, MLA_HEADS * MLA_V).astype(BF16)
            wg = jnp.pad(gla_w_gate2[i], ((64, LANES - 64 - GLA_GATE_RANK), (0, 0))).astype(BF16)
            bg, gn = row2(gla_b_gate[i]), row2(gla_norm[i])
            qn, kvn = row2(mla_q_norm[i]), row2(mla_kv_norm[i])
            w_out = w_out_a[i].astype(BF16)

            h = _proj(xp, w_in, TM_DENSE)
            rows, qf, kf, vt = _mla_pre(h, qn, kvn, wq_full, wuk_pad, wuv_t, tabs_p, seq)
            mla_out = _mla_attn(qf, kf, vt, batch, seq)
            gla_out, s_fin = _gla(h, wg, bg, gn, gla_consts, batch, seq)
            mla_p.append(rows.reshape(batch, seq, MLA_KV_DIM))
            gla_p.append(s_fin)
            xp = _mix_ln([mla_out, gla_out], w_out, xp, g0, b0, TM_DENSE)

            h = _proj(xs, w_in, TM_DENSE)
            rows, q_abs = _mla_pre_s(h, qn, kvn, wq_nope, wq_rope, wuk_t, tabs_s)
            lat = _mla_dec(page_table, jnp.transpose(q_abs, (1, 0, 2)), rows, cache_mla, i)
            mla_out = _proj(lat.reshape(dbatch, MLA_HEADS * MLA_KV_LORA), wuv_bd, TM_DENSE)
            gla_out, s_fin = _gla_dec(h, state_gla, i, wg, bg, gn)
            mla_s.append(rows.reshape(dbatch, 1, MLA_KV_DIM))
            gla_s.append(s_fin)
            xs = _mix_ln([mla_out, gla_out], w_out, xs, g0, b0, TM_DENSE)
        else:
            w_in = jnp.take(w_in_c[i], jnp.asarray(odd_cols), axis=1).astype(BF16)
            w_out = w_out_c[i].astype(BF16)

            kv, r1, r2, r3 = _proj_dil(xp, w_in)
            nt = seq // DIL_TM
            res = [r1.reshape(batch, seq // DIL_STEPS, 1, DIL_STEPS, QKV_G)] + [
                r.reshape(batch, nt, r.shape[1], r.shape[2], QKV_G) for r in (r2, r3)]
            os_, lses = zip(*[_dil_attn(res[g], g, batch, seq) for g in range(DIL_GROUPS)])
            kv3 = kv.reshape(batch, seq, 2 * MIX_C * DIL_GROUPS)
            for g, (window, _) in enumerate(DIL_PAIRS):
                keep = min(window, seq)
                dil_p[g].append(kv3[:, seq - keep:, 2 * g * MIX_C:2 * (g + 1) * MIX_C]
                                .reshape(batch, keep, 2, DIL_HPG, DIL_HD))
            xp = _dil_mix_ln(os_, lses, w_out, xp, g0, b0)

            qkv = _proj(xs, w_in, TM_WIDE).reshape(dbatch, 3 * DIL_GROUPS, DIL_HPG, DIL_HD)
            mix = _dil_dec(qkv, dil_caches, i).reshape(dbatch, MIX_C)
            for g in range(DIL_GROUPS):
                dil_s[g].append(qkv[:, 3 * g + 1:3 * g + 3].reshape(dbatch, 1, 2, DIL_HPG, DIL_HD))
            xs = _mix_ln([mix], w_out, xs, g0, b0, TM_DENSE)

        wi, wo = ffn_w_in[layer].astype(BF16), ffn_w_out[layer].astype(BF16)
        xp = _ffn(xp, wi, wo, g1, b1, TM_DENSE)
        xs = _ffn(xs, wi, wo, g1, b1, TM_DENSE)

    st = jnp.stack
    return (xp.reshape(batch, seq, D_MODEL), xs.reshape(dbatch, 1, D_MODEL), st(mla_p), st(mla_s), st(gla_p), st(gla_s),
            st(dil_p[0]), st(dil_s[0]), st(dil_p[1]), st(dil_s[1]), st(dil_p[2]), st(dil_s[2]))
```

```python
import functools
import math

import numpy as np
import jax
import jax.numpy as jnp
from jax import lax
from jax.experimental import pallas as pl
from jax.experimental.pallas import tpu as pltpu

F32 = jnp.float32
BF16 = jnp.bfloat16

D_MODEL = 1024
DEPTH = 4
PAST_LEN = 16384
PAGE_SIZE = 128
MLA_HEADS = 8
MLA_Q_LORA = 384
MLA_KV_LORA = 256
MLA_NOPE = 64
MLA_ROPE = 32
MLA_V = 64
MLA_KV_DIM = MLA_KV_LORA + MLA_ROPE
ROPE_BASE = 10000.0
GLA_HEADS = 4
GLA_DK = 64
GLA_DV = 128
GLA_GATE_RANK = 16
GLA_TAU = 16.0
GLA_CHUNK = 64
DIL_PAIRS = ((128, 1), (512, 4), (2048, 16))
DIL_GROUPS = 3
DIL_HPG = 8
DIL_HD = 64
DIL_STEPS = 128
D_FF = -(-8 * D_MODEL // (3 * 256)) * 256
DN_ALPHA = (2.0 * DEPTH) ** 0.25
IN_A_PAD = 2304
MIX_C = DIL_HPG * DIL_HD
IN_C = 3 * DIL_GROUPS * MIX_C
MLA_SCALE = (MLA_NOPE + MLA_ROPE) ** -0.5

V7X_VMEM_BYTES = 64 * 1024 * 1024
VMEM_LIMIT = 56 * 1024 * 1024
LANES = 128

NEG_INF = float("-inf")


def _cparams(sem, limit=None):
    return pltpu.CompilerParams(dimension_semantics=sem, vmem_limit_bytes=limit)


def _resident(shape):
    nd = len(shape)
    return pl.BlockSpec(shape, lambda *_: (0,) * nd, pipeline_mode=pl.Buffered(1))


def _layer_norm(z, g, b):
    mu = jnp.mean(z, axis=-1, keepdims=True)
    zc = z - mu
    var = jnp.mean(zc * zc, axis=-1, keepdims=True)
    return zc * lax.rsqrt(var + 1e-5) * g + b


def _rms_norm(z, g):
    return z * lax.rsqrt(jnp.mean(z * z, axis=-1, keepdims=True) + 1e-6) * g


def _silu(z):
    return z * (1.0 / (1.0 + jnp.exp(-z)))


def _dot(a, b):
    return jnp.dot(a, b, preferred_element_type=F32)


def _dot_nt(a, b):
    return lax.dot_general(a, b, (((1,), (1,)), ((), ())), preferred_element_type=F32)


def _dot_tn(a, b):
    return lax.dot_general(a, b, (((0,), (0,)), ((), ())), preferred_element_type=F32)


def _proj_kernel(x_ref, w_ref, o_ref):
    o_ref[...] = _dot(x_ref[...].astype(BF16), w_ref[...]).astype(o_ref.dtype)


def _proj(x, w, tm, out_dtype=F32):
    m, k = x.shape
    n = w.shape[1]
    tm = min(tm, m)
    return pl.pallas_call(
        _proj_kernel,
        grid=(m // tm,),
        in_specs=[pl.BlockSpec((tm, k), lambda i: (i, 0)), _resident((k, n))],
        out_specs=pl.BlockSpec((tm, n), lambda i: (i, 0)),
        out_shape=jax.ShapeDtypeStruct((m, n), out_dtype),
        compiler_params=_cparams(("parallel",), VMEM_LIMIT),
        name="proj",
    )(x, w)


def _mix_ln_kernel(*refs, n_parts):
    parts = refs[:n_parts]
    w_ref, x_ref, g_ref, b_ref, o_ref = refs[n_parts:]
    a = jnp.concatenate([p[...].astype(BF16) for p in parts], axis=-1) if n_parts > 1 else parts[0][...].astype(BF16)
    z = DN_ALPHA * x_ref[...] + _dot(a, w_ref[...])
    o_ref[...] = _layer_norm(z, g_ref[...], b_ref[...])


def _mix_ln(parts, w, x, g, b, tm):
    m = x.shape[0]
    tm = min(tm, m)
    kern = functools.partial(_mix_ln_kernel, n_parts=len(parts))
    return pl.pallas_call(
        kern,
        grid=(m // tm,),
        in_specs=[pl.BlockSpec((tm, p.shape[1]), lambda i: (i, 0)) for p in parts]
        + [_resident(w.shape), pl.BlockSpec((tm, D_MODEL), lambda i: (i, 0)), _resident(g.shape), _resident(b.shape)],
        out_specs=pl.BlockSpec((tm, D_MODEL), lambda i: (i, 0)),
        out_shape=jax.ShapeDtypeStruct((m, D_MODEL), F32),
        compiler_params=_cparams(("parallel",), VMEM_LIMIT),
        name="mix_ln",
    )(*parts, w, x, g, b)


FF_CHUNK = 1408


def _ffn_kernel(x_ref, wi_ref, wo_ref, g_ref, b_ref, o_ref, act_ref):
    x = x_ref[...]
    xb = x.astype(BF16)
    for c in range(D_FF // FF_CHUNK):
        lo = c * FF_CHUNK
        gate = _dot(xb, wi_ref[:, lo:lo + FF_CHUNK])
        up = _dot(xb, wi_ref[:, D_FF + lo:D_FF + lo + FF_CHUNK])
        act_ref[:, lo:lo + FF_CHUNK] = (_silu(gate) * up).astype(BF16)
    z = DN_ALPHA * x + _dot(act_ref[...], wo_ref[...])
    o_ref[...] = _layer_norm(z, g_ref[...], b_ref[...])


def _ffn(x, wi, wo, g, b, tm):
    m = x.shape[0]
    tm = min(tm, m)
    return pl.pallas_call(
        _ffn_kernel,
        grid=(m // tm,),
        in_specs=[pl.BlockSpec((tm, D_MODEL), lambda i: (i, 0)), _resident(wi.shape), _resident(wo.shape),
                  _resident(g.shape), _resident(b.shape)],
        out_specs=pl.BlockSpec((tm, D_MODEL), lambda i: (i, 0)),
        out_shape=jax.ShapeDtypeStruct((m, D_MODEL), F32),
        scratch_shapes=[pltpu.VMEM((tm, D_FF), BF16)],
        compiler_params=_cparams(("parallel",), VMEM_LIMIT),
        name="ffn",
    )(x, wi, wo, g, b)


def _rope_tables(pos):
    half = MLA_ROPE // 2
    inv = ROPE_BASE ** (-jnp.arange(half, dtype=F32) / half)
    ang = pos.astype(F32)[:, None] * inv[None, :]
    cos, sin = jnp.cos(ang), jnp.sin(ang)
    n = pos.shape[0]
    z = lambda w: jnp.zeros((n, w), F32)
    cc = jnp.concatenate([cos, cos], axis=1)
    ss = jnp.concatenate([-sin, sin], axis=1)
    cq = jnp.concatenate([jnp.ones((n, MLA_NOPE), F32), cc, z(32)], axis=1)
    sq = jnp.concatenate([z(MLA_NOPE), ss, z(32)], axis=1)
    ck = jnp.concatenate([cc, z(96)], axis=1)
    sk = jnp.concatenate([ss, z(96)], axis=1)
    return cq, sq, ck, sk


def _roll_lanes_left(x, k):
    return pltpu.roll(x, x.shape[-1] - k, axis=x.ndim - 1)


def _mla_pre_kernel(h_ref, qn_ref, kvn_ref, wq_ref, wuk_ref, wuvt_ref, cq_ref, sq_ref, ck_ref, sk_ref, *rest):
    rows_ref, q_ref, k_ref, vt_ref = rest[-4:]
    h = h_ref[...]
    qn = _rms_norm(h[:, :MLA_Q_LORA], qn_ref[...]).astype(BF16)
    q = _dot(qn, wq_ref[...])
    cq = jnp.concatenate([cq_ref[...]] * MLA_HEADS, axis=1)
    sq = jnp.concatenate([sq_ref[...]] * MLA_HEADS, axis=1)
    q_ref[...] = (q * cq + _roll_lanes_left(q, 32) * sq).astype(BF16)
    ckv = _rms_norm(h[:, MLA_Q_LORA:MLA_Q_LORA + MLA_KV_LORA], kvn_ref[...])
    blk = h[:, 640:768]
    kpe = blk * ck_ref[...] + _roll_lanes_left(blk, 32) * sk_ref[...]
    rows_ref[:MLA_KV_LORA, :] = ckv.T
    rows_ref[MLA_KV_LORA:, :] = kpe.T[:MLA_ROPE, :]
    ckvb = ckv.astype(BF16)
    kpe_mid = pltpu.roll(kpe, 64, axis=1)
    k_ref[...] = (_dot(ckvb, wuk_ref[...]) + jnp.concatenate([kpe_mid] * MLA_HEADS, axis=1)).astype(BF16)
    vt_ref[0] = _dot_nt(wuvt_ref[...], ckvb).astype(BF16)


MLA_TQ = 512


def _mla_pre(h, qn, kvn, wq, wuk, wuvt, tabs, seq, rows_t, layer, n_layers):
    m = h.shape[0]
    tm = MLA_TQ
    nblk = seq // tm
    tab = pl.BlockSpec((tm, LANES), lambda i: (i % nblk, 0))
    row = lambda w: pl.BlockSpec((tm, w), lambda i: (i, 0))
    hv = MLA_HEADS * MLA_V
    in_specs = [pl.BlockSpec((tm, 768), lambda i: (i, 0)), _resident(qn.shape), _resident(kvn.shape),
                _resident(wq.shape), _resident(wuk.shape), _resident(wuvt.shape), tab, tab, tab, tab]
    args = [h, qn, kvn, wq, wuk, wuvt, *tabs]
    aliases = {}
    if rows_t is not None:
        in_specs.append(pl.BlockSpec(memory_space=pl.ANY))
        args.append(rows_t)
        aliases = {len(args) - 1: 0}
    return pl.pallas_call(
        _mla_pre_kernel,
        grid=(m // tm,),
        in_specs=in_specs,
        out_specs=[pl.BlockSpec((None, None, MLA_KV_DIM, tm), lambda i: (layer, i // nblk, 0, i % nblk)),
                   row(1024), row(1024), pl.BlockSpec((1, hv, tm), lambda i: (i, 0, 0))],
        out_shape=[jax.ShapeDtypeStruct((n_layers, m // seq, MLA_KV_DIM, seq), F32),
                   jax.ShapeDtypeStruct((m, 1024), BF16),
                   jax.ShapeDtypeStruct((m, 1024), BF16), jax.ShapeDtypeStruct((m // tm, hv, tm), BF16)],
        input_output_aliases=aliases,
        compiler_params=_cparams(("arbitrary",), VMEM_LIMIT),
        name="mla_pre",
    )(*args)


def _mla_attn_kernel(q_ref, k_ref, vt_ref, o_ref):
    qi = pl.program_id(2)
    t = MLA_TQ
    qs = [q_ref[:, j * LANES:(j + 1) * LANES] for j in range(2)]
    krow = lax.broadcasted_iota(jnp.int32, (t, t), 0)
    qcol = lax.broadcasted_iota(jnp.int32, (t, t), 1)

    def update(kb, carry, diag):
        out = []
        for j in range(2):
            m, l, acc = carry[j]
            kk = k_ref[pl.ds(pl.multiple_of(kb * t, t), t), j * LANES:(j + 1) * LANES]
            s = _dot_nt(kk, qs[j]) * MLA_SCALE
            if diag:
                s = jnp.where(krow <= qcol, s, NEG_INF)
            m_new = jnp.maximum(m, jnp.max(s, axis=0, keepdims=True))
            alpha = jnp.exp(m - m_new)
            p = jnp.exp(s - m_new)
            vt = vt_ref[kb, j * MLA_V:(j + 1) * MLA_V, :]
            out.append((m_new, alpha * l + jnp.sum(p, axis=0, keepdims=True),
                        alpha * acc + _dot(vt, p.astype(BF16))))
        return tuple(out)

    init = tuple((jnp.full((1, t), NEG_INF, F32), jnp.zeros((1, t), F32), jnp.zeros((MLA_V, t), F32))
                 for _ in range(2))
    carry = lax.fori_loop(0, qi, lambda kb, c: update(kb, c, False), init)
    carry = update(qi, carry, True)
    ot = jnp.concatenate([acc / l for _, l, acc in carry], axis=0)
    o_ref[...] = ot.T.astype(BF16)


def _mla_attn(q, k, vt, batch, seq):
    nq = seq // MLA_TQ
    vt4 = vt.reshape(batch, nq, MLA_HEADS * MLA_V, MLA_TQ)
    return pl.pallas_call(
        _mla_attn_kernel,
        grid=(batch, MLA_HEADS // 2, nq),
        in_specs=[pl.BlockSpec((MLA_TQ, 256), lambda b, p, i: (b * nq + i, p)),
                  pl.BlockSpec((seq, 256), lambda b, p, i: (b, p)),
                  pl.BlockSpec((None, nq, 2 * MLA_V, MLA_TQ), lambda b, p, i: (b, 0, p, 0))],
        out_specs=pl.BlockSpec((MLA_TQ, LANES), lambda b, p, i: (b * nq + i, p)),
        out_shape=jax.ShapeDtypeStruct((batch * seq, MLA_HEADS * MLA_V), BF16),
        compiler_params=_cparams(("parallel", "parallel", "arbitrary"), VMEM_LIMIT),
        name="mla_attn",
    )(q, k, vt4)


GLA_TB = 256
GLA_SLOTS = 12
HK = GLA_HEADS * GLA_DK
HV = GLA_HEADS * GLA_DV


def _gla_constants():
    c = GLA_CHUNK
    t = np.arange(c)[:, None]
    s = np.arange(c)[None, :]
    slots = []
    zero = np.zeros((c, c), bool)
    for p in (1, 2, 3):
        slots.append((t // 16 == p) & (s < 16 * p))
    slots.append(zero)
    for p in (1, 2, 3):
        slots.append((t // 16 == s // 16) & ((t % 16) // 4 == p) & (s % 16 < 4 * p))
    slots.append(zero)
    for d in (1, 2, 3):
        slots.append((t == s + d) & (t // 4 == s // 4))
    slots.append(t == s)
    mask = np.concatenate(slots, axis=1).astype(np.float32)
    assert (sum(slots) == (t >= s)).all()
    mask = np.tile(mask, (GLA_HEADS, 1))
    tri = (t >= s).astype(np.float32)
    hm = np.kron(np.eye(GLA_HEADS), np.ones((c, GLA_DK))).astype(np.float32)
    bd = np.kron(np.eye(GLA_HEADS), np.ones((GLA_DV, GLA_DK))).astype(np.float32)
    return (jnp.asarray(tri, BF16), jnp.asarray(hm, F32), jnp.asarray(mask, F32), jnp.asarray(bd, F32))


def _log_sigmoid(x):
    return jnp.minimum(x, 0.0) - jnp.log1p(jnp.exp(-jnp.abs(x)))


def _split3(x):
    hi = x.astype(BF16)
    r1 = x - hi.astype(F32)
    mid = r1.astype(BF16)
    lo = (r1 - mid.astype(F32)).astype(BF16)
    return hi, mid, lo


def _bc_rows(row, n):
    return jnp.broadcast_to(row, (n, row.shape[1]))


def _gla_kernel(blk_ref, q_ref, v_ref, r_ref, k_ref, wg_ref, bg_ref, gn_ref, tri_ref, hm_ref, mask_ref, bd_ref,
                o_ref, st_ref, st_scr):
    c = GLA_CHUNK

    @pl.when(pl.program_id(1) == 0)
    def _():
        st_scr[...] = jnp.zeros_like(st_scr)

    rows = lax.broadcasted_iota(jnp.int32, (c, HK), 0)
    rm4 = rows & 3

    def chunk(ci, _):
        sl = pl.ds(pl.multiple_of(ci * c, c), c)
        pre = _dot(blk_ref[sl, :].astype(BF16), wg_ref[...]) + bg_ref[...]
        g = _log_sigmoid(pre) * (1.0 / GLA_TAU)
        cum3 = _dot(tri_ref[...], jnp.concatenate(_split3(g), axis=1))
        cum = cum3[:, :HK] + cum3[:, HK:2 * HK] + cum3[:, 2 * HK:]
        q = q_ref[sl, :] * (GLA_DK ** -0.5)
        k = k_ref[sl, :]
        vb = v_ref[sl, :].astype(BF16)

        def kvar(e):
            return (k * jnp.exp(jnp.minimum(e, 0.0))).astype(BF16)

        zeros_k = jnp.zeros((c, HK), BF16)
        bnd = [cum[16 * p - 1:16 * p, :] for p in (1, 2, 3)]
        base1 = jnp.concatenate([jnp.zeros((16, HK), F32)] + [_bc_rows(b, 16) for b in bnd], axis=0)
        q1 = q * jnp.exp(cum - base1)
        k1 = jnp.concatenate([kvar(_bc_rows(b, c) - cum) for b in bnd] + [zeros_k], axis=0)
        sh = [pltpu.roll(cum, d, axis=0) for d in (1, 2, 3, 4)]
        base2 = jnp.where(rm4 == 0, sh[0], jnp.where(rm4 == 1, sh[1], jnp.where(rm4 == 2, sh[2], sh[3])))
        base2 = jnp.where(rows < 4, 0.0, base2)
        q2 = q * jnp.exp(cum - base2)
        k2 = []
        for p in (1, 2, 3):
            bk = jnp.concatenate([_bc_rows(cum[16 * i + 4 * p - 1:16 * i + 4 * p, :], 16) for i in range(4)], axis=0)
            k2.append(kvar(bk - cum))
        k2 = jnp.concatenate(k2 + [zeros_k], axis=0)
        k3 = jnp.concatenate([kvar(pltpu.roll(cum, c - d, axis=0) - cum) for d in (1, 2, 3)] + [k.astype(BF16)],
                             axis=0)
        hm = hm_ref[...]

        def stack(x):
            return (jnp.concatenate([x] * GLA_HEADS, axis=0) * hm).astype(BF16)

        p_all = jnp.concatenate([_dot_nt(stack(q1), k1), _dot_nt(stack(q2), k2), _dot_nt(stack(q), k3)], axis=1)
        pm = (p_all * mask_ref[...]).astype(BF16)
        st = st_scr[...]
        o = _dot_nt((q * jnp.exp(cum)).astype(BF16), st.astype(BF16))
        intra = []
        for h in range(GLA_HEADS):
            vh = vb[:, h * GLA_DV:(h + 1) * GLA_DV]
            intra.append(_dot(pm[h * c:(h + 1) * c, :], jnp.concatenate([vh] * GLA_SLOTS, axis=0)))
        o = o + jnp.concatenate(intra, axis=1)
        last = cum[c - 1:c, :]
        st_scr[...] = st * jnp.exp(last) + _dot_tn(vb, kvar(_bc_rows(last, c) - cum)) * bd_ref[...]
        r = r_ref[sl, :]
        outs = []
        for h in range(GLA_HEADS):
            lanes = slice(h * GLA_DV, (h + 1) * GLA_DV)
            outs.append(_rms_norm(o[:, lanes], gn_ref[...]) * _silu(r[:, lanes]))
        o_ref[sl, :] = jnp.concatenate(outs, axis=1).astype(BF16)
        return 0

    lax.fori_loop(0, GLA_TB // c, chunk, 0)

    @pl.when(pl.program_id(1) == pl.num_programs(1) - 1)
    def _():
        for h in range(GLA_HEADS):
            blk = st_scr[h * GLA_DV:(h + 1) * GLA_DV, :]
            st_ref[0, h] = blk.T[h * GLA_DK:(h + 1) * GLA_DK, :]


def _gla(h, wg, bg, gn, consts, batch, seq):
    nt = seq // GLA_TB
    tri, hm, mask, bd = consts
    rowblk = lambda w, cb: pl.BlockSpec((GLA_TB, w), lambda b, t: (b * nt + t, cb))
    return pl.pallas_call(
        _gla_kernel,
        grid=(batch, nt),
        in_specs=[rowblk(LANES, 5), rowblk(HK, 3), rowblk(HV, 2), rowblk(HV, 3), rowblk(HK, 8),
                  _resident(wg.shape), _resident(bg.shape), _resident(gn.shape),
                  _resident(tri.shape), _resident(hm.shape), _resident(mask.shape), _resident(bd.shape)],
        out_specs=[pl.BlockSpec((GLA_TB, HV), lambda b, t: (b * nt + t, 0)),
                   pl.BlockSpec((1, GLA_HEADS, GLA_DK, GLA_DV), lambda b, t: (b, 0, 0, 0))],
        out_shape=[jax.ShapeDtypeStruct((batch * seq, HV), BF16),
                   jax.ShapeDtypeStruct((batch, GLA_HEADS, GLA_DK, GLA_DV), F32)],
        scratch_shapes=[pltpu.VMEM((HV, HK), F32)],
        compiler_params=_cparams(("parallel", "arbitrary"), VMEM_LIMIT),
        name="gla",
    )(h, h, h, h, h, wg, bg, gn, tri, hm, mask, bd)


def _alibi_slope(group, head):
    return 2.0 ** (-8.0 * (group * DIL_HPG + head + 1) / (DIL_GROUPS * DIL_HPG))


DIL_TM = 256
QKV_G = 3 * MIX_C


def _proj_dil_kernel(x_ref, w_ref, *rest, n_prev, tiles):
    p_refs = rest[n_prev:n_prev + DIL_GROUPS]
    r1_ref, r2_ref, r3_ref, acc_ref = rest[n_prev + DIL_GROUPS:]
    t = pl.program_id(0) % tiles
    res = _dot(x_ref[...].astype(BF16), w_ref[...])

    def put_rows(g, lo, width):
        for which in range(2):
            c0 = g * QKV_G + (1 + which) * MIX_C
            p_refs[g][which] = res[lo:lo + width, c0:c0 + MIX_C].T.reshape(DIL_HPG, DIL_HD, width)

    put_rows(2, 0, DIL_TM)
    pl.when(t >= tiles - DIL_PAIRS[1][0] // DIL_TM)(lambda: put_rows(1, 0, DIL_TM))
    pl.when(t == tiles - 1)(lambda: put_rows(0, DIL_TM - DIL_PAIRS[0][0], DIL_PAIRS[0][0]))
    r1_ref[...] = res[:, :QKV_G].astype(BF16)
    nb = QKV_G // LANES
    for c in range(2 * nb):
        acc_ref[c] = res[:, QKV_G + c * LANES:QKV_G + (c + 1) * LANES]
    for g, r_ref in ((1, r2_ref), (2, r3_ref)):
        dil = DIL_PAIRS[g][1]
        for r in range(dil):
            for c in range(nb):
                r_ref[0, r, :, c * LANES:(c + 1) * LANES] = (
                    acc_ref[(g - 1) * nb + c, pl.ds(r, DIL_TM // dil, stride=dil), :].astype(BF16))


def _proj_dil(x, w, seq, prev, layer, n_layers):
    m, k = x.shape
    tm = DIL_TM
    nt = m // tm
    tiles = seq // tm
    res = lambda dil: ((nt, dil, tm // dil, QKV_G), pl.BlockSpec((1, dil, tm // dil, QKV_G), lambda i: (i, 0, 0, 0)))
    (s2, b2), (s3, b3) = res(DIL_PAIRS[1][1]), res(DIL_PAIRS[2][1])
    p_shapes, p_specs = [], []
    for window, _ in DIL_PAIRS:
        keep = min(window, seq)
        width = min(keep, tm)
        first = tiles - max(keep // tm, 1)
        p_shapes.append(jax.ShapeDtypeStruct((n_layers, m // seq, 2, DIL_HPG, DIL_HD, keep), F32))
        p_specs.append(pl.BlockSpec(
            (None, None, 2, DIL_HPG, DIL_HD, width),
            lambda i, first=first: (layer, i // tiles, 0, 0, 0, jnp.maximum(i % tiles - first, 0))))
    in_specs = [pl.BlockSpec((tm, k), lambda i: (i, 0)), _resident(w.shape)]
    args = [x, w]
    aliases = {}
    if prev is not None:
        in_specs += [pl.BlockSpec(memory_space=pl.ANY)] * DIL_GROUPS
        aliases = {len(args) + g: g for g in range(DIL_GROUPS)}
        args += list(prev)
    outs = pl.pallas_call(
        functools.partial(_proj_dil_kernel, n_prev=len(aliases), tiles=tiles),
        grid=(nt,),
        in_specs=in_specs,
        out_specs=p_specs + [pl.BlockSpec((tm, QKV_G), lambda i: (i, 0)), b2, b3],
        out_shape=p_shapes + [jax.ShapeDtypeStruct((m, QKV_G), BF16),
                              jax.ShapeDtypeStruct(s2, BF16), jax.ShapeDtypeStruct(s3, BF16)],
        scratch_shapes=[pltpu.VMEM((2 * QKV_G // LANES, tm, LANES), F32)],
        input_output_aliases=aliases,
        compiler_params=_cparams(("arbitrary",), VMEM_LIMIT),
        name="proj_dil",
    )(*args)
    return outs[:DIL_GROUPS], outs[DIL_GROUPS:]


def _dil_attn_kernel(q_ref, kc_ref, vc_ref, *rest, group, dil, has_prev):
    if has_prev:
        kp_ref, vp_ref, o_ref, lse_ref = rest
    else:
        o_ref, lse_ref = rest
    n = DIL_STEPS
    nb = pl.program_id(2)
    nk = 2 * n if has_prev else n
    qi = lax.broadcasted_iota(jnp.int32, (n, nk), 0)
    ki = lax.broadcasted_iota(jnp.int32, (n, nk), 1)
    steps = qi + (n if has_prev else 0) - ki
    valid = (steps >= 0) & (steps <= n)
    if has_prev:
        valid = valid & ((ki >= n) | (nb > 0))
    dist = (steps * dil).astype(F32)
    lane = lax.broadcasted_iota(jnp.int32, (n, LANES), 1)
    low = lane < DIL_HD
    rows = lambda ref: ref[...].reshape(n, MIX_C)
    q = rows(q_ref)
    k = jnp.concatenate([rows(kp_ref), rows(kc_ref)], axis=0) if has_prev else rows(kc_ref)
    v = jnp.concatenate([rows(vp_ref), rows(vc_ref)], axis=0) if has_prev else rows(vc_ref)
    o_all, lse_all = [], []
    for j in range(DIL_HPG // 2):
        cols = slice(j * LANES, (j + 1) * LANES)
        q2, k2, v2 = q[:, cols], k[:, cols], v[:, cols]
        outs, lses = [], []
        for par in range(2):
            qh = jnp.where(low if par == 0 else ~low, q2, jnp.zeros_like(q2))
            s = _dot_nt(qh, k2) * (DIL_HD ** -0.5) - _alibi_slope(group, 2 * j + par) * dist
            s = jnp.where(valid, s, NEG_INF)
            m = jnp.max(s, axis=-1, keepdims=True)
            p = jnp.exp(s - m)
            l = jnp.sum(p, axis=-1, keepdims=True)
            outs.append(_dot(p.astype(BF16), v2) / l)
            lses.append(jnp.broadcast_to(m + jnp.log(l), (n, LANES)))
        o_all.append(jnp.where(low, outs[0], outs[1]))
        lse_all.append(jnp.where(low, lses[0], lses[1]))
    o_ref[...] = jnp.concatenate(o_all, axis=1).reshape(o_ref.shape)
    lse_ref[...] = jnp.concatenate(lse_all, axis=1).reshape(lse_ref.shape)


def _dil_attn(qkv_r, group, batch, seq):
    _, dil = DIL_PAIRS[group]
    n = DIL_STEPS
    _, nt, _, rr, _ = qkv_r.shape
    tpb = n // rr
    nblk = nt // tpb
    has_prev = nblk > 1
    cur = lambda c: pl.BlockSpec((None, tpb, None, rr, MIX_C), lambda b, r, i: (b, i, r, 0, c))
    prev = lambda c: pl.BlockSpec((None, tpb, None, rr, MIX_C), lambda b, r, i: (b, jnp.maximum(i - 1, 0), r, 0, c))
    in_specs = [cur(0), cur(1), cur(2)] + ([prev(1), prev(2)] if has_prev else [])
    out = pl.BlockSpec((None, tpb, None, rr, MIX_C), lambda b, r, i: (b, i, r, 0, 0))
    return pl.pallas_call(
        functools.partial(_dil_attn_kernel, group=group, dil=dil, has_prev=has_prev),
        grid=(batch, dil, nblk),
        in_specs=in_specs,
        out_specs=[out, out],
        out_shape=[jax.ShapeDtypeStruct((batch, nt, dil, rr, MIX_C), F32)] * 2,
        compiler_params=_cparams(("parallel", "parallel", "arbitrary"), VMEM_LIMIT),
        name="dil_attn%d" % group,
    )(*([qkv_r] * len(in_specs)))


def _dil_mix_ln_kernel(o1, o2, o3, l1, l2, l3, w_ref, x_ref, g_ref, b_ref, o_ref, *scr):
    nb = MIX_C // LANES
    for src, dst, dil in ((o2, scr[0], DIL_PAIRS[1][1]), (l2, scr[1], DIL_PAIRS[1][1]),
                          (o3, scr[2], DIL_PAIRS[2][1]), (l3, scr[3], DIL_PAIRS[2][1])):
        for r in range(dil):
            for c in range(nb):
                dst[c, pl.ds(r, DIL_TM // dil, stride=dil), :] = src[0, r, :, c * LANES:(c + 1) * LANES]
    whole = lambda s: jnp.concatenate([s[c] for c in range(nb)], axis=1)
    os_ = [o1[...], whole(scr[0]), whole(scr[2])]
    ls = [l1[...], whole(scr[1]), whole(scr[3])]
    mx = jnp.maximum(jnp.maximum(ls[0], ls[1]), ls[2])
    es = [jnp.exp(l - mx) for l in ls]
    mix = (es[0] * os_[0] + es[1] * os_[1] + es[2] * os_[2]) / (es[0] + es[1] + es[2])
    z = DN_ALPHA * x_ref[...] + _dot(mix.astype(BF16), w_ref[...])
    o_ref[...] = _layer_norm(z, g_ref[...], b_ref[...])


def _dil_mix_ln(os_, lses, w, x, g, b):
    m = x.shape[0]
    tm = DIL_TM
    nt = m // tm
    row = lambda wd: pl.BlockSpec((tm, wd), lambda i: (i, 0))

    def res(a):
        dil = a.shape[2]
        a = a.reshape(nt, dil, tm // dil, MIX_C)
        return a, pl.BlockSpec((1, dil, tm // dil, MIX_C), lambda i: (i, 0, 0, 0))

    flat = lambda a: a.reshape(m, MIX_C)
    (o2, s2), (o3, s3), (l2, _), (l3, _) = res(os_[1]), res(os_[2]), res(lses[1]), res(lses[2])
    return pl.pallas_call(
        _dil_mix_ln_kernel,
        grid=(nt,),
        in_specs=[row(MIX_C), s2, s3, row(MIX_C), s2, s3,
                  _resident(w.shape), row(D_MODEL), _resident(g.shape), _resident(b.shape)],
        out_specs=row(D_MODEL),
        out_shape=jax.ShapeDtypeStruct((m, D_MODEL), F32),
        scratch_shapes=[pltpu.VMEM((MIX_C // LANES, tm, LANES), F32)] * 4,
        compiler_params=_cparams(("parallel",), VMEM_LIMIT),
        name="dil_mix_ln",
    )(flat(os_[0]), o2, o3, flat(lses[0]), l2, l3, w, x, g, b)


def _mla_pre_s_kernel(h_ref, qn_ref, kvn_ref, wqn_ref, wqp_ref, wukt_ref, ck_ref, sk_ref, rows_ref, q_ref):
    h = h_ref[...]
    qn = _rms_norm(h[:, :MLA_Q_LORA], qn_ref[...]).astype(BF16)
    qnope = _dot(qn, wqn_ref[...]).astype(BF16)
    qp = _dot(qn, wqp_ref[...])
    cq = jnp.concatenate([ck_ref[...]] * MLA_HEADS, axis=1)
    sq = jnp.concatenate([sk_ref[...]] * MLA_HEADS, axis=1)
    qp = qp * cq + _roll_lanes_left(qp, 32) * sq
    ckv = _rms_norm(h[:, MLA_Q_LORA:MLA_Q_LORA + MLA_KV_LORA], kvn_ref[...])
    blk = h[:, 640:768]
    kpe = blk * ck_ref[...] + _roll_lanes_left(blk, 32) * sk_ref[...]
    rows_ref[:, :MLA_KV_LORA] = ckv
    rows_ref[:, MLA_KV_LORA:] = kpe[:, :MLA_ROPE]
    for hd in range(MLA_HEADS):
        cols = slice(hd * LANES, (hd + 1) * LANES)
        q_ref[hd, :, :MLA_KV_LORA] = _dot(qnope[:, cols], wukt_ref[hd])
        q_ref[hd, :, MLA_KV_LORA:] = qp[:, hd * LANES:hd * LANES + MLA_ROPE]


def _mla_pre_s(h, qn, kvn, wqn, wqp, wukt, tabs):
    m = h.shape[0]
    full = lambda a: _resident(a.shape)
    return pl.pallas_call(
        _mla_pre_s_kernel,
        grid=(1,),
        in_specs=[pl.BlockSpec((m, 768), lambda i: (0, 0)), full(qn), full(kvn), full(wqn), full(wqp), full(wukt)]
        + [full(t) for t in tabs[2:]],
        out_specs=[pl.BlockSpec((m, MLA_KV_DIM), lambda i: (0, 0)),
                   pl.BlockSpec((MLA_HEADS, m, MLA_KV_DIM), lambda i: (0, 0, 0))],
        out_shape=[jax.ShapeDtypeStruct((m, MLA_KV_DIM), F32), jax.ShapeDtypeStruct((MLA_HEADS, m, MLA_KV_DIM), F32)],
        compiler_params=_cparams(("arbitrary",), VMEM_LIMIT),
        name="mla_pre_s",
    )(h, qn, kvn, wqn, wqp, wukt, *tabs[2:])


DEC_PAGES = 32


def _mla_dec_kernel(pt_ref, q_ref, new_ref, *rest):
    pages = rest[:DEC_PAGES]
    o_ref, m_scr, l_scr, acc_scr = rest[DEC_PAGES:]
    step = pl.program_id(1)
    q = q_ref[0]
    qb = q.astype(BF16)

    @pl.when(step == 0)
    def _():
        new = new_ref[0]
        m_scr[...] = jnp.sum(q * new, axis=-1, keepdims=True) * MLA_SCALE
        l_scr[...] = jnp.ones_like(l_scr)
        acc_scr[...] = jnp.broadcast_to(new[:, :MLA_KV_LORA], acc_scr.shape)

    kvt = jnp.concatenate([p[...].astype(BF16) for p in pages], axis=1)
    s = _dot(qb, kvt) * MLA_SCALE
    m_old = m_scr[...]
    m_new = jnp.maximum(m_old, jnp.max(s, axis=-1, keepdims=True))
    alpha = jnp.exp(m_old - m_new)
    p = jnp.exp(s - m_new)
    m_scr[...] = m_new
    l_scr[...] = alpha * l_scr[...] + jnp.sum(p, axis=-1, keepdims=True)
    acc_scr[...] = alpha * acc_scr[...] + _dot_nt(p.astype(BF16), kvt[:MLA_KV_LORA, :])

    @pl.when(step == pl.num_programs(1) - 1)
    def _():
        o_ref[0] = acc_scr[...] / l_scr[...]


def _mla_dec(page_table, q_abs, rows_new, cache_t, layer):
    batch = q_abs.shape[0]
    n_pages = page_table.shape[1]
    steps = n_pages // DEC_PAGES
    pt = page_table.reshape(-1)

    def page_spec(j):
        return pl.BlockSpec((None, None, MLA_KV_DIM, PAGE_SIZE),
                            lambda b, s, pt_ref: (layer, pt_ref[b * n_pages + s * DEC_PAGES + j], 0, 0))

    grid_spec = pltpu.PrefetchScalarGridSpec(
        num_scalar_prefetch=1,
        grid=(batch, steps),
        in_specs=[pl.BlockSpec((1, MLA_HEADS, MLA_KV_DIM), lambda b, s, pt_ref: (b, 0, 0)),
                  pl.BlockSpec((1, 1, MLA_KV_DIM), lambda b, s, pt_ref: (b, 0, 0))]
        + [page_spec(j) for j in range(DEC_PAGES)],
        out_specs=pl.BlockSpec((1, MLA_HEADS, MLA_KV_LORA), lambda b, s, pt_ref: (b, 0, 0)),
        scratch_shapes=[pltpu.VMEM((MLA_HEADS, 1), F32), pltpu.VMEM((MLA_HEADS, 1), F32),
                        pltpu.VMEM((MLA_HEADS, MLA_KV_LORA), F32)],
    )
    return pl.pallas_call(
        _mla_dec_kernel,
        grid_spec=grid_spec,
        out_shape=jax.ShapeDtypeStruct((batch, MLA_HEADS, MLA_KV_LORA), F32),
        compiler_params=_cparams(("parallel", "arbitrary"), VMEM_LIMIT),
        name="mla_dec",
    )(pt, q_abs, rows_new.reshape(batch, 1, MLA_KV_DIM), *([cache_t] * DEC_PAGES))


def _gla_dec_kernel(blk_ref, q_ref, v_ref, r_ref, k_ref, s_ref, wg_ref, bg_ref, gn_ref, o_ref, so_ref):
    pre = _dot(blk_ref[0].astype(BF16), wg_ref[...]) + bg_ref[...]
    a = jnp.exp(_log_sigmoid(pre) * (1.0 / GLA_TAU))
    q = q_ref[0] * (GLA_DK ** -0.5)
    k = k_ref[0]
    v = v_ref[0]
    r = r_ref[0]
    eye = (lax.broadcasted_iota(jnp.int32, (GLA_DK, GLA_DK), 0)
           == lax.broadcasted_iota(jnp.int32, (GLA_DK, GLA_DK), 1))
    outs = []
    for h in range(GLA_HEADS):
        kl = slice(h * GLA_DK, (h + 1) * GLA_DK)
        vl = slice(h * GLA_DV, (h + 1) * GLA_DV)
        diag = lambda row: jnp.where(eye, jnp.broadcast_to(row, (GLA_DK, GLA_DK)), 0.0)
        lhs = jnp.concatenate([diag(a[:, kl]), diag(k[:, kl])], axis=1)
        rhs = jnp.concatenate([s_ref[0, h], jnp.broadcast_to(v[:, vl], (GLA_DK, GLA_DV))], axis=0)
        s_new = jnp.dot(lhs, rhs, preferred_element_type=F32, precision=lax.Precision.HIGHEST)
        so_ref[0, h] = s_new
        o = jnp.dot(jnp.broadcast_to(q[:, kl], (8, GLA_DK)), s_new, preferred_element_type=F32,
                    precision=lax.Precision.HIGHEST)[:1]
        outs.append(_rms_norm(o, gn_ref[...]) * _silu(r[:, vl]))
    o_ref[0] = jnp.concatenate(outs, axis=1)


def _gla_dec(h, state, layer, wg, bg, gn):
    batch = h.shape[0]
    x = h.reshape(batch, 1, IN_A_PAD)
    rowblk = lambda w, cb: pl.BlockSpec((1, 1, w), lambda b: (b, 0, cb))
    st_in = pl.BlockSpec((None, 1, GLA_HEADS, GLA_DK, GLA_DV), lambda b: (layer, b, 0, 0, 0))
    st = pl.BlockSpec((1, GLA_HEADS, GLA_DK, GLA_DV), lambda b: (b, 0, 0, 0))
    o, s_new = pl.pallas_call(
        _gla_dec_kernel,
        grid=(batch,),
        in_specs=[rowblk(LANES, 5), rowblk(HK, 3), rowblk(HV, 2), rowblk(HV, 3), rowblk(HK, 8), st_in,
                  _resident(wg.shape), _resident(bg.shape), _resident(gn.shape)],
        out_specs=[pl.BlockSpec((1, 1, HV), lambda b: (b, 0, 0)), st],
        out_shape=[jax.ShapeDtypeStruct((batch, 1, HV), F32), jax.ShapeDtypeStruct(state.shape[1:], F32)],
        compiler_params=_cparams(("parallel",), VMEM_LIMIT),
        name="gla_dec",
    )(x, x, x, x, x, state, wg, bg, gn)
    return o.reshape(batch, HV), s_new


def _dil_dec_kernel(x_ref, c1_ref, c2_ref, c3_ref, o_ref):
    cols = []
    for h in range(DIL_HPG):
        outs, lses = [], []
        for g, c_ref in enumerate((c1_ref, c2_ref, c3_ref)):
            window, dil = DIL_PAIRS[g]
            pos = lax.broadcasted_iota(jnp.int32, (1, window), 1)
            q, k_new, v_new = (x_ref[0, 3 * g + w, :, h:h + 1] for w in range(3))
            s = jnp.sum(c_ref[0, h] * q, axis=0, keepdims=True) * (DIL_HD ** -0.5)
            s = s - _alibi_slope(g, h) * (window - pos).astype(F32)
            s = jnp.where((pos & (dil - 1)) == 0, s, NEG_INF)
            s_new = jnp.sum(q * k_new, axis=0, keepdims=True) * (DIL_HD ** -0.5)
            m = jnp.maximum(jnp.max(s, axis=1, keepdims=True), s_new)
            p = jnp.exp(s - m)
            p_new = jnp.exp(s_new - m)
            l = jnp.sum(p, axis=1, keepdims=True) + p_new
            outs.append((jnp.sum(c_ref[1, h] * p, axis=1, keepdims=True) + p_new * v_new) / l)
            lses.append(m + jnp.log(l))
        mx = jnp.maximum(jnp.maximum(lses[0], lses[1]), lses[2])
        es = [jnp.exp(l - mx) for l in lses]
        cols.append((es[0] * outs[0] + es[1] * outs[1] + es[2] * outs[2]) / (es[0] + es[1] + es[2]))
    o_ref[0] = jnp.concatenate(cols, axis=1)


def _dil_dec(qkv_t, caches_t, layer):
    batch = qkv_t.shape[0]
    specs = [pl.BlockSpec((1, 3 * DIL_GROUPS, DIL_HD, DIL_HPG), lambda b: (b, 0, 0, 0))]
    for (window, _), c in zip(DIL_PAIRS, caches_t):
        assert c.shape[-1] == window
        specs.append(pl.BlockSpec((None, None, 2, DIL_HPG, DIL_HD, window), lambda b: (layer, b, 0, 0, 0, 0)))
    return pl.pallas_call(
        _dil_dec_kernel,
        grid=(batch,),
        in_specs=specs,
        out_specs=pl.BlockSpec((1, DIL_HD, DIL_HPG), lambda b: (b, 0, 0)),
        out_shape=jax.ShapeDtypeStruct((batch, DIL_HD, DIL_HPG), F32),
        compiler_params=_cparams(("parallel",), VMEM_LIMIT),
        name="dil_dec",
    )(qkv_t, *caches_t)


def _even_in_cols():
    o_cq, o_kv, o_gq, o_gk, o_gv, o_lr, o_gr = 0, 384, 672, 928, 1184, 1696, 1712
    r = np.arange
    kpe = o_kv + MLA_KV_LORA
    idx = np.concatenate([r(o_cq, o_cq + 384), r(o_kv, o_kv + 256), r(kpe, kpe + 32), r(kpe + 16, kpe + 32),
                          r(kpe, kpe + 16), r(o_lr, o_lr + 16), np.full(48, -1), r(o_gq, o_gq + 256),
                          r(o_gv, o_gv + 512), r(o_gr, o_gr + 512), r(o_gk, o_gk + 256)])
    assert idx.shape[0] == IN_A_PAD
    return idx


def _gather_cols(w, idx):
    return jnp.where(jnp.asarray(idx >= 0)[None, :], jnp.take(w, jnp.asarray(np.maximum(idx, 0)), axis=1), 0.0)


def _uq_cols(kind):
    idx = []
    for h in range(MLA_HEADS):
        b = h * (MLA_NOPE + MLA_ROPE)
        nope = np.arange(b, b + 64)
        x1, x2 = np.arange(b + 64, b + 80), np.arange(b + 80, b + 96)
        if kind == "full":
            idx += [nope, x1, x2, x2, x1]
        elif kind == "nope":
            idx += [nope, np.full(64, -1)]
        else:
            idx += [x1, x2, x2, x1, np.full(64, -1)]
    return np.concatenate(idx)


def _odd_in_cols():
    blk = lambda which, g: np.arange((which * DIL_GROUPS + g) * MIX_C, (which * DIL_GROUPS + g + 1) * MIX_C)
    return np.concatenate([blk(which, g) for g in range(DIL_GROUPS) for which in range(3)])


TM_DENSE = 512
TM_WIDE = 256


def kernel(x_prompt, x_sample, cache_mla, state_gla, cache_dil_w128, cache_dil_w512, cache_dil_w2048, page_table,
           w_in_a, mla_q_norm, mla_w_uq, mla_kv_norm, mla_w_uk, mla_w_uv, gla_w_gate2, gla_b_gate, gla_norm, w_out_a,
           w_in_c, w_out_c, ffn_w_in, ffn_w_out, ln_g, ln_b):
    batch, seq, _ = x_prompt.shape
    dbatch = x_sample.shape[0]
    assert seq % MLA_TQ == 0 and seq % DIL_TM == 0 and seq >= DIL_PAIRS[-1][0]
    n_even, n_odd = (DEPTH + 1) // 2, DEPTH // 2
    cache_mla_t = jnp.transpose(cache_mla, (0, 1, 3, 2))
    dil_caches_t = [jnp.transpose(c, (0, 1, 3, 4, 5, 2)) for c in (cache_dil_w128, cache_dil_w512, cache_dil_w2048)]
    gla_consts = _gla_constants()
    tabs_p = _rope_tables(jnp.arange(seq, dtype=jnp.int32))
    tabs_s = _rope_tables(jnp.full((dbatch,), PAST_LEN, jnp.int32))
    even_cols, odd_cols = _even_in_cols(), _odd_in_cols()
    row2 = lambda v: v.reshape(1, -1).astype(F32)

    xp = x_prompt.reshape(batch * seq, D_MODEL)
    xs = x_sample.reshape(dbatch, D_MODEL)
    mla_s, gla_p, gla_s = [], [], []
    rows_t, dil_t = None, None
    dil_s = [[] for _ in DIL_PAIRS]

    for layer in range(DEPTH):
        i = layer // 2
        g0, b0, g1, b1 = (row2(ln_g[layer, 0]), row2(ln_b[layer, 0]), row2(ln_g[layer, 1]), row2(ln_b[layer, 1]))
        if layer % 2 == 0:
            w_in = _gather_cols(w_in_a[i], even_cols).astype(BF16)
            wq_full = _gather_cols(mla_w_uq[i], _uq_cols("full")).astype(BF16)
            wq_nope = _gather_cols(mla_w_uq[i], _uq_cols("nope")).astype(BF16)
            wq_rope = _gather_cols(mla_w_uq[i], _uq_cols("rope")).astype(BF16)
            wuk_pad = jnp.pad(mla_w_uk[i], ((0, 0), (0, 0), (0, LANES - MLA_NOPE))).reshape(MLA_KV_LORA, -1).astype(BF16)
            wuk_t = jnp.pad(jnp.transpose(mla_w_uk[i], (1, 2, 0)), ((0, 0), (0, LANES - MLA_NOPE), (0, 0))).astype(BF16)
            wuv_t = mla_w_uv[i].reshape(MLA_KV_LORA, -1).T.astype(BF16)
            wuv_bd = (jnp.eye(MLA_HEADS, dtype=F32)[:, None, :, None]
                      * jnp.transpose(mla_w_uv[i], (1, 0, 2))[:, :, None, :]).reshape(
                          MLA_HEADS * MLA_KV_LORA, MLA_HEADS * MLA_V).astype(BF16)
            wg = jnp.pad(gla_w_gate2[i], ((64, LANES - 64 - GLA_GATE_RANK), (0, 0))).astype(BF16)
            bg, gn = row2(gla_b_gate[i]), row2(gla_norm[i])
            qn, kvn = row2(mla_q_norm[i]), row2(mla_kv_norm[i])
            w_out = w_out_a[i].astype(BF16)

            h = _proj(xp, w_in, TM_DENSE)
            rows_t, qf, kf, vt = _mla_pre(h, qn, kvn, wq_full, wuk_pad, wuv_t, tabs_p, seq, rows_t, i, n_even)
            mla_out = _mla_attn(qf, kf, vt, batch, seq)
            gla_out, s_fin = _gla(h, wg, bg, gn, gla_consts, batch, seq)
            gla_p.append(s_fin)
            xp = _mix_ln([mla_out, gla_out], w_out, xp, g0, b0, TM_DENSE)

            h = _proj(xs, w_in, TM_DENSE)
            rows, q_abs = _mla_pre_s(h, qn, kvn, wq_nope, wq_rope, wuk_t, tabs_s)
            lat = _mla_dec(page_table, jnp.transpose(q_abs, (1, 0, 2)), rows, cache_mla_t, i)
            mla_out = _proj(lat.reshape(dbatch, MLA_HEADS * MLA_KV_LORA), wuv_bd, TM_DENSE)
            gla_out, s_fin = _gla_dec(h, state_gla, i, wg, bg, gn)
            mla_s.append(rows.reshape(dbatch, 1, MLA_KV_DIM))
            gla_s.append(s_fin)
            xs = _mix_ln([mla_out, gla_out], w_out, xs, g0, b0, TM_DENSE)
        else:
            w_in = jnp.take(w_in_c[i], jnp.asarray(odd_cols), axis=1).astype(BF16)
            w_out = w_out_c[i].astype(BF16)

            dil_t, (r1, r2, r3) = _proj_dil(xp, w_in, seq, dil_t, i, n_odd)
            nt = seq // DIL_TM
            res = [r1.reshape(batch, seq // DIL_STEPS, 1, DIL_STEPS, QKV_G)] + [
                r.reshape(batch, nt, r.shape[1], r.shape[2], QKV_G) for r in (r2, r3)]
            os_, lses = zip(*[_dil_attn(res[g], g, batch, seq) for g in range(DIL_GROUPS)])
            xp = _dil_mix_ln(os_, lses, w_out, xp, g0, b0)

            qkv = _proj(xs, w_in, TM_WIDE).reshape(dbatch, 3 * DIL_GROUPS, DIL_HPG, DIL_HD)
            mix = _dil_dec(jnp.transpose(qkv, (0, 1, 3, 2)), dil_caches_t, i)
            mix = jnp.transpose(mix, (0, 2, 1)).reshape(dbatch, MIX_C)
            for g in range(DIL_GROUPS):
                dil_s[g].append(qkv[:, 3 * g + 1:3 * g + 3].reshape(dbatch, 1, 2, DIL_HPG, DIL_HD))
            xs = _mix_ln([mix], w_out, xs, g0, b0, TM_DENSE)

        wi, wo = ffn_w_in[layer].astype(BF16), ffn_w_out[layer].astype(BF16)
        xp = _ffn(xp, wi, wo, g1, b1, TM_DENSE)
        xs = _ffn(xs, wi, wo, g1, b1, TM_DENSE)

    st = jnp.stack
    mla_p = jnp.transpose(rows_t, (0, 1, 3, 2))
    dil_p = [jnp.transpose(p, (0, 1, 5, 2, 3, 4)) for p in dil_t]
    return (xp.reshape(batch, seq, D_MODEL), xs.reshape(dbatch, 1, D_MODEL), mla_p, st(mla_s), st(gla_p), st(gla_s),
            dil_p[0], st(dil_s[0]), dil_p[1], st(dil_s[1]), dil_p[2], st(dil_s[2]))
```

```python
import functools
import math

import numpy as np
import jax
import jax.numpy as jnp
from jax import lax
from jax.experimental import pallas as pl
from jax.experimental.pallas import tpu as pltpu

F32 = jnp.float32
BF16 = jnp.bfloat16

D_MODEL = 1024
DEPTH = 4
PAST_LEN = 16384
PAGE_SIZE = 128
MLA_HEADS = 8
MLA_Q_LORA = 384
MLA_KV_LORA = 256
MLA_NOPE = 64
MLA_ROPE = 32
MLA_V = 64
MLA_KV_DIM = MLA_KV_LORA + MLA_ROPE
ROPE_BASE = 10000.0
GLA_HEADS = 4
GLA_DK = 64
GLA_DV = 128
GLA_GATE_RANK = 16
GLA_TAU = 16.0
GLA_CHUNK = 64
DIL_PAIRS = ((128, 1), (512, 4), (2048, 16))
DIL_GROUPS = 3
DIL_HPG = 8
DIL_HD = 64
DIL_STEPS = 128
D_FF = -(-8 * D_MODEL // (3 * 256)) * 256
DN_ALPHA = (2.0 * DEPTH) ** 0.25
IN_A_PAD = 2304
MIX_C = DIL_HPG * DIL_HD
IN_C = 3 * DIL_GROUPS * MIX_C
MLA_SCALE = (MLA_NOPE + MLA_ROPE) ** -0.5

V7X_VMEM_BYTES = 64 * 1024 * 1024
VMEM_LIMIT = 56 * 1024 * 1024
LANES = 128

NEG_INF = float("-inf")


def _cparams(sem, limit=None):
    return pltpu.CompilerParams(dimension_semantics=sem, vmem_limit_bytes=limit)


def _resident(shape):
    nd = len(shape)
    return pl.BlockSpec(shape, lambda *_: (0,) * nd, pipeline_mode=pl.Buffered(1))


def _layer_norm(z, g, b):
    mu = jnp.mean(z, axis=-1, keepdims=True)
    zc = z - mu
    var = jnp.mean(zc * zc, axis=-1, keepdims=True)
    return zc * lax.rsqrt(var + 1e-5) * g + b


def _rms_norm(z, g):
    return z * lax.rsqrt(jnp.mean(z * z, axis=-1, keepdims=True) + 1e-6) * g


def _silu(z):
    return z * (1.0 / (1.0 + jnp.exp(-z)))


def _dot(a, b):
    return jnp.dot(a, b, preferred_element_type=F32)


def _dot_nt(a, b):
    return lax.dot_general(a, b, (((1,), (1,)), ((), ())), preferred_element_type=F32)


def _dot_tn(a, b):
    return lax.dot_general(a, b, (((0,), (0,)), ((), ())), preferred_element_type=F32)


def _proj_kernel(x_ref, w_ref, o_ref):
    o_ref[...] = _dot(x_ref[...].astype(BF16), w_ref[...]).astype(o_ref.dtype)


def _proj(x, w, tm, out_dtype=F32):
    m, k = x.shape
    n = w.shape[1]
    tm = min(tm, m)
    return pl.pallas_call(
        _proj_kernel,
        grid=(m // tm,),
        in_specs=[pl.BlockSpec((tm, k), lambda i: (i, 0)), _resident((k, n))],
        out_specs=pl.BlockSpec((tm, n), lambda i: (i, 0)),
        out_shape=jax.ShapeDtypeStruct((m, n), out_dtype),
        compiler_params=_cparams(("parallel",), VMEM_LIMIT),
        name="proj",
    )(x, w)


def _mix_ln_kernel(*refs, n_parts):
    parts = refs[:n_parts]
    w_ref, x_ref, g_ref, b_ref, o_ref = refs[n_parts:]
    a = jnp.concatenate([p[...].astype(BF16) for p in parts], axis=-1) if n_parts > 1 else parts[0][...].astype(BF16)
    z = DN_ALPHA * x_ref[...] + _dot(a, w_ref[...])
    o_ref[...] = _layer_norm(z, g_ref[...], b_ref[...])


def _mix_ln(parts, w, x, g, b, tm):
    m = x.shape[0]
    tm = min(tm, m)
    kern = functools.partial(_mix_ln_kernel, n_parts=len(parts))
    return pl.pallas_call(
        kern,
        grid=(m // tm,),
        in_specs=[pl.BlockSpec((tm, p.shape[1]), lambda i: (i, 0)) for p in parts]
        + [_resident(w.shape), pl.BlockSpec((tm, D_MODEL), lambda i: (i, 0)), _resident(g.shape), _resident(b.shape)],
        out_specs=pl.BlockSpec((tm, D_MODEL), lambda i: (i, 0)),
        out_shape=jax.ShapeDtypeStruct((m, D_MODEL), F32),
        compiler_params=_cparams(("parallel",), VMEM_LIMIT),
        name="mix_ln",
    )(*parts, w, x, g, b)


FF_CHUNK = 1408


def _ffn_kernel(x_ref, wi_ref, wo_ref, g_ref, b_ref, o_ref, act_ref):
    x = x_ref[...]
    xb = x.astype(BF16)
    for c in range(D_FF // FF_CHUNK):
        lo = c * FF_CHUNK
        gate = _dot(xb, wi_ref[:, lo:lo + FF_CHUNK])
        up = _dot(xb, wi_ref[:, D_FF + lo:D_FF + lo + FF_CHUNK])
        act_ref[:, lo:lo + FF_CHUNK] = (_silu(gate) * up).astype(BF16)
    z = DN_ALPHA * x + _dot(act_ref[...], wo_ref[...])
    o_ref[...] = _layer_norm(z, g_ref[...], b_ref[...])


def _ffn(x, wi, wo, g, b, tm):
    m = x.shape[0]
    tm = min(tm, m)
    return pl.pallas_call(
        _ffn_kernel,
        grid=(m // tm,),
        in_specs=[pl.BlockSpec((tm, D_MODEL), lambda i: (i, 0)), _resident(wi.shape), _resident(wo.shape),
                  _resident(g.shape), _resident(b.shape)],
        out_specs=pl.BlockSpec((tm, D_MODEL), lambda i: (i, 0)),
        out_shape=jax.ShapeDtypeStruct((m, D_MODEL), F32),
        scratch_shapes=[pltpu.VMEM((tm, D_FF), BF16)],
        compiler_params=_cparams(("parallel",), VMEM_LIMIT),
        name="ffn",
    )(x, wi, wo, g, b)


def _rope_tables(pos):
    half = MLA_ROPE // 2
    inv = ROPE_BASE ** (-jnp.arange(half, dtype=F32) / half)
    ang = pos.astype(F32)[:, None] * inv[None, :]
    cos, sin = jnp.cos(ang), jnp.sin(ang)
    n = pos.shape[0]
    z = lambda w: jnp.zeros((n, w), F32)
    cc = jnp.concatenate([cos, cos], axis=1)
    ss = jnp.concatenate([-sin, sin], axis=1)
    cq = jnp.concatenate([jnp.ones((n, MLA_NOPE), F32), cc, z(32)], axis=1)
    sq = jnp.concatenate([z(MLA_NOPE), ss, z(32)], axis=1)
    ck = jnp.concatenate([cc, z(96)], axis=1)
    sk = jnp.concatenate([ss, z(96)], axis=1)
    return cq, sq, ck, sk


def _roll_lanes_left(x, k):
    return pltpu.roll(x, x.shape[-1] - k, axis=x.ndim - 1)


def _mla_pre_kernel(h_ref, qn_ref, kvn_ref, wq_ref, wuk_ref, wuvt_ref, cq_ref, sq_ref, ck_ref, sk_ref, *rest):
    rows_ref, q_ref, k_ref, vt_ref = rest[-4:]
    h = h_ref[...]
    qn = _rms_norm(h[:, :MLA_Q_LORA], qn_ref[...]).astype(BF16)
    q = _dot(qn, wq_ref[...])
    cq = jnp.concatenate([cq_ref[...]] * MLA_HEADS, axis=1)
    sq = jnp.concatenate([sq_ref[...]] * MLA_HEADS, axis=1)
    q_ref[...] = (q * cq + _roll_lanes_left(q, 32) * sq).astype(BF16)
    ckv = _rms_norm(h[:, MLA_Q_LORA:MLA_Q_LORA + MLA_KV_LORA], kvn_ref[...])
    blk = h[:, 640:768]
    kpe = blk * ck_ref[...] + _roll_lanes_left(blk, 32) * sk_ref[...]
    rows_ref[:MLA_KV_LORA, :] = ckv.T
    rows_ref[MLA_KV_LORA:, :] = kpe.T[:MLA_ROPE, :]
    ckvb = ckv.astype(BF16)
    kpe_mid = pltpu.roll(kpe, 64, axis=1)
    k_ref[...] = (_dot(ckvb, wuk_ref[...]) + jnp.concatenate([kpe_mid] * MLA_HEADS, axis=1)).astype(BF16)
    vt_ref[0] = _dot_nt(wuvt_ref[...], ckvb).astype(BF16)


MLA_TQ = 512


def _mla_pre(h, qn, kvn, wq, wuk, wuvt, tabs, seq, rows_t, layer, n_layers):
    m = h.shape[0]
    tm = MLA_TQ
    nblk = seq // tm
    tab = pl.BlockSpec((tm, LANES), lambda i: (i % nblk, 0))
    row = lambda w: pl.BlockSpec((tm, w), lambda i: (i, 0))
    hv = MLA_HEADS * MLA_V
    in_specs = [pl.BlockSpec((tm, 768), lambda i: (i, 0)), _resident(qn.shape), _resident(kvn.shape),
                _resident(wq.shape), _resident(wuk.shape), _resident(wuvt.shape), tab, tab, tab, tab]
    args = [h, qn, kvn, wq, wuk, wuvt, *tabs]
    aliases = {}
    if rows_t is not None:
        in_specs.append(pl.BlockSpec(memory_space=pl.ANY))
        args.append(rows_t)
        aliases = {len(args) - 1: 0}
    return pl.pallas_call(
        _mla_pre_kernel,
        grid=(m // tm,),
        in_specs=in_specs,
        out_specs=[pl.BlockSpec((None, None, MLA_KV_DIM, tm), lambda i: (layer, i // nblk, 0, i % nblk)),
                   row(1024), row(1024), pl.BlockSpec((1, hv, tm), lambda i: (i, 0, 0))],
        out_shape=[jax.ShapeDtypeStruct((n_layers, m // seq, MLA_KV_DIM, seq), F32),
                   jax.ShapeDtypeStruct((m, 1024), BF16),
                   jax.ShapeDtypeStruct((m, 1024), BF16), jax.ShapeDtypeStruct((m // tm, hv, tm), BF16)],
        input_output_aliases=aliases,
        compiler_params=_cparams(("arbitrary",), VMEM_LIMIT),
        name="mla_pre",
    )(*args)


def _mla_attn_kernel(q_ref, k_ref, vt_ref, o_ref):
    t = MLA_TQ
    nq = q_ref.shape[0] // t
    krow = lax.broadcasted_iota(jnp.int32, (t, t), 0)
    qcol = lax.broadcasted_iota(jnp.int32, (t, t), 1)
    for qi in range(nq):
        outs = []
        for j in range(2):
            qh = q_ref[qi * t:(qi + 1) * t, j * LANES:(j + 1) * LANES]
            m = l = acc = None
            for kb in range(qi + 1):
                s = _dot_nt(k_ref[kb * t:(kb + 1) * t, j * LANES:(j + 1) * LANES], qh) * MLA_SCALE
                if kb == qi:
                    s = jnp.where(krow <= qcol, s, NEG_INF)
                bm = jnp.max(s, axis=0, keepdims=True)
                m_new = bm if m is None else jnp.maximum(m, bm)
                p = jnp.exp(s - m_new)
                pv = _dot(vt_ref[kb, j * MLA_V:(j + 1) * MLA_V, :], p.astype(BF16))
                ps = jnp.sum(p, axis=0, keepdims=True)
                if m is None:
                    l, acc = ps, pv
                else:
                    alpha = jnp.exp(m - m_new)
                    l, acc = alpha * l + ps, alpha * acc + pv
                m = m_new
            outs.append(acc / l)
        o_ref[qi * t:(qi + 1) * t, :] = jnp.concatenate(outs, axis=0).T.astype(BF16)


def _mla_attn(q, k, vt, batch, seq):
    nq = seq // MLA_TQ
    vt4 = vt.reshape(batch, nq, MLA_HEADS * MLA_V, MLA_TQ)
    return pl.pallas_call(
        _mla_attn_kernel,
        grid=(batch, MLA_HEADS // 2),
        in_specs=[pl.BlockSpec((seq, 256), lambda b, p: (b, p)),
                  pl.BlockSpec((seq, 256), lambda b, p: (b, p)),
                  pl.BlockSpec((None, nq, 2 * MLA_V, MLA_TQ), lambda b, p: (b, 0, p, 0))],
        out_specs=pl.BlockSpec((seq, LANES), lambda b, p: (b, p)),
        out_shape=jax.ShapeDtypeStruct((batch * seq, MLA_HEADS * MLA_V), BF16),
        compiler_params=_cparams(("parallel", "parallel"), VMEM_LIMIT),
        name="mla_attn",
    )(q, k, vt4)


GLA_TB = 256
GLA_SLOTS = 12
HK = GLA_HEADS * GLA_DK
HV = GLA_HEADS * GLA_DV


def _gla_constants():
    c = GLA_CHUNK
    t = np.arange(c)[:, None]
    s = np.arange(c)[None, :]
    slots = []
    zero = np.zeros((c, c), bool)
    for p in (1, 2, 3):
        slots.append((t // 16 == p) & (s < 16 * p))
    slots.append(zero)
    for p in (1, 2, 3):
        slots.append((t // 16 == s // 16) & ((t % 16) // 4 == p) & (s % 16 < 4 * p))
    slots.append(zero)
    for d in (1, 2, 3):
        slots.append((t == s + d) & (t // 4 == s // 4))
    slots.append(t == s)
    mask = np.concatenate(slots, axis=1).astype(np.float32)
    assert (sum(slots) == (t >= s)).all()
    mask = np.tile(mask, (GLA_HEADS, 1))
    tri = (t >= s).astype(np.float32)
    hm = np.kron(np.eye(GLA_HEADS), np.ones((c, GLA_DK))).astype(np.float32)
    bd = np.kron(np.eye(GLA_HEADS), np.ones((GLA_DV, GLA_DK))).astype(np.float32)
    return (jnp.asarray(tri, BF16), jnp.asarray(hm, F32), jnp.asarray(mask, F32), jnp.asarray(bd, F32))


def _log_sigmoid(x):
    return jnp.minimum(x, 0.0) - jnp.log1p(jnp.exp(-jnp.abs(x)))


def _split3(x):
    hi = x.astype(BF16)
    r1 = x - hi.astype(F32)
    mid = r1.astype(BF16)
    lo = (r1 - mid.astype(F32)).astype(BF16)
    return hi, mid, lo


def _bc_rows(row, n):
    return jnp.broadcast_to(row, (n, row.shape[1]))


def _gla_kernel(blk_ref, q_ref, v_ref, r_ref, k_ref, wg_ref, bg_ref, gn_ref, tri_ref, hm_ref, mask_ref, bd_ref,
                o_ref, st_ref, st_scr):
    c = GLA_CHUNK

    @pl.when(pl.program_id(1) == 0)
    def _():
        st_scr[...] = jnp.zeros_like(st_scr)

    rows = lax.broadcasted_iota(jnp.int32, (c, HK), 0)
    rm4 = rows & 3

    def chunk(ci, st):
        sl = slice(ci * c, (ci + 1) * c)
        pre = _dot(blk_ref[sl, :].astype(BF16), wg_ref[...]) + bg_ref[...]
        g = _log_sigmoid(pre) * (1.0 / GLA_TAU)
        cum3 = _dot(tri_ref[...], jnp.concatenate(_split3(g), axis=1))
        cum = cum3[:, :HK] + cum3[:, HK:2 * HK] + cum3[:, 2 * HK:]
        q = q_ref[sl, :] * (GLA_DK ** -0.5)
        k = k_ref[sl, :]
        vb = v_ref[sl, :].astype(BF16)

        def kvar(e):
            return (k * jnp.exp(jnp.minimum(e, 0.0))).astype(BF16)

        zeros_k = jnp.zeros((c, HK), BF16)
        bnd = [cum[16 * p - 1:16 * p, :] for p in (1, 2, 3)]
        base1 = jnp.concatenate([jnp.zeros((16, HK), F32)] + [_bc_rows(b, 16) for b in bnd], axis=0)
        q1 = q * jnp.exp(cum - base1)
        k1 = jnp.concatenate([kvar(_bc_rows(b, c) - cum) for b in bnd] + [zeros_k], axis=0)
        sh = [pltpu.roll(cum, d, axis=0) for d in (1, 2, 3, 4)]
        base2 = jnp.where(rm4 == 0, sh[0], jnp.where(rm4 == 1, sh[1], jnp.where(rm4 == 2, sh[2], sh[3])))
        base2 = jnp.where(rows < 4, 0.0, base2)
        q2 = q * jnp.exp(cum - base2)
        k2 = []
        for p in (1, 2, 3):
            bk = jnp.concatenate([_bc_rows(cum[16 * i + 4 * p - 1:16 * i + 4 * p, :], 16) for i in range(4)], axis=0)
            k2.append(kvar(bk - cum))
        k2 = jnp.concatenate(k2 + [zeros_k], axis=0)
        k3 = jnp.concatenate([kvar(pltpu.roll(cum, c - d, axis=0) - cum) for d in (1, 2, 3)] + [k.astype(BF16)],
                             axis=0)
        hm = hm_ref[...]

        def stack(x):
            return (jnp.concatenate([x] * GLA_HEADS, axis=0) * hm).astype(BF16)

        p_all = jnp.concatenate([_dot_nt(stack(q1), k1), _dot_nt(stack(q2), k2), _dot_nt(stack(q), k3)], axis=1)
        pm = (p_all * mask_ref[...]).astype(BF16)
        o = _dot_nt((q * jnp.exp(cum)).astype(BF16), st.astype(BF16))
        intra = []
        for h in range(GLA_HEADS):
            vh = vb[:, h * GLA_DV:(h + 1) * GLA_DV]
            intra.append(_dot(pm[h * c:(h + 1) * c, :], jnp.concatenate([vh] * GLA_SLOTS, axis=0)))
        o = o + jnp.concatenate(intra, axis=1)
        last = cum[c - 1:c, :]
        st = st * jnp.exp(last) + _dot_tn(vb, kvar(_bc_rows(last, c) - cum)) * bd_ref[...]
        r = r_ref[sl, :]
        outs = []
        for h in range(GLA_HEADS):
            lanes = slice(h * GLA_DV, (h + 1) * GLA_DV)
            outs.append(_rms_norm(o[:, lanes], gn_ref[...]) * _silu(r[:, lanes]))
        o_ref[sl, :] = jnp.concatenate(outs, axis=1).astype(BF16)
        return st

    st = st_scr[...]
    for ci in range(GLA_TB // c):
        st = chunk(ci, st)
    st_scr[...] = st

    @pl.when(pl.program_id(1) == pl.num_programs(1) - 1)
    def _():
        for h in range(GLA_HEADS):
            blk = st_scr[h * GLA_DV:(h + 1) * GLA_DV, :]
            st_ref[0, h] = blk.T[h * GLA_DK:(h + 1) * GLA_DK, :]


def _gla(h, wg, bg, gn, consts, batch, seq):
    nt = seq // GLA_TB
    tri, hm, mask, bd = consts
    rowblk = lambda w, cb: pl.BlockSpec((GLA_TB, w), lambda b, t: (b * nt + t, cb))
    return pl.pallas_call(
        _gla_kernel,
        grid=(batch, nt),
        in_specs=[rowblk(LANES, 5), rowblk(HK, 3), rowblk(HV, 2), rowblk(HV, 3), rowblk(HK, 8),
                  _resident(wg.shape), _resident(bg.shape), _resident(gn.shape),
                  _resident(tri.shape), _resident(hm.shape), _resident(mask.shape), _resident(bd.shape)],
        out_specs=[pl.BlockSpec((GLA_TB, HV), lambda b, t: (b * nt + t, 0)),
                   pl.BlockSpec((1, GLA_HEADS, GLA_DK, GLA_DV), lambda b, t: (b, 0, 0, 0))],
        out_shape=[jax.ShapeDtypeStruct((batch * seq, HV), BF16),
                   jax.ShapeDtypeStruct((batch, GLA_HEADS, GLA_DK, GLA_DV), F32)],
        scratch_shapes=[pltpu.VMEM((HV, HK), F32)],
        compiler_params=_cparams(("parallel", "arbitrary"), VMEM_LIMIT),
        name="gla",
    )(h, h, h, h, h, wg, bg, gn, tri, hm, mask, bd)


def _alibi_slope(group, head):
    return 2.0 ** (-8.0 * (group * DIL_HPG + head + 1) / (DIL_GROUPS * DIL_HPG))


DIL_TM = 256
QKV_G = 3 * MIX_C


def _proj_dil_kernel(x_ref, w_ref, *rest, n_prev, tiles):
    p_refs = rest[n_prev:n_prev + DIL_GROUPS]
    r1_ref, r2_ref, r3_ref, acc_ref = rest[n_prev + DIL_GROUPS:]
    t = pl.program_id(0) % tiles
    res = _dot(x_ref[...].astype(BF16), w_ref[...])

    def put_rows(g, lo, width):
        for which in range(2):
            c0 = g * QKV_G + (1 + which) * MIX_C
            p_refs[g][which] = res[lo:lo + width, c0:c0 + MIX_C].T.reshape(DIL_HPG, DIL_HD, width)

    put_rows(2, 0, DIL_TM)
    pl.when(t >= tiles - DIL_PAIRS[1][0] // DIL_TM)(lambda: put_rows(1, 0, DIL_TM))
    pl.when(t == tiles - 1)(lambda: put_rows(0, DIL_TM - DIL_PAIRS[0][0], DIL_PAIRS[0][0]))
    r1_ref[...] = res[:, :QKV_G].astype(BF16)
    nb = QKV_G // LANES
    for c in range(2 * nb):
        acc_ref[c] = res[:, QKV_G + c * LANES:QKV_G + (c + 1) * LANES]
    for g, r_ref in ((1, r2_ref), (2, r3_ref)):
        dil = DIL_PAIRS[g][1]
        for r in range(dil):
            for c in range(nb):
                r_ref[r, :, c * LANES:(c + 1) * LANES] = (
                    acc_ref[(g - 1) * nb + c, pl.ds(r, DIL_TM // dil, stride=dil), :].astype(BF16))


def _proj_dil(x, w, seq, prev, layer, n_layers):
    m, k = x.shape
    tm = DIL_TM
    nt = m // tm
    tiles = seq // tm
    res = lambda dil: ((m // seq, dil, seq // dil, QKV_G),
                       pl.BlockSpec((None, dil, tm // dil, QKV_G), lambda i: (i // tiles, 0, i % tiles, 0)))
    (s2, b2), (s3, b3) = res(DIL_PAIRS[1][1]), res(DIL_PAIRS[2][1])
    p_shapes, p_specs = [], []
    for window, _ in DIL_PAIRS:
        keep = min(window, seq)
        width = min(keep, tm)
        first = tiles - max(keep // tm, 1)
        p_shapes.append(jax.ShapeDtypeStruct((n_layers, m // seq, 2, DIL_HPG, DIL_HD, keep), F32))
        p_specs.append(pl.BlockSpec(
            (None, None, 2, DIL_HPG, DIL_HD, width),
            lambda i, first=first: (layer, i // tiles, 0, 0, 0, jnp.maximum(i % tiles - first, 0))))
    in_specs = [pl.BlockSpec((tm, k), lambda i: (i, 0)), _resident(w.shape)]
    args = [x, w]
    aliases = {}
    if prev is not None:
        in_specs += [pl.BlockSpec(memory_space=pl.ANY)] * DIL_GROUPS
        aliases = {len(args) + g: g for g in range(DIL_GROUPS)}
        args += list(prev)
    outs = pl.pallas_call(
        functools.partial(_proj_dil_kernel, n_prev=len(aliases), tiles=tiles),
        grid=(nt,),
        in_specs=in_specs,
        out_specs=p_specs + [pl.BlockSpec((tm, QKV_G), lambda i: (i, 0)), b2, b3],
        out_shape=p_shapes + [jax.ShapeDtypeStruct((m, QKV_G), BF16),
                              jax.ShapeDtypeStruct(s2, BF16), jax.ShapeDtypeStruct(s3, BF16)],
        scratch_shapes=[pltpu.VMEM((2 * QKV_G // LANES, tm, LANES), F32)],
        input_output_aliases=aliases,
        compiler_params=_cparams(("arbitrary",), VMEM_LIMIT),
        name="proj_dil",
    )(*args)
    return outs[:DIL_GROUPS], outs[DIL_GROUPS:]


DIL_QB = 4


def _dil_attn_kernel(q_ref, k_ref, v_ref, *rest, group, dil, mode):
    n = DIL_STEPS
    if mode == "chain":
        kp_ref, vp_ref, o_ref, lse_ref = rest
    else:
        o_ref, lse_ref = rest
    nk = 2 * n
    qi = lax.broadcasted_iota(jnp.int32, (n, nk), 0)
    ki = lax.broadcasted_iota(jnp.int32, (n, nk), 1)
    steps = qi + n - ki
    band = (steps >= 0) & (steps <= n)
    dist = (steps * dil).astype(F32)
    lane = lax.broadcasted_iota(jnp.int32, (n, LANES), 1)
    low = lane < DIL_HD
    for blk in range(DIL_QB):
        cur = slice(blk * n, (blk + 1) * n)
        if mode == "single":
            q = q_ref[blk]
            k = jnp.concatenate([k_ref[blk], k_ref[blk]], axis=0)
            v = jnp.concatenate([v_ref[blk], v_ref[blk]], axis=0)
            valid = band & (ki >= n)
        else:
            q = q_ref[cur, :]
            if blk > 0:
                k, v = k_ref[(blk - 1) * n:(blk + 1) * n, :], v_ref[(blk - 1) * n:(blk + 1) * n, :]
                valid = band
            elif mode == "chain":
                k = jnp.concatenate([kp_ref[...], k_ref[cur, :]], axis=0)
                v = jnp.concatenate([vp_ref[...], v_ref[cur, :]], axis=0)
                valid = band & ((ki >= n) | (pl.program_id(1) > 0))
            else:
                k = jnp.concatenate([k_ref[cur, :], k_ref[cur, :]], axis=0)
                v = jnp.concatenate([v_ref[cur, :], v_ref[cur, :]], axis=0)
                valid = band & (ki >= n)
        o_all, lse_all = [], []
        for j in range(DIL_HPG // 2):
            cols = slice(j * LANES, (j + 1) * LANES)
            q2, k2, v2 = q[:, cols], k[:, cols], v[:, cols]
            outs, lses = [], []
            for par in range(2):
                qh = jnp.where(low if par == 0 else ~low, q2, jnp.zeros_like(q2))
                s = _dot_nt(qh, k2) * (DIL_HD ** -0.5) - _alibi_slope(group, 2 * j + par) * dist
                s = jnp.where(valid, s, NEG_INF)
                m = jnp.max(s, axis=-1, keepdims=True)
                p = jnp.exp(s - m)
                l = jnp.sum(p, axis=-1, keepdims=True)
                outs.append(_dot(p.astype(BF16), v2) / l)
                lses.append(jnp.broadcast_to(m + jnp.log(l), (n, LANES)))
            o_all.append(jnp.where(low, outs[0], outs[1]))
            lse_all.append(jnp.where(low, lses[0], lses[1]))
        dst = (blk,) if mode == "single" else (cur, slice(None))
        o_ref[dst] = jnp.concatenate(o_all, axis=1)
        lse_ref[dst] = jnp.concatenate(lse_all, axis=1)


def _dil_attn(qkv_r, group):
    n = DIL_STEPS
    batch, dil, ls, _ = qkv_r.shape
    rows = DIL_QB * n
    if ls == n:
        mode, steps = "single", dil // DIL_QB
        spec = lambda c: pl.BlockSpec((None, DIL_QB, n, MIX_C), lambda b, s: (b, s, 0, c))
        extra = []
    elif ls == rows:
        mode, steps = "whole", dil
        spec = lambda c: pl.BlockSpec((None, None, rows, MIX_C), lambda b, s: (b, s, 0, c))
        extra = []
    else:
        assert dil == 1 and ls % rows == 0
        mode, steps = "chain", ls // rows
        spec = lambda c: pl.BlockSpec((None, None, rows, MIX_C), lambda b, s: (b, 0, s, c))
        extra = [pl.BlockSpec((None, None, n, MIX_C), lambda b, s, c=c: (b, 0, jnp.maximum(s * DIL_QB - 1, 0), c))
                 for c in (1, 2)]
    in_specs = [spec(0), spec(1), spec(2)] + extra
    return pl.pallas_call(
        functools.partial(_dil_attn_kernel, group=group, dil=DIL_PAIRS[group][1], mode=mode),
        grid=(batch, steps),
        in_specs=in_specs,
        out_specs=[spec(0), spec(0)],
        out_shape=[jax.ShapeDtypeStruct((batch, dil, ls, MIX_C), F32)] * 2,
        compiler_params=_cparams(("parallel", "arbitrary"), VMEM_LIMIT),
        name="dil_attn%d" % group,
    )(*([qkv_r] * len(in_specs)))


def _dil_mix_ln_kernel(o1, o2, o3, l1, l2, l3, w_ref, x_ref, g_ref, b_ref, o_ref, *scr):
    nb = MIX_C // LANES
    for src, dst, dil in ((o2, scr[0], DIL_PAIRS[1][1]), (l2, scr[1], DIL_PAIRS[1][1]),
                          (o3, scr[2], DIL_PAIRS[2][1]), (l3, scr[3], DIL_PAIRS[2][1])):
        for r in range(dil):
            for c in range(nb):
                dst[c, pl.ds(r, DIL_TM // dil, stride=dil), :] = src[r, :, c * LANES:(c + 1) * LANES]
    whole = lambda s: jnp.concatenate([s[c] for c in range(nb)], axis=1)
    os_ = [o1[...], whole(scr[0]), whole(scr[2])]
    ls = [l1[...], whole(scr[1]), whole(scr[3])]
    mx = jnp.maximum(jnp.maximum(ls[0], ls[1]), ls[2])
    es = [jnp.exp(l - mx) for l in ls]
    mix = (es[0] * os_[0] + es[1] * os_[1] + es[2] * os_[2]) / (es[0] + es[1] + es[2])
    z = DN_ALPHA * x_ref[...] + _dot(mix.astype(BF16), w_ref[...])
    o_ref[...] = _layer_norm(z, g_ref[...], b_ref[...])


def _dil_mix_ln(os_, lses, w, x, g, b):
    m = x.shape[0]
    tm = DIL_TM
    nt = m // tm
    tiles = nt // os_[0].shape[0]
    row = lambda wd: pl.BlockSpec((tm, wd), lambda i: (i, 0))
    res = lambda a: pl.BlockSpec((None, a.shape[1], tm // a.shape[1], MIX_C), lambda i: (i // tiles, 0, i % tiles, 0))
    flat = lambda a: a.reshape(m, MIX_C)
    return pl.pallas_call(
        _dil_mix_ln_kernel,
        grid=(nt,),
        in_specs=[row(MIX_C), res(os_[1]), res(os_[2]), row(MIX_C), res(lses[1]), res(lses[2]),
                  _resident(w.shape), row(D_MODEL), _resident(g.shape), _resident(b.shape)],
        out_specs=row(D_MODEL),
        out_shape=jax.ShapeDtypeStruct((m, D_MODEL), F32),
        scratch_shapes=[pltpu.VMEM((MIX_C // LANES, tm, LANES), F32)] * 4,
        compiler_params=_cparams(("parallel",), VMEM_LIMIT),
        name="dil_mix_ln",
    )(flat(os_[0]), os_[1], os_[2], flat(lses[0]), lses[1], lses[2], w, x, g, b)


def _mla_pre_s_kernel(h_ref, qn_ref, kvn_ref, wqn_ref, wqp_ref, wukt_ref, ck_ref, sk_ref, rows_ref, q_ref):
    h = h_ref[...]
    qn = _rms_norm(h[:, :MLA_Q_LORA], qn_ref[...]).astype(BF16)
    qnope = _dot(qn, wqn_ref[...]).astype(BF16)
    qp = _dot(qn, wqp_ref[...])
    cq = jnp.concatenate([ck_ref[...]] * MLA_HEADS, axis=1)
    sq = jnp.concatenate([sk_ref[...]] * MLA_HEADS, axis=1)
    qp = qp * cq + _roll_lanes_left(qp, 32) * sq
    ckv = _rms_norm(h[:, MLA_Q_LORA:MLA_Q_LORA + MLA_KV_LORA], kvn_ref[...])
    blk = h[:, 640:768]
    kpe = blk * ck_ref[...] + _roll_lanes_left(blk, 32) * sk_ref[...]
    rows_ref[:, :MLA_KV_LORA] = ckv
    rows_ref[:, MLA_KV_LORA:] = kpe[:, :MLA_ROPE]
    for hd in range(MLA_HEADS):
        cols = slice(hd * LANES, (hd + 1) * LANES)
        q_ref[hd, :, :MLA_KV_LORA] = _dot(qnope[:, cols], wukt_ref[hd])
        q_ref[hd, :, MLA_KV_LORA:] = qp[:, hd * LANES:hd * LANES + MLA_ROPE]


def _mla_pre_s(h, qn, kvn, wqn, wqp, wukt, tabs):
    m = h.shape[0]
    full = lambda a: _resident(a.shape)
    return pl.pallas_call(
        _mla_pre_s_kernel,
        grid=(1,),
        in_specs=[pl.BlockSpec((m, 768), lambda i: (0, 0)), full(qn), full(kvn), full(wqn), full(wqp), full(wukt)]
        + [full(t) for t in tabs[2:]],
        out_specs=[pl.BlockSpec((m, MLA_KV_DIM), lambda i: (0, 0)),
                   pl.BlockSpec((MLA_HEADS, m, MLA_KV_DIM), lambda i: (0, 0, 0))],
        out_shape=[jax.ShapeDtypeStruct((m, MLA_KV_DIM), F32), jax.ShapeDtypeStruct((MLA_HEADS, m, MLA_KV_DIM), F32)],
        compiler_params=_cparams(("arbitrary",), VMEM_LIMIT),
        name="mla_pre_s",
    )(h, qn, kvn, wqn, wqp, wukt, *tabs[2:])


DEC_PAGES = 32


def _mla_dec_kernel(pt_ref, q_ref, new_ref, *rest):
    pages = rest[:DEC_PAGES]
    o_ref, m_scr, l_scr, acc_scr = rest[DEC_PAGES:]
    step = pl.program_id(1)
    q = q_ref[0]
    qb = q.astype(BF16)

    @pl.when(step == 0)
    def _():
        new = new_ref[0]
        m_scr[...] = jnp.sum(q * new, axis=-1, keepdims=True) * MLA_SCALE
        l_scr[...] = jnp.ones_like(l_scr)
        acc_scr[...] = jnp.broadcast_to(new[:, :MLA_KV_LORA], acc_scr.shape)

    kvt = jnp.concatenate([p[...].astype(BF16) for p in pages], axis=1)
    s = _dot(qb, kvt) * MLA_SCALE
    m_old = m_scr[...]
    m_new = jnp.maximum(m_old, jnp.max(s, axis=-1, keepdims=True))
    alpha = jnp.exp(m_old - m_new)
    p = jnp.exp(s - m_new)
    m_scr[...] = m_new
    l_scr[...] = alpha * l_scr[...] + jnp.sum(p, axis=-1, keepdims=True)
    acc_scr[...] = alpha * acc_scr[...] + _dot_nt(p.astype(BF16), kvt[:MLA_KV_LORA, :])

    @pl.when(step == pl.num_programs(1) - 1)
    def _():
        o_ref[0] = acc_scr[...] / l_scr[...]


def _mla_dec(page_table, q_abs, rows_new, cache_t, layer):
    batch = q_abs.shape[0]
    n_pages = page_table.shape[1]
    steps = n_pages // DEC_PAGES
    pt = page_table.reshape(-1)

    def page_spec(j):
        return pl.BlockSpec((None, None, MLA_KV_DIM, PAGE_SIZE),
                            lambda b, s, pt_ref: (layer, pt_ref[b * n_pages + s * DEC_PAGES + j], 0, 0))

    grid_spec = pltpu.PrefetchScalarGridSpec(
        num_scalar_prefetch=1,
        grid=(batch, steps),
        in_specs=[pl.BlockSpec((1, MLA_HEADS, MLA_KV_DIM), lambda b, s, pt_ref: (b, 0, 0)),
                  pl.BlockSpec((1, 1, MLA_KV_DIM), lambda b, s, pt_ref: (b, 0, 0))]
        + [page_spec(j) for j in range(DEC_PAGES)],
        out_specs=pl.BlockSpec((1, MLA_HEADS, MLA_KV_LORA), lambda b, s, pt_ref: (b, 0, 0)),
        scratch_shapes=[pltpu.VMEM((MLA_HEADS, 1), F32), pltpu.VMEM((MLA_HEADS, 1), F32),
                        pltpu.VMEM((MLA_HEADS, MLA_KV_LORA), F32)],
    )
    return pl.pallas_call(
        _mla_dec_kernel,
        grid_spec=grid_spec,
        out_shape=jax.ShapeDtypeStruct((batch, MLA_HEADS, MLA_KV_LORA), F32),
        compiler_params=_cparams(("parallel", "arbitrary"), VMEM_LIMIT),
        name="mla_dec",
    )(pt, q_abs, rows_new.reshape(batch, 1, MLA_KV_DIM), *([cache_t] * DEC_PAGES))


def _gla_dec_kernel(blk_ref, q_ref, v_ref, r_ref, k_ref, s_ref, wg_ref, bg_ref, gn_ref, o_ref, so_ref):
    pre = _dot(blk_ref[0].astype(BF16), wg_ref[...]) + bg_ref[...]
    a = jnp.exp(_log_sigmoid(pre) * (1.0 / GLA_TAU))
    q = q_ref[0] * (GLA_DK ** -0.5)
    k = k_ref[0]
    v = v_ref[0]
    r = r_ref[0]
    eye = (lax.broadcasted_iota(jnp.int32, (GLA_DK, GLA_DK), 0)
           == lax.broadcasted_iota(jnp.int32, (GLA_DK, GLA_DK), 1))
    outs = []
    for h in range(GLA_HEADS):
        kl = slice(h * GLA_DK, (h + 1) * GLA_DK)
        vl = slice(h * GLA_DV, (h + 1) * GLA_DV)
        diag = lambda row: jnp.where(eye, jnp.broadcast_to(row, (GLA_DK, GLA_DK)), 0.0)
        lhs = jnp.concatenate([diag(a[:, kl]), diag(k[:, kl])], axis=1)
        rhs = jnp.concatenate([s_ref[0, h], jnp.broadcast_to(v[:, vl], (GLA_DK, GLA_DV))], axis=0)
        s_new = jnp.dot(lhs, rhs, preferred_element_type=F32, precision=lax.Precision.HIGHEST)
        so_ref[0, h] = s_new
        o = jnp.dot(jnp.broadcast_to(q[:, kl], (8, GLA_DK)), s_new, preferred_element_type=F32,
                    precision=lax.Precision.HIGHEST)[:1]
        outs.append(_rms_norm(o, gn_ref[...]) * _silu(r[:, vl]))
    o_ref[0] = jnp.concatenate(outs, axis=1)


def _gla_dec(h, state, layer, wg, bg, gn):
    batch = h.shape[0]
    x = h.reshape(batch, 1, IN_A_PAD)
    rowblk = lambda w, cb: pl.BlockSpec((1, 1, w), lambda b: (b, 0, cb))
    st_in = pl.BlockSpec((None, 1, GLA_HEADS, GLA_DK, GLA_DV), lambda b: (layer, b, 0, 0, 0))
    st = pl.BlockSpec((1, GLA_HEADS, GLA_DK, GLA_DV), lambda b: (b, 0, 0, 0))
    o, s_new = pl.pallas_call(
        _gla_dec_kernel,
        grid=(batch,),
        in_specs=[rowblk(LANES, 5), rowblk(HK, 3), rowblk(HV, 2), rowblk(HV, 3), rowblk(HK, 8), st_in,
                  _resident(wg.shape), _resident(bg.shape), _resident(gn.shape)],
        out_specs=[pl.BlockSpec((1, 1, HV), lambda b: (b, 0, 0)), st],
        out_shape=[jax.ShapeDtypeStruct((batch, 1, HV), F32), jax.ShapeDtypeStruct(state.shape[1:], F32)],
        compiler_params=_cparams(("parallel",), VMEM_LIMIT),
        name="gla_dec",
    )(x, x, x, x, x, state, wg, bg, gn)
    return o.reshape(batch, HV), s_new


def _dil_dec_kernel(x_ref, c1_ref, c2_ref, c3_ref, o_ref):
    heads = range(DIL_HPG)
    hcol = lax.broadcasted_iota(jnp.int32, (DIL_HPG, 1), 0).astype(F32)
    outs, lses = [], []
    for g, c_ref in enumerate((c1_ref, c2_ref, c3_ref)):
        window, dil = DIL_PAIRS[g]
        pos = lax.broadcasted_iota(jnp.int32, (1, window), 1)
        q, k_new, v_new = (x_ref[0, 3 * g + w] for w in range(3))
        s = jnp.concatenate([jnp.sum(c_ref[0, h] * q[:, h:h + 1], axis=0, keepdims=True) for h in heads], axis=0)
        slope = jnp.exp((g * DIL_HPG + hcol + 1.0) * (-8.0 * math.log(2.0) / (DIL_GROUPS * DIL_HPG)))
        s = s * (DIL_HD ** -0.5) - slope * (window - pos).astype(F32)
        s = jnp.where((pos & (dil - 1)) == 0, s, NEG_INF)
        qk = jnp.sum(q * k_new, axis=0, keepdims=True) * (DIL_HD ** -0.5)
        s_new = jnp.concatenate([qk[:, h:h + 1] for h in heads], axis=0)
        m = jnp.maximum(jnp.max(s, axis=1, keepdims=True), s_new)
        p = jnp.exp(s - m)
        p_new = jnp.exp(s_new - m)
        l = jnp.sum(p, axis=1, keepdims=True) + p_new
        w = p / l
        w_new = p_new / l
        outs.append([jnp.sum(c_ref[1, h] * w[h:h + 1, :], axis=1, keepdims=True)
                     + w_new[h:h + 1, :] * v_new[:, h:h + 1] for h in heads])
        lses.append(m + jnp.log(l))
    mx = jnp.maximum(jnp.maximum(lses[0], lses[1]), lses[2])
    es = [jnp.exp(l - mx) for l in lses]
    den = es[0] + es[1] + es[2]
    ws = [e / den for e in es]
    o_ref[0] = jnp.concatenate(
        [sum(ws[g][h:h + 1, :] * outs[g][h] for g in range(DIL_GROUPS)) for h in heads], axis=1)


def _dil_dec(qkv_t, caches_t, layer):
    batch = qkv_t.shape[0]
    specs = [pl.BlockSpec((1, 3 * DIL_GROUPS, DIL_HD, DIL_HPG), lambda b: (b, 0, 0, 0))]
    for (window, _), c in zip(DIL_PAIRS, caches_t):
        assert c.shape[-1] == window
        specs.append(pl.BlockSpec((None, None, 2, DIL_HPG, DIL_HD, window), lambda b: (layer, b, 0, 0, 0, 0)))
    return pl.pallas_call(
        _dil_dec_kernel,
        grid=(batch,),
        in_specs=specs,
        out_specs=pl.BlockSpec((1, DIL_HD, DIL_HPG), lambda b: (b, 0, 0)),
        out_shape=jax.ShapeDtypeStruct((batch, DIL_HD, DIL_HPG), F32),
        compiler_params=_cparams(("parallel",), VMEM_LIMIT),
        name="dil_dec",
    )(qkv_t, *caches_t)


def _even_in_cols():
    o_cq, o_kv, o_gq, o_gk, o_gv, o_lr, o_gr = 0, 384, 672, 928, 1184, 1696, 1712
    r = np.arange
    kpe = o_kv + MLA_KV_LORA
    idx = np.concatenate([r(o_cq, o_cq + 384), r(o_kv, o_kv + 256), r(kpe, kpe + 32), r(kpe + 16, kpe + 32),
                          r(kpe, kpe + 16), r(o_lr, o_lr + 16), np.full(48, -1), r(o_gq, o_gq + 256),
                          r(o_gv, o_gv + 512), r(o_gr, o_gr + 512), r(o_gk, o_gk + 256)])
    assert idx.shape[0] == IN_A_PAD
    return idx


def _gather_cols(w, idx):
    return jnp.where(jnp.asarray(idx >= 0)[None, :], jnp.take(w, jnp.asarray(np.maximum(idx, 0)), axis=1), 0.0)


def _uq_cols(kind):
    idx = []
    for h in range(MLA_HEADS):
        b = h * (MLA_NOPE + MLA_ROPE)
        nope = np.arange(b, b + 64)
        x1, x2 = np.arange(b + 64, b + 80), np.arange(b + 80, b + 96)
        if kind == "full":
            idx += [nope, x1, x2, x2, x1]
        elif kind == "nope":
            idx += [nope, np.full(64, -1)]
        else:
            idx += [x1, x2, x2, x1, np.full(64, -1)]
    return np.concatenate(idx)


def _odd_in_cols():
    blk = lambda which, g: np.arange((which * DIL_GROUPS + g) * MIX_C, (which * DIL_GROUPS + g + 1) * MIX_C)
    return np.concatenate([blk(which, g) for g in range(DIL_GROUPS) for which in range(3)])


TM_DENSE = 512
TM_WIDE = 256


def kernel(x_prompt, x_sample, cache_mla, state_gla, cache_dil_w128, cache_dil_w512, cache_dil_w2048, page_table,
           w_in_a, mla_q_norm, mla_w_uq, mla_kv_norm, mla_w_uk, mla_w_uv, gla_w_gate2, gla_b_gate, gla_norm, w_out_a,
           w_in_c, w_out_c, ffn_w_in, ffn_w_out, ln_g, ln_b):
    batch, seq, _ = x_prompt.shape
    dbatch = x_sample.shape[0]
    assert seq % MLA_TQ == 0 and seq % DIL_TM == 0 and seq >= DIL_PAIRS[-1][0]
    n_even, n_odd = (DEPTH + 1) // 2, DEPTH // 2
    cache_mla_t = jnp.transpose(cache_mla, (0, 1, 3, 2))
    dil_caches_t = [jnp.transpose(c, (0, 1, 3, 4, 5, 2)) for c in (cache_dil_w128, cache_dil_w512, cache_dil_w2048)]
    gla_consts = _gla_constants()
    tabs_p = _rope_tables(jnp.arange(seq, dtype=jnp.int32))
    tabs_s = _rope_tables(jnp.full((dbatch,), PAST_LEN, jnp.int32))
    even_cols, odd_cols = _even_in_cols(), _odd_in_cols()
    row2 = lambda v: v.reshape(1, -1).astype(F32)

    xp = x_prompt.reshape(batch * seq, D_MODEL)
    xs = x_sample.reshape(dbatch, D_MODEL)
    mla_s, gla_p, gla_s = [], [], []
    rows_t, dil_t = None, None
    dil_s = [[] for _ in DIL_PAIRS]

    for layer in range(DEPTH):
        i = layer // 2
        g0, b0, g1, b1 = (row2(ln_g[layer, 0]), row2(ln_b[layer, 0]), row2(ln_g[layer, 1]), row2(ln_b[layer, 1]))
        if layer % 2 == 0:
            w_in = _gather_cols(w_in_a[i], even_cols).astype(BF16)
            wq_full = _gather_cols(mla_w_uq[i], _uq_cols("full")).astype(BF16)
            wq_nope = _gather_cols(mla_w_uq[i], _uq_cols("nope")).astype(BF16)
            wq_rope = _gather_cols(mla_w_uq[i], _uq_cols("rope")).astype(BF16)
            wuk_pad = jnp.pad(mla_w_uk[i], ((0, 0), (0, 0), (0, LANES - MLA_NOPE))).reshape(MLA_KV_LORA, -1).astype(BF16)
            wuk_t = jnp.pad(jnp.transpose(mla_w_uk[i], (1, 2, 0)), ((0, 0), (0, LANES - MLA_NOPE), (0, 0))).astype(BF16)
            wuv_t = mla_w_uv[i].reshape(MLA_KV_LORA, -1).T.astype(BF16)
            wuv_bd = (jnp.eye(MLA_HEADS, dtype=F32)[:, None, :, None]
                      * jnp.transpose(mla_w_uv[i], (1, 0, 2))[:, :, None, :]).reshape(
                          MLA_HEADS * MLA_KV_LORA, MLA_HEADS * MLA_V).astype(BF16)
            wg = jnp.pad(gla_w_gate2[i], ((64, LANES - 64 - GLA_GATE_RANK), (0, 0))).astype(BF16)
            bg, gn = row2(gla_b_gate[i]), row2(gla_norm[i])
            qn, kvn = row2(mla_q_norm[i]), row2(mla_kv_norm[i])
            w_out = w_out_a[i].astype(BF16)

            h = _proj(xp, w_in, TM_DENSE)
            rows_t, qf, kf, vt = _mla_pre(h, qn, kvn, wq_full, wuk_pad, wuv_t, tabs_p, seq, rows_t, i, n_even)
            mla_out = _mla_attn(qf, kf, vt, batch, seq)
            gla_out, s_fin = _gla(h, wg, bg, gn, gla_consts, batch, seq)
            gla_p.append(s_fin)
            xp = _mix_ln([mla_out, gla_out], w_out, xp, g0, b0, TM_DENSE)

            h = _proj(xs, w_in, TM_DENSE)
            rows, q_abs = _mla_pre_s(h, qn, kvn, wq_nope, wq_rope, wuk_t, tabs_s)
            lat = _mla_dec(page_table, jnp.transpose(q_abs, (1, 0, 2)), rows, cache_mla_t, i)
            mla_out = _proj(lat.reshape(dbatch, MLA_HEADS * MLA_KV_LORA), wuv_bd, TM_DENSE)
            gla_out, s_fin = _gla_dec(h, state_gla, i, wg, bg, gn)
            mla_s.append(rows.reshape(dbatch, 1, MLA_KV_DIM))
            gla_s.append(s_fin)
            xs = _mix_ln([mla_out, gla_out], w_out, xs, g0, b0, TM_DENSE)
        else:
            w_in = jnp.take(w_in_c[i], jnp.asarray(odd_cols), axis=1).astype(BF16)
            w_out = w_out_c[i].astype(BF16)

            dil_t, (r1, r2, r3) = _proj_dil(xp, w_in, seq, dil_t, i, n_odd)
            res = [r1.reshape(batch, 1, seq, QKV_G), r2, r3]
            os_, lses = zip(*[_dil_attn(res[g], g) for g in range(DIL_GROUPS)])
            xp = _dil_mix_ln(os_, lses, w_out, xp, g0, b0)

            qkv = _proj(xs, w_in, TM_WIDE).reshape(dbatch, 3 * DIL_GROUPS, DIL_HPG, DIL_HD)
            mix = _dil_dec(jnp.transpose(qkv, (0, 1, 3, 2)), dil_caches_t, i)
            mix = jnp.transpose(mix, (0, 2, 1)).reshape(dbatch, MIX_C)
            for g in range(DIL_GROUPS):
                dil_s[g].append(qkv[:, 3 * g + 1:3 * g + 3].reshape(dbatch, 1, 2, DIL_HPG, DIL_HD))
            xs = _mix_ln([mix], w_out, xs, g0, b0, TM_DENSE)

        wi, wo = ffn_w_in[layer].astype(BF16), ffn_w_out[layer].astype(BF16)
        xp = _ffn(xp, wi, wo, g1, b1, TM_DENSE)
        xs = _ffn(xs, wi, wo, g1, b1, TM_DENSE)

    st = jnp.stack
    mla_p = jnp.transpose(rows_t, (0, 1, 3, 2))
    dil_p = [jnp.transpose(p, (0, 1, 5, 2, 3, 4)) for p in dil_t]
    return (xp.reshape(batch, seq, D_MODEL), xs.reshape(dbatch, 1, D_MODEL), mla_p, st(mla_s), st(gla_p), st(gla_s),
            dil_p[0], st(dil_s[0]), dil_p[1], st(dil_s[1]), dil_p[2], st(dil_s[2]))
```

```python
import functools
import math

import numpy as np
import jax
import jax.numpy as jnp
from jax import lax
from jax.experimental import pallas as pl
from jax.experimental.pallas import tpu as pltpu

F32 = jnp.float32
BF16 = jnp.bfloat16

D_MODEL = 1024
DEPTH = 4
PAST_LEN = 16384
PAGE_SIZE = 128
MLA_HEADS = 8
MLA_Q_LORA = 384
MLA_KV_LORA = 256
MLA_NOPE = 64
MLA_ROPE = 32
MLA_V = 64
MLA_KV_DIM = MLA_KV_LORA + MLA_ROPE
ROPE_BASE = 10000.0
GLA_HEADS = 4
GLA_DK = 64
GLA_DV = 128
GLA_GATE_RANK = 16
GLA_TAU = 16.0
GLA_CHUNK = 64
DIL_PAIRS = ((128, 1), (512, 4), (2048, 16))
DIL_GROUPS = 3
DIL_HPG = 8
DIL_HD = 64
DIL_STEPS = 128
D_FF = -(-8 * D_MODEL // (3 * 256)) * 256
DN_ALPHA = (2.0 * DEPTH) ** 0.25
IN_A_PAD = 2304
MIX_C = DIL_HPG * DIL_HD
IN_C = 3 * DIL_GROUPS * MIX_C
MLA_SCALE = (MLA_NOPE + MLA_ROPE) ** -0.5

V7X_VMEM_BYTES = 64 * 1024 * 1024
VMEM_LIMIT = 56 * 1024 * 1024
LANES = 128

NEG_INF = float("-inf")


def _cparams(sem, limit=None):
    return pltpu.CompilerParams(dimension_semantics=sem, vmem_limit_bytes=limit)


def _resident(shape):
    nd = len(shape)
    return pl.BlockSpec(shape, lambda *_: (0,) * nd, pipeline_mode=pl.Buffered(1))


def _layer_norm(z, g, b):
    mu = jnp.mean(z, axis=-1, keepdims=True)
    zc = z - mu
    var = jnp.mean(zc * zc, axis=-1, keepdims=True)
    return zc * lax.rsqrt(var + 1e-5) * g + b


def _rms_norm(z, g):
    return z * lax.rsqrt(jnp.mean(z * z, axis=-1, keepdims=True) + 1e-6) * g


def _silu(z):
    return z * (1.0 / (1.0 + jnp.exp(-z)))


def _dot(a, b):
    return jnp.dot(a, b, preferred_element_type=F32)


def _dot_nt(a, b):
    return lax.dot_general(a, b, (((1,), (1,)), ((), ())), preferred_element_type=F32)


def _dot_tn(a, b):
    return lax.dot_general(a, b, (((0,), (0,)), ((), ())), preferred_element_type=F32)


def _proj_kernel(x_ref, w_ref, o_ref):
    o_ref[...] = _dot(x_ref[...].astype(BF16), w_ref[...]).astype(o_ref.dtype)


def _proj(x, w, tm, out_dtype=F32):
    m, k = x.shape
    n = w.shape[1]
    tm = min(tm, m)
    return pl.pallas_call(
        _proj_kernel,
        grid=(m // tm,),
        in_specs=[pl.BlockSpec((tm, k), lambda i: (i, 0)), _resident((k, n))],
        out_specs=pl.BlockSpec((tm, n), lambda i: (i, 0)),
        out_shape=jax.ShapeDtypeStruct((m, n), out_dtype),
        compiler_params=_cparams(("parallel",), VMEM_LIMIT),
        name="proj",
    )(x, w)


def _mix_ln_kernel(*refs, n_parts):
    parts = refs[:n_parts]
    w_ref, x_ref, g_ref, b_ref, o_ref = refs[n_parts:]
    a = jnp.concatenate([p[...].astype(BF16) for p in parts], axis=-1) if n_parts > 1 else parts[0][...].astype(BF16)
    z = DN_ALPHA * x_ref[...] + _dot(a, w_ref[...])
    o_ref[...] = _layer_norm(z, g_ref[...], b_ref[...])


def _mix_ln(parts, w, x, g, b, tm):
    m = x.shape[0]
    tm = min(tm, m)
    kern = functools.partial(_mix_ln_kernel, n_parts=len(parts))
    return pl.pallas_call(
        kern,
        grid=(m // tm,),
        in_specs=[pl.BlockSpec((tm, p.shape[1]), lambda i: (i, 0)) for p in parts]
        + [_resident(w.shape), pl.BlockSpec((tm, D_MODEL), lambda i: (i, 0)), _resident(g.shape), _resident(b.shape)],
        out_specs=pl.BlockSpec((tm, D_MODEL), lambda i: (i, 0)),
        out_shape=jax.ShapeDtypeStruct((m, D_MODEL), F32),
        compiler_params=_cparams(("parallel",), VMEM_LIMIT),
        name="mix_ln",
    )(*parts, w, x, g, b)


FF_CHUNK = 1408


def _ffn_kernel(x_ref, wi_ref, wo_ref, g_ref, b_ref, o_ref, act_ref):
    x = x_ref[...]
    xb = x.astype(BF16)
    for c in range(D_FF // FF_CHUNK):
        lo = c * FF_CHUNK
        gate = _dot(xb, wi_ref[:, lo:lo + FF_CHUNK])
        up = _dot(xb, wi_ref[:, D_FF + lo:D_FF + lo + FF_CHUNK])
        act_ref[:, lo:lo + FF_CHUNK] = (_silu(gate) * up).astype(BF16)
    z = DN_ALPHA * x + _dot(act_ref[...], wo_ref[...])
    o_ref[...] = _layer_norm(z, g_ref[...], b_ref[...])


def _ffn(x, wi, wo, g, b, tm):
    m = x.shape[0]
    tm = min(tm, m)
    return pl.pallas_call(
        _ffn_kernel,
        grid=(m // tm,),
        in_specs=[pl.BlockSpec((tm, D_MODEL), lambda i: (i, 0)), _resident(wi.shape), _resident(wo.shape),
                  _resident(g.shape), _resident(b.shape)],
        out_specs=pl.BlockSpec((tm, D_MODEL), lambda i: (i, 0)),
        out_shape=jax.ShapeDtypeStruct((m, D_MODEL), F32),
        scratch_shapes=[pltpu.VMEM((tm, D_FF), BF16)],
        compiler_params=_cparams(("parallel",), VMEM_LIMIT),
        name="ffn",
    )(x, wi, wo, g, b)


def _rope_tables(pos):
    half = MLA_ROPE // 2
    inv = ROPE_BASE ** (-jnp.arange(half, dtype=F32) / half)
    ang = pos.astype(F32)[:, None] * inv[None, :]
    cos, sin = jnp.cos(ang), jnp.sin(ang)
    n = pos.shape[0]
    z = lambda w: jnp.zeros((n, w), F32)
    cc = jnp.concatenate([cos, cos], axis=1)
    ss = jnp.concatenate([-sin, sin], axis=1)
    cq = jnp.concatenate([jnp.ones((n, MLA_NOPE), F32), cc, z(32)], axis=1)
    sq = jnp.concatenate([z(MLA_NOPE), ss, z(32)], axis=1)
    ck = jnp.concatenate([cc, z(96)], axis=1)
    sk = jnp.concatenate([ss, z(96)], axis=1)
    return cq, sq, ck, sk


def _roll_lanes_left(x, k):
    return pltpu.roll(x, x.shape[-1] - k, axis=x.ndim - 1)


def _mla_pre_kernel(h_ref, qn_ref, kvn_ref, wq_ref, wuk_ref, wuvt_ref, cq_ref, sq_ref, ck_ref, sk_ref, *rest):
    rows_ref, q_ref, k_ref, vt_ref = rest[-4:]
    h = h_ref[...]
    qn = _rms_norm(h[:, :MLA_Q_LORA], qn_ref[...]).astype(BF16)
    q = _dot(qn, wq_ref[...])
    cq = jnp.concatenate([cq_ref[...]] * MLA_HEADS, axis=1)
    sq = jnp.concatenate([sq_ref[...]] * MLA_HEADS, axis=1)
    q_ref[...] = ((q * cq + _roll_lanes_left(q, 32) * sq) * MLA_SCALE).astype(BF16)
    ckv = _rms_norm(h[:, MLA_Q_LORA:MLA_Q_LORA + MLA_KV_LORA], kvn_ref[...])
    blk = h[:, 640:768]
    kpe = blk * ck_ref[...] + _roll_lanes_left(blk, 32) * sk_ref[...]
    rows_ref[:MLA_KV_LORA, :] = ckv.T
    rows_ref[MLA_KV_LORA:, :] = kpe.T[:MLA_ROPE, :]
    ckvb = ckv.astype(BF16)
    kpe_mid = pltpu.roll(kpe, 64, axis=1)
    k_ref[...] = (_dot(ckvb, wuk_ref[...]) + jnp.concatenate([kpe_mid] * MLA_HEADS, axis=1)).astype(BF16)
    vt_ref[0] = _dot_nt(wuvt_ref[...], ckvb).astype(BF16)


MLA_TQ = 512


def _mla_pre(h, qn, kvn, wq, wuk, wuvt, tabs, seq, rows_t, layer, n_layers):
    m = h.shape[0]
    tm = MLA_TQ
    nblk = seq // tm
    tab = pl.BlockSpec((tm, LANES), lambda i: (i % nblk, 0))
    row = lambda w: pl.BlockSpec((tm, w), lambda i: (i, 0))
    hv = MLA_HEADS * MLA_V
    in_specs = [pl.BlockSpec((tm, 768), lambda i: (i, 0)), _resident(qn.shape), _resident(kvn.shape),
                _resident(wq.shape), _resident(wuk.shape), _resident(wuvt.shape), tab, tab, tab, tab]
    args = [h, qn, kvn, wq, wuk, wuvt, *tabs]
    aliases = {}
    if rows_t is not None:
        in_specs.append(pl.BlockSpec(memory_space=pl.ANY))
        args.append(rows_t)
        aliases = {len(args) - 1: 0}
    return pl.pallas_call(
        _mla_pre_kernel,
        grid=(m // tm,),
        in_specs=in_specs,
        out_specs=[pl.BlockSpec((None, None, MLA_KV_DIM, tm), lambda i: (layer, i // nblk, 0, i % nblk)),
                   row(1024), row(1024), pl.BlockSpec((1, hv, tm), lambda i: (i, 0, 0))],
        out_shape=[jax.ShapeDtypeStruct((n_layers, m // seq, MLA_KV_DIM, seq), F32),
                   jax.ShapeDtypeStruct((m, 1024), BF16),
                   jax.ShapeDtypeStruct((m, 1024), BF16), jax.ShapeDtypeStruct((m // tm, hv, tm), BF16)],
        input_output_aliases=aliases,
        compiler_params=_cparams(("arbitrary",), VMEM_LIMIT),
        name="mla_pre",
    )(*args)


def _mla_attn_kernel(q_ref, k_ref, vt_ref, o_ref):
    t = MLA_TQ
    nq = q_ref.shape[0] // t
    krow = lax.broadcasted_iota(jnp.int32, (t, t), 0)
    qcol = lax.broadcasted_iota(jnp.int32, (t, t), 1)
    ones = jnp.ones((8, t), BF16)
    for qi in range(nq):
        outs = []
        for j in range(2):
            qh = q_ref[qi * t:(qi + 1) * t, j * LANES:(j + 1) * LANES]
            m = acc = None
            for kb in range(qi + 1):
                s = _dot_nt(k_ref[kb * t:(kb + 1) * t, j * LANES:(j + 1) * LANES], qh)
                if kb == qi:
                    s = jnp.where(krow <= qcol, s, NEG_INF)
                bm = jnp.max(s, axis=0, keepdims=True)
                m_new = bm if m is None else jnp.maximum(m, bm)
                p = jnp.exp(s - m_new).astype(BF16)
                vt1 = jnp.concatenate([vt_ref[kb, j * MLA_V:(j + 1) * MLA_V, :], ones], axis=0)
                pv = _dot(vt1, p)
                acc = pv if m is None else jnp.exp(m - m_new) * acc + pv
                m = m_new
            outs.append(acc[:MLA_V] / acc[MLA_V:MLA_V + 1])
        o_ref[qi * t:(qi + 1) * t, :] = jnp.concatenate(outs, axis=0).T.astype(BF16)


def _mla_attn(q, k, vt, batch, seq):
    nq = seq // MLA_TQ
    vt4 = vt.reshape(batch, nq, MLA_HEADS * MLA_V, MLA_TQ)
    return pl.pallas_call(
        _mla_attn_kernel,
        grid=(batch, MLA_HEADS // 2),
        in_specs=[pl.BlockSpec((seq, 256), lambda b, p: (b, p)),
                  pl.BlockSpec((seq, 256), lambda b, p: (b, p)),
                  pl.BlockSpec((None, nq, 2 * MLA_V, MLA_TQ), lambda b, p: (b, 0, p, 0))],
        out_specs=pl.BlockSpec((seq, LANES), lambda b, p: (b, p)),
        out_shape=jax.ShapeDtypeStruct((batch * seq, MLA_HEADS * MLA_V), BF16),
        compiler_params=_cparams(("parallel", "parallel"), VMEM_LIMIT),
        name="mla_attn",
    )(q, k, vt4)


GLA_TB = 256
GLA_SLOTS = 12
HK = GLA_HEADS * GLA_DK
HV = GLA_HEADS * GLA_DV


def _gla_constants():
    c = GLA_CHUNK
    t = np.arange(c)[:, None]
    s = np.arange(c)[None, :]
    slots = []
    zero = np.zeros((c, c), bool)
    for p in (1, 2, 3):
        slots.append((t // 16 == p) & (s < 16 * p))
    slots.append(zero)
    for p in (1, 2, 3):
        slots.append((t // 16 == s // 16) & ((t % 16) // 4 == p) & (s % 16 < 4 * p))
    slots.append(zero)
    for d in (1, 2, 3):
        slots.append((t == s + d) & (t // 4 == s // 4))
    slots.append(t == s)
    mask = np.concatenate(slots, axis=1).astype(np.float32)
    assert (sum(slots) == (t >= s)).all()
    mask = np.tile(mask, (GLA_HEADS, 1))
    tri = (t >= s).astype(np.float32)
    hm = np.kron(np.eye(GLA_HEADS), np.ones((c, GLA_DK))).astype(np.float32)
    bd = np.kron(np.eye(GLA_HEADS), np.ones((GLA_DV, GLA_DK))).astype(np.float32)
    return (jnp.asarray(tri, BF16), jnp.asarray(hm, F32), jnp.asarray(mask, F32), jnp.asarray(bd, F32))


def _log_sigmoid(x):
    return jnp.minimum(x, 0.0) - jnp.log1p(jnp.exp(-jnp.abs(x)))


def _split3(x):
    hi = x.astype(BF16)
    r1 = x - hi.astype(F32)
    mid = r1.astype(BF16)
    lo = (r1 - mid.astype(F32)).astype(BF16)
    return hi, mid, lo


def _bc_rows(row, n):
    return jnp.broadcast_to(row, (n, row.shape[1]))


def _gla_kernel(blk_ref, q_ref, v_ref, r_ref, k_ref, wg_ref, bg_ref, gn_ref, tri_ref, hm_ref, mask_ref, bd_ref,
                o_ref, st_ref, st_scr):
    c = GLA_CHUNK

    @pl.when(pl.program_id(1) == 0)
    def _():
        st_scr[...] = jnp.zeros_like(st_scr)

    rows = lax.broadcasted_iota(jnp.int32, (c, HK), 0)
    rm4 = rows & 3

    def chunk(ci, st):
        sl = slice(ci * c, (ci + 1) * c)
        pre = _dot(blk_ref[sl, :].astype(BF16), wg_ref[...]) + bg_ref[...]
        g = _log_sigmoid(pre) * (1.0 / GLA_TAU)
        cum3 = _dot(tri_ref[...], jnp.concatenate(_split3(g), axis=1))
        cum = cum3[:, :HK] + cum3[:, HK:2 * HK] + cum3[:, 2 * HK:]
        q = q_ref[sl, :] * (GLA_DK ** -0.5)
        k = k_ref[sl, :]
        vb = v_ref[sl, :].astype(BF16)

        def kvar(e):
            return (k * jnp.exp(jnp.minimum(e, 0.0))).astype(BF16)

        zeros_k = jnp.zeros((c, HK), BF16)
        bnd = [cum[16 * p - 1:16 * p, :] for p in (1, 2, 3)]
        base1 = jnp.concatenate([jnp.zeros((16, HK), F32)] + [_bc_rows(b, 16) for b in bnd], axis=0)
        q1 = q * jnp.exp(cum - base1)
        k1 = jnp.concatenate([kvar(_bc_rows(b, c) - cum) for b in bnd] + [zeros_k], axis=0)
        sh = [pltpu.roll(cum, d, axis=0) for d in (1, 2, 3, 4)]
        base2 = jnp.where(rm4 == 0, sh[0], jnp.where(rm4 == 1, sh[1], jnp.where(rm4 == 2, sh[2], sh[3])))
        base2 = jnp.where(rows < 4, 0.0, base2)
        q2 = q * jnp.exp(cum - base2)
        k2 = []
        for p in (1, 2, 3):
            bk = jnp.concatenate([_bc_rows(cum[16 * i + 4 * p - 1:16 * i + 4 * p, :], 16) for i in range(4)], axis=0)
            k2.append(kvar(bk - cum))
        k2 = jnp.concatenate(k2 + [zeros_k], axis=0)
        k3 = jnp.concatenate([kvar(pltpu.roll(cum, c - d, axis=0) - cum) for d in (1, 2, 3)] + [k.astype(BF16)],
                             axis=0)
        hm = hm_ref[...]

        def stack(x):
            return (jnp.concatenate([x] * GLA_HEADS, axis=0) * hm).astype(BF16)

        p_all = jnp.concatenate([_dot_nt(stack(q1), k1), _dot_nt(stack(q2), k2), _dot_nt(stack(q), k3)], axis=1)
        pm = (p_all * mask_ref[...]).astype(BF16)
        o = _dot_nt((q * jnp.exp(cum)).astype(BF16), st.astype(BF16))
        intra = []
        for h in range(GLA_HEADS):
            vh = vb[:, h * GLA_DV:(h + 1) * GLA_DV]
            intra.append(_dot(pm[h * c:(h + 1) * c, :], jnp.concatenate([vh] * GLA_SLOTS, axis=0)))
        o = o + jnp.concatenate(intra, axis=1)
        last = cum[c - 1:c, :]
        st = st * jnp.exp(last) + _dot_tn(vb, kvar(_bc_rows(last, c) - cum)) * bd_ref[...]
        r = r_ref[sl, :]
        outs = []
        for h in range(GLA_HEADS):
            lanes = slice(h * GLA_DV, (h + 1) * GLA_DV)
            outs.append(_rms_norm(o[:, lanes], gn_ref[...]) * _silu(r[:, lanes]))
        o_ref[sl, :] = jnp.concatenate(outs, axis=1).astype(BF16)
        return st

    st = st_scr[...]
    for ci in range(GLA_TB // c):
        st = chunk(ci, st)
    st_scr[...] = st

    @pl.when(pl.program_id(1) == pl.num_programs(1) - 1)
    def _():
        for h in range(GLA_HEADS):
            blk = st_scr[h * GLA_DV:(h + 1) * GLA_DV, :]
            st_ref[0, h] = blk.T[h * GLA_DK:(h + 1) * GLA_DK, :]


def _gla(h, wg, bg, gn, consts, batch, seq):
    nt = seq // GLA_TB
    tri, hm, mask, bd = consts
    rowblk = lambda w, cb: pl.BlockSpec((GLA_TB, w), lambda b, t: (b * nt + t, cb))
    return pl.pallas_call(
        _gla_kernel,
        grid=(batch, nt),
        in_specs=[rowblk(LANES, 5), rowblk(HK, 3), rowblk(HV, 2), rowblk(HV, 3), rowblk(HK, 8),
                  _resident(wg.shape), _resident(bg.shape), _resident(gn.shape),
                  _resident(tri.shape), _resident(hm.shape), _resident(mask.shape), _resident(bd.shape)],
        out_specs=[pl.BlockSpec((GLA_TB, HV), lambda b, t: (b * nt + t, 0)),
                   pl.BlockSpec((1, GLA_HEADS, GLA_DK, GLA_DV), lambda b, t: (b, 0, 0, 0))],
        out_shape=[jax.ShapeDtypeStruct((batch * seq, HV), BF16),
                   jax.ShapeDtypeStruct((batch, GLA_HEADS, GLA_DK, GLA_DV), F32)],
        scratch_shapes=[pltpu.VMEM((HV, HK), F32)],
        compiler_params=_cparams(("parallel", "arbitrary"), VMEM_LIMIT),
        name="gla",
    )(h, h, h, h, h, wg, bg, gn, tri, hm, mask, bd)


def _alibi_slope(group, head):
    return 2.0 ** (-8.0 * (group * DIL_HPG + head + 1) / (DIL_GROUPS * DIL_HPG))


DIL_TM = 256
QKV_G = 3 * MIX_C


def _proj_dil_kernel(x_ref, w_ref, *rest, n_prev, tiles):
    p_refs = rest[n_prev:n_prev + DIL_GROUPS]
    r1_ref, r2_ref, r3_ref, acc_ref = rest[n_prev + DIL_GROUPS:]
    t = pl.program_id(0) % tiles
    res = _dot(x_ref[...].astype(BF16), w_ref[...])

    def put_rows(g, lo, width):
        for which in range(2):
            c0 = g * QKV_G + (1 + which) * MIX_C
            p_refs[g][which] = res[lo:lo + width, c0:c0 + MIX_C].T.reshape(DIL_HPG, DIL_HD, width)

    put_rows(2, 0, DIL_TM)
    pl.when(t >= tiles - DIL_PAIRS[1][0] // DIL_TM)(lambda: put_rows(1, 0, DIL_TM))
    pl.when(t == tiles - 1)(lambda: put_rows(0, DIL_TM - DIL_PAIRS[0][0], DIL_PAIRS[0][0]))
    r1_ref[...] = res[:, :QKV_G].astype(BF16)
    nb = QKV_G // LANES
    for c in range(2 * nb):
        acc_ref[c] = res[:, QKV_G + c * LANES:QKV_G + (c + 1) * LANES]
    for g, r_ref in ((1, r2_ref), (2, r3_ref)):
        dil = DIL_PAIRS[g][1]
        for r in range(dil):
            for c in range(nb):
                r_ref[r, :, c * LANES:(c + 1) * LANES] = (
                    acc_ref[(g - 1) * nb + c, pl.ds(r, DIL_TM // dil, stride=dil), :].astype(BF16))


def _proj_dil(x, w, seq, prev, layer, n_layers):
    m, k = x.shape
    tm = DIL_TM
    nt = m // tm
    tiles = seq // tm
    res = lambda dil: ((m // seq, dil, seq // dil, QKV_G),
                       pl.BlockSpec((None, dil, tm // dil, QKV_G), lambda i: (i // tiles, 0, i % tiles, 0)))
    (s2, b2), (s3, b3) = res(DIL_PAIRS[1][1]), res(DIL_PAIRS[2][1])
    p_shapes, p_specs = [], []
    for window, _ in DIL_PAIRS:
        keep = min(window, seq)
        width = min(keep, tm)
        first = tiles - max(keep // tm, 1)
        p_shapes.append(jax.ShapeDtypeStruct((n_layers, m // seq, 2, DIL_HPG, DIL_HD, keep), F32))
        p_specs.append(pl.BlockSpec(
            (None, None, 2, DIL_HPG, DIL_HD, width),
            lambda i, first=first: (layer, i // tiles, 0, 0, 0, jnp.maximum(i % tiles - first, 0))))
    in_specs = [pl.BlockSpec((tm, k), lambda i: (i, 0)), _resident(w.shape)]
    args = [x, w]
    aliases = {}
    if prev is not None:
        in_specs += [pl.BlockSpec(memory_space=pl.ANY)] * DIL_GROUPS
        aliases = {len(args) + g: g for g in range(DIL_GROUPS)}
        args += list(prev)
    outs = pl.pallas_call(
        functools.partial(_proj_dil_kernel, n_prev=len(aliases), tiles=tiles),
        grid=(nt,),
        in_specs=in_specs,
        out_specs=p_specs + [pl.BlockSpec((tm, QKV_G), lambda i: (i, 0)), b2, b3],
        out_shape=p_shapes + [jax.ShapeDtypeStruct((m, QKV_G), BF16),
                              jax.ShapeDtypeStruct(s2, BF16), jax.ShapeDtypeStruct(s3, BF16)],
        scratch_shapes=[pltpu.VMEM((2 * QKV_G // LANES, tm, LANES), F32)],
        input_output_aliases=aliases,
        compiler_params=_cparams(("arbitrary",), VMEM_LIMIT),
        name="proj_dil",
    )(*args)
    return outs[:DIL_GROUPS], outs[DIL_GROUPS:]


DIL_QB = 4


def _dil_attn_kernel(q_ref, k_ref, v_ref, *rest, group, dil, mode):
    n = DIL_STEPS
    if mode == "chain":
        kp_ref, vp_ref, o_ref, lse_ref = rest
    else:
        o_ref, lse_ref = rest
    nk = 2 * n
    qi = lax.broadcasted_iota(jnp.int32, (n, nk), 0)
    ki = lax.broadcasted_iota(jnp.int32, (n, nk), 1)
    steps = qi + n - ki
    band = (steps >= 0) & (steps <= n)
    dist = (steps * dil).astype(F32)
    lane = lax.broadcasted_iota(jnp.int32, (n, LANES), 1)
    low = lane < DIL_HD
    for blk in range(DIL_QB):
        cur = slice(blk * n, (blk + 1) * n)
        if mode == "single":
            q = q_ref[blk]
            k = jnp.concatenate([k_ref[blk], k_ref[blk]], axis=0)
            v = jnp.concatenate([v_ref[blk], v_ref[blk]], axis=0)
            valid = band & (ki >= n)
        else:
            q = q_ref[cur, :]
            if blk > 0:
                k, v = k_ref[(blk - 1) * n:(blk + 1) * n, :], v_ref[(blk - 1) * n:(blk + 1) * n, :]
                valid = band
            elif mode == "chain":
                k = jnp.concatenate([kp_ref[...], k_ref[cur, :]], axis=0)
                v = jnp.concatenate([vp_ref[...], v_ref[cur, :]], axis=0)
                valid = band & ((ki >= n) | (pl.program_id(1) > 0))
            else:
                k = jnp.concatenate([k_ref[cur, :], k_ref[cur, :]], axis=0)
                v = jnp.concatenate([v_ref[cur, :], v_ref[cur, :]], axis=0)
                valid = band & (ki >= n)
        o_all = []
        lse = jnp.zeros((n, LANES), F32)
        for j in range(DIL_HPG // 2):
            cols = slice(j * LANES, (j + 1) * LANES)
            q2, k2, v2 = q[:, cols], k[:, cols], v[:, cols]
            outs = []
            for par in range(2):
                qh = jnp.where(low if par == 0 else ~low, q2, jnp.zeros_like(q2))
                s = _dot_nt(qh, k2) * (DIL_HD ** -0.5) - _alibi_slope(group, 2 * j + par) * dist
                s = jnp.where(valid, s, NEG_INF)
                m = jnp.max(s, axis=-1, keepdims=True)
                p = jnp.exp(s - m)
                l = jnp.sum(p, axis=-1, keepdims=True)
                outs.append(_dot(p.astype(BF16), v2) / l)
                lse = jnp.where(lane == 2 * j + par, m + jnp.log(l), lse)
            o_all.append(jnp.where(low, outs[0], outs[1]))
        dst = (blk,) if mode == "single" else (cur, slice(None))
        o_ref[dst] = jnp.concatenate(o_all, axis=1).astype(BF16)
        lse_ref[dst] = lse


def _dil_attn(qkv_r, group):
    n = DIL_STEPS
    batch, dil, ls, _ = qkv_r.shape
    rows = DIL_QB * n
    if ls == n:
        mode, steps = "single", dil // DIL_QB
        spec = lambda c, w=MIX_C: pl.BlockSpec((None, DIL_QB, n, w), lambda b, s: (b, s, 0, c))
        extra = []
    elif ls == rows:
        mode, steps = "whole", dil
        spec = lambda c, w=MIX_C: pl.BlockSpec((None, None, rows, w), lambda b, s: (b, s, 0, c))
        extra = []
    else:
        assert dil == 1 and ls % rows == 0
        mode, steps = "chain", ls // rows
        spec = lambda c, w=MIX_C: pl.BlockSpec((None, None, rows, w), lambda b, s: (b, 0, s, c))
        extra = [pl.BlockSpec((None, None, n, MIX_C), lambda b, s, c=c: (b, 0, jnp.maximum(s * DIL_QB - 1, 0), c))
                 for c in (1, 2)]
    in_specs = [spec(0), spec(1), spec(2)] + extra
    return pl.pallas_call(
        functools.partial(_dil_attn_kernel, group=group, dil=DIL_PAIRS[group][1], mode=mode),
        grid=(batch, steps),
        in_specs=in_specs,
        out_specs=[spec(0), spec(0, LANES)],
        out_shape=[jax.ShapeDtypeStruct((batch, dil, ls, MIX_C), BF16),
                   jax.ShapeDtypeStruct((batch, dil, ls, LANES), F32)],
        compiler_params=_cparams(("parallel", "arbitrary"), VMEM_LIMIT),
        name="dil_attn%d" % group,
    )(*([qkv_r] * len(in_specs)))


def _dil_mix_ln_kernel(o1, o2, o3, l1, l2, l3, e_ref, w_ref, x_ref, g_ref, b_ref, o_ref, so2, so3, sl2, sl3):
    for src, dst, dil in ((o2, so2, DIL_PAIRS[1][1]), (l2, sl2, DIL_PAIRS[1][1]),
                          (o3, so3, DIL_PAIRS[2][1]), (l3, sl3, DIL_PAIRS[2][1])):
        for r in range(dil):
            for c in range(dst.shape[0]):
                dst[c, pl.ds(r, DIL_TM // dil, stride=dil), :] = src[r, :, c * LANES:(c + 1) * LANES].astype(F32)
    whole = lambda s: jnp.concatenate([s[c] for c in range(s.shape[0])], axis=1)
    os_ = [o1[...].astype(F32), whole(so2), whole(so3)]
    ls = [l1[...], sl2[0], sl3[0]]
    mx = jnp.maximum(jnp.maximum(ls[0], ls[1]), ls[2])
    es = [jnp.exp(l - mx) for l in ls]
    den = es[0] + es[1] + es[2]

    def spread(wt):
        hi = wt.astype(BF16)
        lo = (wt - hi.astype(F32)).astype(BF16)
        return _dot(hi, e_ref[...]) + _dot(lo, e_ref[...])

    mix = sum(spread(e / den) * o for e, o in zip(es, os_))
    z = DN_ALPHA * x_ref[...] + _dot(mix.astype(BF16), w_ref[...])
    o_ref[...] = _layer_norm(z, g_ref[...], b_ref[...])


def _dil_mix_ln(os_, lses, w, x, g, b):
    m = x.shape[0]
    tm = DIL_TM
    nt = m // tm
    tiles = nt // os_[0].shape[0]
    row = lambda wd: pl.BlockSpec((tm, wd), lambda i: (i, 0))
    res = lambda a: pl.BlockSpec((None, a.shape[1], tm // a.shape[1], a.shape[3]),
                                 lambda i: (i // tiles, 0, i % tiles, 0))
    flat = lambda a: a.reshape(m, a.shape[3])
    spread = np.kron(np.eye(DIL_HPG), np.ones((1, DIL_HD)))
    spread = jnp.asarray(np.concatenate([spread, np.zeros((LANES - DIL_HPG, MIX_C))]), BF16)
    blocks = lambda wd: pltpu.VMEM((wd // LANES, tm, LANES), F32)
    return pl.pallas_call(
        _dil_mix_ln_kernel,
        grid=(nt,),
        in_specs=[row(MIX_C), res(os_[1]), res(os_[2]), row(LANES), res(lses[1]), res(lses[2]), _resident(spread.shape),
                  _resident(w.shape), row(D_MODEL), _resident(g.shape), _resident(b.shape)],
        out_specs=row(D_MODEL),
        out_shape=jax.ShapeDtypeStruct((m, D_MODEL), F32),
        scratch_shapes=[blocks(MIX_C), blocks(MIX_C), blocks(LANES), blocks(LANES)],
        compiler_params=_cparams(("parallel",), VMEM_LIMIT),
        name="dil_mix_ln",
    )(flat(os_[0]), os_[1], os_[2], flat(lses[0]), lses[1], lses[2], spread, w, x, g, b)


def _mla_pre_s_kernel(h_ref, qn_ref, kvn_ref, wqn_ref, wqp_ref, wukt_ref, ck_ref, sk_ref, rows_ref, q_ref):
    h = h_ref[...]
    qn = _rms_norm(h[:, :MLA_Q_LORA], qn_ref[...]).astype(BF16)
    qnope = _dot(qn, wqn_ref[...]).astype(BF16)
    qp = _dot(qn, wqp_ref[...])
    cq = jnp.concatenate([ck_ref[...]] * MLA_HEADS, axis=1)
    sq = jnp.concatenate([sk_ref[...]] * MLA_HEADS, axis=1)
    qp = qp * cq + _roll_lanes_left(qp, 32) * sq
    ckv = _rms_norm(h[:, MLA_Q_LORA:MLA_Q_LORA + MLA_KV_LORA], kvn_ref[...])
    blk = h[:, 640:768]
    kpe = blk * ck_ref[...] + _roll_lanes_left(blk, 32) * sk_ref[...]
    rows_ref[:, :MLA_KV_LORA] = ckv
    rows_ref[:, MLA_KV_LORA:] = kpe[:, :MLA_ROPE]
    for hd in range(MLA_HEADS):
        cols = slice(hd * LANES, (hd + 1) * LANES)
        q_ref[hd, :, :MLA_KV_LORA] = _dot(qnope[:, cols], wukt_ref[hd])
        q_ref[hd, :, MLA_KV_LORA:] = qp[:, hd * LANES:hd * LANES + MLA_ROPE]


def _mla_pre_s(h, qn, kvn, wqn, wqp, wukt, tabs):
    m = h.shape[0]
    full = lambda a: _resident(a.shape)
    return pl.pallas_call(
        _mla_pre_s_kernel,
        grid=(1,),
        in_specs=[pl.BlockSpec((m, 768), lambda i: (0, 0)), full(qn), full(kvn), full(wqn), full(wqp), full(wukt)]
        + [full(t) for t in tabs[2:]],
        out_specs=[pl.BlockSpec((m, MLA_KV_DIM), lambda i: (0, 0)),
                   pl.BlockSpec((MLA_HEADS, m, MLA_KV_DIM), lambda i: (0, 0, 0))],
        out_shape=[jax.ShapeDtypeStruct((m, MLA_KV_DIM), F32), jax.ShapeDtypeStruct((MLA_HEADS, m, MLA_KV_DIM), F32)],
        compiler_params=_cparams(("arbitrary",), VMEM_LIMIT),
        name="mla_pre_s",
    )(h, qn, kvn, wqn, wqp, wukt, *tabs[2:])


DEC_PAGES = 32


def _mla_dec_kernel(pt_ref, q_ref, new_ref, *rest):
    pages = rest[:DEC_PAGES]
    o_ref, m_scr, l_scr, acc_scr = rest[DEC_PAGES:]
    step = pl.program_id(1)
    q = q_ref[0]
    qb = q.astype(BF16)

    @pl.when(step == 0)
    def _():
        new = new_ref[0]
        m_scr[...] = jnp.sum(q * new, axis=-1, keepdims=True) * MLA_SCALE
        l_scr[...] = jnp.ones_like(l_scr)
        acc_scr[...] = jnp.broadcast_to(new[:, :MLA_KV_LORA], acc_scr.shape)

    kvt = jnp.concatenate([p[...].astype(BF16) for p in pages], axis=1)
    s = _dot(qb, kvt) * MLA_SCALE
    m_old = m_scr[...]
    m_new = jnp.maximum(m_old, jnp.max(s, axis=-1, keepdims=True))
    alpha = jnp.exp(m_old - m_new)
    p = jnp.exp(s - m_new)
    m_scr[...] = m_new
    l_scr[...] = alpha * l_scr[...] + jnp.sum(p, axis=-1, keepdims=True)
    acc_scr[...] = alpha * acc_scr[...] + _dot_nt(p.astype(BF16), kvt[:MLA_KV_LORA, :])

    @pl.when(step == pl.num_programs(1) - 1)
    def _():
        o_ref[0] = acc_scr[...] / l_scr[...]


def _mla_dec(page_table, q_abs, rows_new, cache_t, layer):
    batch = q_abs.shape[0]
    n_pages = page_table.shape[1]
    steps = n_pages // DEC_PAGES
    pt = page_table.reshape(-1)

    def page_spec(j):
        return pl.BlockSpec((None, None, MLA_KV_DIM, PAGE_SIZE),
                            lambda b, s, pt_ref: (layer, pt_ref[b * n_pages + s * DEC_PAGES + j], 0, 0))

    grid_spec = pltpu.PrefetchScalarGridSpec(
        num_scalar_prefetch=1,
        grid=(batch, steps),
        in_specs=[pl.BlockSpec((1, MLA_HEADS, MLA_KV_DIM), lambda b, s, pt_ref: (b, 0, 0)),
                  pl.BlockSpec((1, 1, MLA_KV_DIM), lambda b, s, pt_ref: (b, 0, 0))]
        + [page_spec(j) for j in range(DEC_PAGES)],
        out_specs=pl.BlockSpec((1, MLA_HEADS, MLA_KV_LORA), lambda b, s, pt_ref: (b, 0, 0)),
        scratch_shapes=[pltpu.VMEM((MLA_HEADS, 1), F32), pltpu.VMEM((MLA_HEADS, 1), F32),
                        pltpu.VMEM((MLA_HEADS, MLA_KV_LORA), F32)],
    )
    return pl.pallas_call(
        _mla_dec_kernel,
        grid_spec=grid_spec,
        out_shape=jax.ShapeDtypeStruct((batch, MLA_HEADS, MLA_KV_LORA), F32),
        compiler_params=_cparams(("parallel", "arbitrary"), VMEM_LIMIT),
        name="mla_dec",
    )(pt, q_abs, rows_new.reshape(batch, 1, MLA_KV_DIM), *([cache_t] * DEC_PAGES))


GLA_DEC_B = 8


def _gla_dec_kernel(blk_ref, q_ref, v_ref, r_ref, k_ref, s_ref, wg_ref, bg_ref, gn_ref, o_ref, so_ref):
    pre = _dot(blk_ref[...].astype(BF16), wg_ref[...]) + bg_ref[...]
    a_all = jnp.exp(_log_sigmoid(pre) * (1.0 / GLA_TAU))
    eye = (lax.broadcasted_iota(jnp.int32, (GLA_DK, GLA_DK), 0)
           == lax.broadcasted_iota(jnp.int32, (GLA_DK, GLA_DK), 1))
    diag = lambda row: jnp.where(eye, jnp.broadcast_to(row, (GLA_DK, GLA_DK)), 0.0)
    for t in range(GLA_DEC_B):
        tok = slice(t, t + 1)
        a, k, v, r = a_all[tok], k_ref[tok, :], v_ref[tok, :], r_ref[tok, :]
        q = q_ref[tok, :] * (GLA_DK ** -0.5)
        outs = []
        for h in range(GLA_HEADS):
            kl = slice(h * GLA_DK, (h + 1) * GLA_DK)
            vl = slice(h * GLA_DV, (h + 1) * GLA_DV)
            lhs = jnp.concatenate([diag(a[:, kl]), diag(k[:, kl])], axis=1)
            rhs = jnp.concatenate([s_ref[t, h], jnp.broadcast_to(v[:, vl], (GLA_DK, GLA_DV))], axis=0)
            s_new = jnp.dot(lhs, rhs, preferred_element_type=F32, precision=lax.Precision.HIGHEST)
            so_ref[t, h] = s_new
            o = jnp.dot(jnp.broadcast_to(q[:, kl], (8, GLA_DK)), s_new, preferred_element_type=F32,
                        precision=lax.Precision.HIGHEST)[:1]
            outs.append(_rms_norm(o, gn_ref[...]) * _silu(r[:, vl]))
        o_ref[tok, :] = jnp.concatenate(outs, axis=1)


def _gla_dec(h, state, layer, wg, bg, gn):
    batch = h.shape[0]
    tb = GLA_DEC_B
    rowblk = lambda w, cb: pl.BlockSpec((tb, w), lambda b: (b, cb))
    st_in = pl.BlockSpec((None, tb, GLA_HEADS, GLA_DK, GLA_DV), lambda b: (layer, b, 0, 0, 0))
    st = pl.BlockSpec((tb, GLA_HEADS, GLA_DK, GLA_DV), lambda b: (b, 0, 0, 0))
    return pl.pallas_call(
        _gla_dec_kernel,
        grid=(batch // tb,),
        in_specs=[rowblk(LANES, 5), rowblk(HK, 3), rowblk(HV, 2), rowblk(HV, 3), rowblk(HK, 8), st_in,
                  _resident(wg.shape), _resident(bg.shape), _resident(gn.shape)],
        out_specs=[pl.BlockSpec((tb, HV), lambda b: (b, 0)), st],
        out_shape=[jax.ShapeDtypeStruct((batch, HV), F32), jax.ShapeDtypeStruct(state.shape[1:], F32)],
        compiler_params=_cparams(("parallel",), VMEM_LIMIT),
        name="gla_dec",
    )(h, h, h, h, h, state, wg, bg, gn)


def _dil_dec_kernel(x_ref, c1_ref, c2_ref, c3_ref, o_ref):
    heads = range(DIL_HPG)
    hcol = lax.broadcasted_iota(jnp.int32, (DIL_HPG, 1), 0).astype(F32)
    outs, lses = [], []
    for g, c_ref in enumerate((c1_ref, c2_ref, c3_ref)):
        window, dil = DIL_PAIRS[g]
        pos = lax.broadcasted_iota(jnp.int32, (1, window), 1)
        q, k_new, v_new = (x_ref[0, 3 * g + w] for w in range(3))
        s = jnp.concatenate([jnp.sum(c_ref[0, h] * q[:, h:h + 1], axis=0, keepdims=True) for h in heads], axis=0)
        slope = jnp.exp((g * DIL_HPG + hcol + 1.0) * (-8.0 * math.log(2.0) / (DIL_GROUPS * DIL_HPG)))
        s = s * (DIL_HD ** -0.5) - slope * (window - pos).astype(F32)
        s = jnp.where((pos & (dil - 1)) == 0, s, NEG_INF)
        qk = jnp.sum(q * k_new, axis=0, keepdims=True) * (DIL_HD ** -0.5)
        s_new = jnp.concatenate([qk[:, h:h + 1] for h in heads], axis=0)
        m = jnp.maximum(jnp.max(s, axis=1, keepdims=True), s_new)
        p = jnp.exp(s - m)
        p_new = jnp.exp(s_new - m)
        l = jnp.sum(p, axis=1, keepdims=True) + p_new
        w = p / l
        w_new = p_new / l
        outs.append([jnp.sum(c_ref[1, h] * w[h:h + 1, :], axis=1, keepdims=True)
                     + w_new[h:h + 1, :] * v_new[:, h:h + 1] for h in heads])
        lses.append(m + jnp.log(l))
    mx = jnp.maximum(jnp.maximum(lses[0], lses[1]), lses[2])
    es = [jnp.exp(l - mx) for l in lses]
    den = es[0] + es[1] + es[2]
    ws = [e / den for e in es]
    o_ref[0] = jnp.concatenate(
        [sum(ws[g][h:h + 1, :] * outs[g][h] for g in range(DIL_GROUPS)) for h in heads], axis=1)


def _dil_dec(qkv_t, caches_t, layer):
    batch = qkv_t.shape[0]
    specs = [pl.BlockSpec((1, 3 * DIL_GROUPS, DIL_HD, DIL_HPG), lambda b: (b, 0, 0, 0))]
    for (window, _), c in zip(DIL_PAIRS, caches_t):
        assert c.shape[-1] == window
        specs.append(pl.BlockSpec((None, None, 2, DIL_HPG, DIL_HD, window), lambda b: (layer, b, 0, 0, 0, 0)))
    return pl.pallas_call(
        _dil_dec_kernel,
        grid=(batch,),
        in_specs=specs,
        out_specs=pl.BlockSpec((1, DIL_HD, DIL_HPG), lambda b: (b, 0, 0)),
        out_shape=jax.ShapeDtypeStruct((batch, DIL_HD, DIL_HPG), F32),
        compiler_params=_cparams(("parallel",), VMEM_LIMIT),
        name="dil_dec",
    )(qkv_t, *caches_t)


def _even_in_cols():
    o_cq, o_kv, o_gq, o_gk, o_gv, o_lr, o_gr = 0, 384, 672, 928, 1184, 1696, 1712
    r = np.arange
    kpe = o_kv + MLA_KV_LORA
    idx = np.concatenate([r(o_cq, o_cq + 384), r(o_kv, o_kv + 256), r(kpe, kpe + 32), r(kpe + 16, kpe + 32),
                          r(kpe, kpe + 16), r(o_lr, o_lr + 16), np.full(48, -1), r(o_gq, o_gq + 256),
                          r(o_gv, o_gv + 512), r(o_gr, o_gr + 512), r(o_gk, o_gk + 256)])
    assert idx.shape[0] == IN_A_PAD
    return idx


def _gather_cols(w, idx):
    idx = np.asarray(idx)
    a, b = idx[:-1], idx[1:]
    same_run = ((a >= 0) & (b >= 0) & (b - a == 1)) | ((a < 0) & (b < 0))
    cuts = np.flatnonzero(~same_run) + 1
    parts = []
    for run in np.split(idx, cuts):
        if run[0] < 0:
            parts.append(jnp.zeros((w.shape[0], run.size), w.dtype))
        else:
            parts.append(w[:, int(run[0]):int(run[-1]) + 1])
    return jnp.concatenate(parts, axis=1)


def _uq_cols(kind):
    idx = []
    for h in range(MLA_HEADS):
        b = h * (MLA_NOPE + MLA_ROPE)
        nope = np.arange(b, b + 64)
        x1, x2 = np.arange(b + 64, b + 80), np.arange(b + 80, b + 96)
        if kind == "full":
            idx += [nope, x1, x2, x2, x1]
        elif kind == "nope":
            idx += [nope, np.full(64, -1)]
        else:
            idx += [x1, x2, x2, x1, np.full(64, -1)]
    return np.concatenate(idx)


def _odd_in_cols():
    blk = lambda which, g: np.arange((which * DIL_GROUPS + g) * MIX_C, (which * DIL_GROUPS + g + 1) * MIX_C)
    return np.concatenate([blk(which, g) for g in range(DIL_GROUPS) for which in range(3)])


TM_DENSE = 512
TM_WIDE = 256


def kernel(x_prompt, x_sample, cache_mla, state_gla, cache_dil_w128, cache_dil_w512, cache_dil_w2048, page_table,
           w_in_a, mla_q_norm, mla_w_uq, mla_kv_norm, mla_w_uk, mla_w_uv, gla_w_gate2, gla_b_gate, gla_norm, w_out_a,
           w_in_c, w_out_c, ffn_w_in, ffn_w_out, ln_g, ln_b):
    batch, seq, _ = x_prompt.shape
    dbatch = x_sample.shape[0]
    assert seq % MLA_TQ == 0 and seq % DIL_TM == 0 and seq >= DIL_PAIRS[-1][0]
    n_even, n_odd = (DEPTH + 1) // 2, DEPTH // 2
    cache_mla_t = jnp.transpose(cache_mla, (0, 1, 3, 2))
    dil_caches_t = [jnp.transpose(c, (0, 1, 3, 4, 5, 2)) for c in (cache_dil_w128, cache_dil_w512, cache_dil_w2048)]
    gla_consts = _gla_constants()
    tabs_p = _rope_tables(jnp.arange(seq, dtype=jnp.int32))
    tabs_s = _rope_tables(jnp.full((dbatch,), PAST_LEN, jnp.int32))
    even_cols, odd_cols = _even_in_cols(), _odd_in_cols()
    row2 = lambda v: v.reshape(1, -1).astype(F32)

    xp = x_prompt.reshape(batch * seq, D_MODEL)
    xs = x_sample.reshape(dbatch, D_MODEL)
    mla_s, gla_p, gla_s = [], [], []
    rows_t, dil_t = None, None
    dil_s = [[] for _ in DIL_PAIRS]

    for layer in range(DEPTH):
        i = layer // 2
        g0, b0, g1, b1 = (row2(ln_g[layer, 0]), row2(ln_b[layer, 0]), row2(ln_g[layer, 1]), row2(ln_b[layer, 1]))
        if layer % 2 == 0:
            w_in = _gather_cols(w_in_a[i], even_cols).astype(BF16)
            wq_full = _gather_cols(mla_w_uq[i], _uq_cols("full")).astype(BF16)
            wq_nope = _gather_cols(mla_w_uq[i], _uq_cols("nope")).astype(BF16)
            wq_rope = _gather_cols(mla_w_uq[i], _uq_cols("rope")).astype(BF16)
            wuk_pad = jnp.pad(mla_w_uk[i], ((0, 0), (0, 0), (0, LANES - MLA_NOPE))).reshape(MLA_KV_LORA, -1).astype(BF16)
            wuk_t = jnp.pad(jnp.transpose(mla_w_uk[i], (1, 2, 0)), ((0, 0), (0, LANES - MLA_NOPE), (0, 0))).astype(BF16)
            wuv_t = mla_w_uv[i].reshape(MLA_KV_LORA, -1).T.astype(BF16)
            wuv_bd = (jnp.eye(MLA_HEADS, dtype=F32)[:, None, :, None]
                      * jnp.transpose(mla_w_uv[i], (1, 0, 2))[:, :, None, :]).reshape(
                          MLA_HEADS * MLA_KV_LORA, MLA_HEADS * MLA_V).astype(BF16)
            wg = jnp.pad(gla_w_gate2[i], ((64, LANES - 64 - GLA_GATE_RANK), (0, 0))).astype(BF16)
            bg, gn = row2(gla_b_gate[i]), row2(gla_norm[i])
            qn, kvn = row2(mla_q_norm[i]), row2(mla_kv_norm[i])
            w_out = w_out_a[i].astype(BF16)

            h = _proj(xp, w_in, TM_DENSE)
            rows_t, qf, kf, vt = _mla_pre(h, qn, kvn, wq_full, wuk_pad, wuv_t, tabs_p, seq, rows_t, i, n_even)
            mla_out = _mla_attn(qf, kf, vt, batch, seq)
            gla_out, s_fin = _gla(h, wg, bg, gn, gla_consts, batch, seq)
            gla_p.append(s_fin)
            xp = _mix_ln([mla_out, gla_out], w_out, xp, g0, b0, TM_DENSE)

            h = _proj(xs, w_in, TM_DENSE)
            rows, q_abs = _mla_pre_s(h, qn, kvn, wq_nope, wq_rope, wuk_t, tabs_s)
            lat = _mla_dec(page_table, jnp.transpose(q_abs, (1, 0, 2)), rows, cache_mla_t, i)
            mla_out = _proj(lat.reshape(dbatch, MLA_HEADS * MLA_KV_LORA), wuv_bd, TM_DENSE)
            gla_out, s_fin = _gla_dec(h, state_gla, i, wg, bg, gn)
            mla_s.append(rows.reshape(dbatch, 1, MLA_KV_DIM))
            gla_s.append(s_fin)
            xs = _mix_ln([mla_out, gla_out], w_out, xs, g0, b0, TM_DENSE)
        else:
            w_in = _gather_cols(w_in_c[i], odd_cols).astype(BF16)
            w_out = w_out_c[i].astype(BF16)

            dil_t, (r1, r2, r3) = _proj_dil(xp, w_in, seq, dil_t, i, n_odd)
            res = [r1.reshape(batch, 1, seq, QKV_G), r2, r3]
            os_, lses = zip(*[_dil_attn(res[g], g) for g in range(DIL_GROUPS)])
            xp = _dil_mix_ln(os_, lses, w_out, xp, g0, b0)

            qkv = _proj(xs, w_in, TM_WIDE).reshape(dbatch, 3 * DIL_GROUPS, DIL_HPG, DIL_HD)
            mix = _dil_dec(jnp.transpose(qkv, (0, 1, 3, 2)), dil_caches_t, i)
            mix = jnp.transpose(mix, (0, 2, 1)).reshape(dbatch, MIX_C)
            for g in range(DIL_GROUPS):
                dil_s[g].append(qkv[:, 3 * g + 1:3 * g + 3].reshape(dbatch, 1, 2, DIL_HPG, DIL_HD))
            xs = _mix_ln([mix], w_out, xs, g0, b0, TM_DENSE)

        wi, wo = ffn_w_in[layer].astype(BF16), ffn_w_out[layer].astype(BF16)
        xp = _ffn(xp, wi, wo, g1, b1, TM_DENSE)
        xs = _ffn(xs, wi, wo, g1, b1, TM_DENSE)

    st = jnp.stack
    mla_p = jnp.transpose(rows_t, (0, 1, 3, 2))
    dil_p = [jnp.transpose(p, (0, 1, 5, 2, 3, 4)) for p in dil_t]
    return (xp.reshape(batch, seq, D_MODEL), xs.reshape(dbatch, 1, D_MODEL), mla_p, st(mla_s), st(gla_p), st(gla_s),
            dil_p[0], st(dil_s[0]), dil_p[1], st(dil_s[1]), dil_p[2], st(dil_s[2]))
```

```python
import functools
import math

import numpy as np
import jax
import jax.numpy as jnp
from jax import lax
from jax.experimental import pallas as pl
from jax.experimental.pallas import tpu as pltpu

F32 = jnp.float32
BF16 = jnp.bfloat16

D_MODEL = 1024
DEPTH = 4
PAST_LEN = 16384
PAGE_SIZE = 128
MLA_HEADS = 8
MLA_Q_LORA = 384
MLA_KV_LORA = 256
MLA_NOPE = 64
MLA_ROPE = 32
MLA_V = 64
MLA_KV_DIM = MLA_KV_LORA + MLA_ROPE
ROPE_BASE = 10000.0
GLA_HEADS = 4
GLA_DK = 64
GLA_DV = 128
GLA_GATE_RANK = 16
GLA_TAU = 16.0
GLA_CHUNK = 64
DIL_PAIRS = ((128, 1), (512, 4), (2048, 16))
DIL_GROUPS = 3
DIL_HPG = 8
DIL_HD = 64
DIL_STEPS = 128
D_FF = -(-8 * D_MODEL // (3 * 256)) * 256
DN_ALPHA = (2.0 * DEPTH) ** 0.25
IN_A_PAD = 2304
MIX_C = DIL_HPG * DIL_HD
IN_C = 3 * DIL_GROUPS * MIX_C
MLA_SCALE = (MLA_NOPE + MLA_ROPE) ** -0.5

V7X_VMEM_BYTES = 64 * 1024 * 1024
VMEM_LIMIT = 56 * 1024 * 1024
LANES = 128

NEG_INF = float("-inf")


def _cparams(sem, limit=None):
    return pltpu.CompilerParams(dimension_semantics=sem, vmem_limit_bytes=limit)


def _resident(shape):
    nd = len(shape)
    return pl.BlockSpec(shape, lambda *_: (0,) * nd, pipeline_mode=pl.Buffered(1))


def _layer_norm(z, g, b):
    mu = jnp.mean(z, axis=-1, keepdims=True)
    zc = z - mu
    var = jnp.mean(zc * zc, axis=-1, keepdims=True)
    return zc * lax.rsqrt(var + 1e-5) * g + b


def _rms_norm(z, g):
    return z * lax.rsqrt(jnp.mean(z * z, axis=-1, keepdims=True) + 1e-6) * g


def _silu(z):
    return z * (1.0 / (1.0 + jnp.exp(-z)))


def _dot(a, b):
    return jnp.dot(a, b, preferred_element_type=F32)


def _dot_nt(a, b):
    return lax.dot_general(a, b, (((1,), (1,)), ((), ())), preferred_element_type=F32)


def _dot_tn(a, b):
    return lax.dot_general(a, b, (((0,), (0,)), ((), ())), preferred_element_type=F32)


def _proj_kernel(x_ref, w_ref, o_ref):
    o_ref[...] = _dot(x_ref[...].astype(BF16), w_ref[...]).astype(o_ref.dtype)


def _proj(x, w, tm, out_dtype=F32):
    m, k = x.shape
    n = w.shape[1]
    tm = min(tm, m)
    return pl.pallas_call(
        _proj_kernel,
        grid=(m // tm,),
        in_specs=[pl.BlockSpec((tm, k), lambda i: (i, 0)), _resident((k, n))],
        out_specs=pl.BlockSpec((tm, n), lambda i: (i, 0)),
        out_shape=jax.ShapeDtypeStruct((m, n), out_dtype),
        compiler_params=_cparams(("parallel",), VMEM_LIMIT),
        name="proj",
    )(x, w)


def _mix_ln_kernel(*refs, n_parts):
    parts = refs[:n_parts]
    w_ref, x_ref, g_ref, b_ref, o_ref = refs[n_parts:]
    a = jnp.concatenate([p[...].astype(BF16) for p in parts], axis=-1) if n_parts > 1 else parts[0][...].astype(BF16)
    z = DN_ALPHA * x_ref[...] + _dot(a, w_ref[...])
    o_ref[...] = _layer_norm(z, g_ref[...], b_ref[...])


def _mix_ln(parts, w, x, g, b, tm):
    m = x.shape[0]
    tm = min(tm, m)
    kern = functools.partial(_mix_ln_kernel, n_parts=len(parts))
    return pl.pallas_call(
        kern,
        grid=(m // tm,),
        in_specs=[pl.BlockSpec((tm, p.shape[1]), lambda i: (i, 0)) for p in parts]
        + [_resident(w.shape), pl.BlockSpec((tm, D_MODEL), lambda i: (i, 0)), _resident(g.shape), _resident(b.shape)],
        out_specs=pl.BlockSpec((tm, D_MODEL), lambda i: (i, 0)),
        out_shape=jax.ShapeDtypeStruct((m, D_MODEL), F32),
        compiler_params=_cparams(("parallel",), VMEM_LIMIT),
        name="mix_ln",
    )(*parts, w, x, g, b)


FF_CHUNK = 256
TM_FFN = 1024


def _ffn_kernel(x_ref, wi_ref, wo_ref, g_ref, b_ref, o_ref, act_ref):
    x = x_ref[...]
    xb = x.astype(BF16)
    for c in range(D_FF // FF_CHUNK):
        lo = c * FF_CHUNK
        gate = _dot(xb, wi_ref[:, lo:lo + FF_CHUNK])
        up = _dot(xb, wi_ref[:, D_FF + lo:D_FF + lo + FF_CHUNK])
        act_ref[:, lo:lo + FF_CHUNK] = (_silu(gate) * up).astype(BF16)
    z = DN_ALPHA * x + _dot(act_ref[...], wo_ref[...])
    o_ref[...] = _layer_norm(z, g_ref[...], b_ref[...])


def _ffn(x, wi, wo, g, b, tm):
    m = x.shape[0]
    tm = min(tm, m)
    return pl.pallas_call(
        _ffn_kernel,
        grid=(m // tm,),
        in_specs=[pl.BlockSpec((tm, D_MODEL), lambda i: (i, 0)), _resident(wi.shape), _resident(wo.shape),
                  _resident(g.shape), _resident(b.shape)],
        out_specs=pl.BlockSpec((tm, D_MODEL), lambda i: (i, 0)),
        out_shape=jax.ShapeDtypeStruct((m, D_MODEL), F32),
        scratch_shapes=[pltpu.VMEM((tm, D_FF), BF16)],
        compiler_params=_cparams(("parallel",), VMEM_LIMIT),
        name="ffn",
    )(x, wi, wo, g, b)


def _rope_tables(pos):
    half = MLA_ROPE // 2
    inv = ROPE_BASE ** (-jnp.arange(half, dtype=F32) / half)
    ang = pos.astype(F32)[:, None] * inv[None, :]
    cos, sin = jnp.cos(ang), jnp.sin(ang)
    n = pos.shape[0]
    z = lambda w: jnp.zeros((n, w), F32)
    cc = jnp.concatenate([cos, cos], axis=1)
    ss = jnp.concatenate([-sin, sin], axis=1)
    cq = jnp.concatenate([jnp.ones((n, MLA_NOPE), F32), cc, z(32)], axis=1)
    sq = jnp.concatenate([z(MLA_NOPE), ss, z(32)], axis=1)
    ck = jnp.concatenate([cc, z(96)], axis=1)
    sk = jnp.concatenate([ss, z(96)], axis=1)
    return cq, sq, ck, sk


def _roll_lanes_left(x, k):
    return pltpu.roll(x, x.shape[-1] - k, axis=x.ndim - 1)


def _mla_pre_kernel(h_ref, qn_ref, kvn_ref, wq_ref, wuk_ref, wuvt_ref, cq_ref, sq_ref, ck_ref, sk_ref, *rest):
    rows_ref, q_ref, k_ref, vt_ref = rest[-4:]
    h = h_ref[...]
    qn = _rms_norm(h[:, :MLA_Q_LORA], qn_ref[...]).astype(BF16)
    q = _dot(qn, wq_ref[...])
    cq = jnp.concatenate([cq_ref[...]] * MLA_HEADS, axis=1)
    sq = jnp.concatenate([sq_ref[...]] * MLA_HEADS, axis=1)
    q_ref[...] = ((q * cq + _roll_lanes_left(q, 32) * sq) * MLA_SCALE).astype(BF16)
    ckv = _rms_norm(h[:, MLA_Q_LORA:MLA_Q_LORA + MLA_KV_LORA], kvn_ref[...])
    blk = h[:, 640:768]
    kpe = blk * ck_ref[...] + _roll_lanes_left(blk, 32) * sk_ref[...]
    rows_ref[:MLA_KV_LORA, :] = ckv.T
    rows_ref[MLA_KV_LORA:, :] = kpe.T[:MLA_ROPE, :]
    ckvb = ckv.astype(BF16)
    kpe_mid = pltpu.roll(kpe, 64, axis=1)
    k_ref[...] = (_dot(ckvb, wuk_ref[...]) + jnp.concatenate([kpe_mid] * MLA_HEADS, axis=1)).astype(BF16)
    vt_ref[0] = _dot_nt(wuvt_ref[...], ckvb).astype(BF16)


MLA_TQ = 512


def _mla_pre(h, qn, kvn, wq, wuk, wuvt, tabs, seq, rows_t, layer, n_layers):
    m = h.shape[0]
    tm = MLA_TQ
    nblk = seq // tm
    tab = pl.BlockSpec((tm, LANES), lambda i: (i % nblk, 0))
    row = lambda w: pl.BlockSpec((tm, w), lambda i: (i, 0))
    hv = MLA_HEADS * MLA_V
    in_specs = [pl.BlockSpec((tm, 768), lambda i: (i, 0)), _resident(qn.shape), _resident(kvn.shape),
                _resident(wq.shape), _resident(wuk.shape), _resident(wuvt.shape), tab, tab, tab, tab]
    args = [h, qn, kvn, wq, wuk, wuvt, *tabs]
    aliases = {}
    if rows_t is not None:
        in_specs.append(pl.BlockSpec(memory_space=pl.ANY))
        args.append(rows_t)
        aliases = {len(args) - 1: 0}
    return pl.pallas_call(
        _mla_pre_kernel,
        grid=(m // tm,),
        in_specs=in_specs,
        out_specs=[pl.BlockSpec((None, None, MLA_KV_DIM, tm), lambda i: (layer, i // nblk, 0, i % nblk)),
                   row(1024), row(1024), pl.BlockSpec((1, hv, tm), lambda i: (i, 0, 0))],
        out_shape=[jax.ShapeDtypeStruct((n_layers, m // seq, MLA_KV_DIM, seq), F32),
                   jax.ShapeDtypeStruct((m, 1024), BF16),
                   jax.ShapeDtypeStruct((m, 1024), BF16), jax.ShapeDtypeStruct((m // tm, hv, tm), BF16)],
        input_output_aliases=aliases,
        compiler_params=_cparams(("arbitrary",), VMEM_LIMIT),
        name="mla_pre",
    )(*args)


def _mla_attn_kernel(q_ref, k_ref, vt_ref, o_ref):
    t = MLA_TQ
    nq = q_ref.shape[0] // t
    krow = lax.broadcasted_iota(jnp.int32, (t, t), 0)
    qcol = lax.broadcasted_iota(jnp.int32, (t, t), 1)
    ones = jnp.ones((8, t), BF16)
    for qi in range(nq):
        outs = []
        for j in range(2):
            qh = q_ref[qi * t:(qi + 1) * t, j * LANES:(j + 1) * LANES]
            m = acc = None
            for kb in range(qi + 1):
                s = _dot_nt(k_ref[kb * t:(kb + 1) * t, j * LANES:(j + 1) * LANES], qh)
                if kb == qi:
                    s = jnp.where(krow <= qcol, s, NEG_INF)
                bm = jnp.max(s, axis=0, keepdims=True)
                m_new = bm if m is None else jnp.maximum(m, bm)
                p = jnp.exp(s - m_new).astype(BF16)
                vt1 = jnp.concatenate([vt_ref[kb, j * MLA_V:(j + 1) * MLA_V, :], ones], axis=0)
                pv = _dot(vt1, p)
                acc = pv if m is None else jnp.exp(m - m_new) * acc + pv
                m = m_new
            outs.append(acc[:MLA_V] / acc[MLA_V:MLA_V + 1])
        o_ref[qi * t:(qi + 1) * t, :] = jnp.concatenate(outs, axis=0).T.astype(BF16)


def _mla_attn(q, k, vt, batch, seq):
    nq = seq // MLA_TQ
    vt4 = vt.reshape(batch, nq, MLA_HEADS * MLA_V, MLA_TQ)
    return pl.pallas_call(
        _mla_attn_kernel,
        grid=(batch, MLA_HEADS // 2),
        in_specs=[pl.BlockSpec((seq, 256), lambda b, p: (b, p)),
                  pl.BlockSpec((seq, 256), lambda b, p: (b, p)),
                  pl.BlockSpec((None, nq, 2 * MLA_V, MLA_TQ), lambda b, p: (b, 0, p, 0))],
        out_specs=pl.BlockSpec((seq, LANES), lambda b, p: (b, p)),
        out_shape=jax.ShapeDtypeStruct((batch * seq, MLA_HEADS * MLA_V), BF16),
        compiler_params=_cparams(("parallel", "parallel"), VMEM_LIMIT),
        name="mla_attn",
    )(q, k, vt4)


GLA_TB = 256
GLA_SLOTS = 12
HK = GLA_HEADS * GLA_DK
HV = GLA_HEADS * GLA_DV


def _gla_constants():
    c = GLA_CHUNK
    t = np.arange(c)[:, None]
    s = np.arange(c)[None, :]
    slots = []
    zero = np.zeros((c, c), bool)
    for p in (1, 2, 3):
        slots.append((t // 16 == p) & (s < 16 * p))
    slots.append(zero)
    for p in (1, 2, 3):
        slots.append((t // 16 == s // 16) & ((t % 16) // 4 == p) & (s % 16 < 4 * p))
    slots.append(zero)
    for d in (1, 2, 3):
        slots.append((t == s + d) & (t // 4 == s // 4))
    slots.append(t == s)
    mask = np.concatenate(slots, axis=1).astype(np.float32)
    assert (sum(slots) == (t >= s)).all()
    mask = np.tile(mask, (GLA_HEADS, 1))
    tri = (t >= s).astype(np.float32)
    hm = np.kron(np.eye(GLA_HEADS), np.ones((c, GLA_DK))).astype(np.float32)
    bd = np.kron(np.eye(GLA_HEADS), np.ones((GLA_DV, GLA_DK))).astype(np.float32)
    return (jnp.asarray(tri, BF16), jnp.asarray(hm, F32), jnp.asarray(mask, F32), jnp.asarray(bd, F32))


def _log_sigmoid(x):
    return jnp.minimum(x, 0.0) - jnp.log1p(jnp.exp(-jnp.abs(x)))


def _split3(x):
    hi = x.astype(BF16)
    r1 = x - hi.astype(F32)
    mid = r1.astype(BF16)
    lo = (r1 - mid.astype(F32)).astype(BF16)
    return hi, mid, lo


def _bc_rows(row, n):
    return jnp.broadcast_to(row, (n, row.shape[1]))


def _gla_kernel(blk_ref, q_ref, v_ref, r_ref, k_ref, wg_ref, bg_ref, gn_ref, tri_ref, hm_ref, mask_ref, bd_ref,
                o_ref, st_ref, st_scr):
    c = GLA_CHUNK

    @pl.when(pl.program_id(1) == 0)
    def _():
        st_scr[...] = jnp.zeros_like(st_scr)

    rows = lax.broadcasted_iota(jnp.int32, (c, HK), 0)
    rm4 = rows & 3

    def chunk(ci, st):
        sl = slice(ci * c, (ci + 1) * c)
        pre = _dot(blk_ref[sl, :].astype(BF16), wg_ref[...]) + bg_ref[...]
        g = _log_sigmoid(pre) * (1.0 / GLA_TAU)
        cum3 = _dot(tri_ref[...], jnp.concatenate(_split3(g), axis=1))
        cum = cum3[:, :HK] + cum3[:, HK:2 * HK] + cum3[:, 2 * HK:]
        q = q_ref[sl, :]
        k = k_ref[sl, :]
        vb = v_ref[sl, :].astype(BF16)

        def kvar(e):
            return (k * jnp.exp(jnp.minimum(e, 0.0))).astype(BF16)

        zeros_k = jnp.zeros((c, HK), BF16)
        bnd = [cum[16 * p - 1:16 * p, :] for p in (1, 2, 3)]
        base1 = jnp.concatenate([jnp.zeros((16, HK), F32)] + [_bc_rows(b, 16) for b in bnd], axis=0)
        q1 = q * jnp.exp(cum - base1)
        k1 = jnp.concatenate([kvar(_bc_rows(b, c) - cum) for b in bnd] + [zeros_k], axis=0)
        sh = [pltpu.roll(cum, d, axis=0) for d in (1, 2, 3, 4)]
        base2 = jnp.where(rm4 == 0, sh[0], jnp.where(rm4 == 1, sh[1], jnp.where(rm4 == 2, sh[2], sh[3])))
        base2 = jnp.where(rows < 4, 0.0, base2)
        q2 = q * jnp.exp(cum - base2)
        k2 = []
        for p in (1, 2, 3):
            bk = jnp.concatenate([_bc_rows(cum[16 * i + 4 * p - 1:16 * i + 4 * p, :], 16) for i in range(4)], axis=0)
            k2.append(kvar(bk - cum))
        k2 = jnp.concatenate(k2 + [zeros_k], axis=0)
        k3 = jnp.concatenate([kvar(pltpu.roll(cum, c - d, axis=0) - cum) for d in (1, 2, 3)] + [k.astype(BF16)],
                             axis=0)
        hm = hm_ref[...]

        def stack(x):
            return (jnp.concatenate([x] * GLA_HEADS, axis=0) * hm).astype(BF16)

        p_all = jnp.concatenate([_dot_nt(stack(q1), k1), _dot_nt(stack(q2), k2), _dot_nt(stack(q), k3)], axis=1)
        pm = (p_all * mask_ref[...]).astype(BF16)
        o = _dot_nt((q * jnp.exp(cum)).astype(BF16), st.astype(BF16))
        intra = []
        for h in range(GLA_HEADS):
            vh = vb[:, h * GLA_DV:(h + 1) * GLA_DV]
            intra.append(_dot(pm[h * c:(h + 1) * c, :], jnp.concatenate([vh] * GLA_SLOTS, axis=0)))
        o = o + jnp.concatenate(intra, axis=1)
        last = cum[c - 1:c, :]
        st = st * jnp.exp(last) + _dot_tn(vb, kvar(_bc_rows(last, c) - cum)) * bd_ref[...]
        r = r_ref[sl, :]
        outs = []
        for h in range(GLA_HEADS):
            lanes = slice(h * GLA_DV, (h + 1) * GLA_DV)
            outs.append(_rms_norm(o[:, lanes], gn_ref[...]) * _silu(r[:, lanes]))
        o_ref[sl, :] = jnp.concatenate(outs, axis=1).astype(BF16)
        return st

    st = st_scr[...]
    for ci in range(GLA_TB // c):
        st = chunk(ci, st)
    st_scr[...] = st

    @pl.when(pl.program_id(1) == pl.num_programs(1) - 1)
    def _():
        for h in range(GLA_HEADS):
            blk = st_scr[h * GLA_DV:(h + 1) * GLA_DV, :]
            st_ref[0, h] = blk.T[h * GLA_DK:(h + 1) * GLA_DK, :]


def _gla(h, wg, bg, gn, consts, batch, seq):
    nt = seq // GLA_TB
    tri, hm, mask, bd = consts
    rowblk = lambda w, cb: pl.BlockSpec((GLA_TB, w), lambda b, t: (b * nt + t, cb))
    return pl.pallas_call(
        _gla_kernel,
        grid=(batch, nt),
        in_specs=[rowblk(LANES, 5), rowblk(HK, 3), rowblk(HV, 2), rowblk(HV, 3), rowblk(HK, 8),
                  _resident(wg.shape), _resident(bg.shape), _resident(gn.shape),
                  _resident(tri.shape), _resident(hm.shape), _resident(mask.shape), _resident(bd.shape)],
        out_specs=[pl.BlockSpec((GLA_TB, HV), lambda b, t: (b * nt + t, 0)),
                   pl.BlockSpec((1, GLA_HEADS, GLA_DK, GLA_DV), lambda b, t: (b, 0, 0, 0))],
        out_shape=[jax.ShapeDtypeStruct((batch * seq, HV), BF16),
                   jax.ShapeDtypeStruct((batch, GLA_HEADS, GLA_DK, GLA_DV), F32)],
        scratch_shapes=[pltpu.VMEM((HV, HK), F32)],
        compiler_params=_cparams(("parallel", "arbitrary"), VMEM_LIMIT),
        name="gla",
    )(h, h, h, h, h, wg, bg, gn, tri, hm, mask, bd)


def _alibi_slope(group, head):
    return 2.0 ** (-8.0 * (group * DIL_HPG + head + 1) / (DIL_GROUPS * DIL_HPG))


DIL_TM = 512
QKV_G = 3 * MIX_C


def _proj_dil_kernel(x_ref, w_ref, *rest, n_prev, tiles):
    p_refs = rest[n_prev:n_prev + DIL_GROUPS]
    r1_ref, r2_ref, r3_ref, acc_ref = rest[n_prev + DIL_GROUPS:]
    t = pl.program_id(0) % tiles
    res = _dot(x_ref[...].astype(BF16), w_ref[...])

    def put_rows(g, lo, width):
        for which in range(2):
            c0 = g * QKV_G + (1 + which) * MIX_C
            p_refs[g][which] = res[lo:lo + width, c0:c0 + MIX_C].T.reshape(DIL_HPG, DIL_HD, width)

    put_rows(2, 0, DIL_TM)
    pl.when(t >= tiles - DIL_PAIRS[1][0] // DIL_TM)(lambda: put_rows(1, 0, DIL_TM))
    pl.when(t == tiles - 1)(lambda: put_rows(0, DIL_TM - DIL_PAIRS[0][0], DIL_PAIRS[0][0]))
    r1_ref[...] = res[:, :QKV_G].astype(BF16)
    nb = QKV_G // LANES
    for c in range(2 * nb):
        acc_ref[c] = res[:, QKV_G + c * LANES:QKV_G + (c + 1) * LANES]
    for g, r_ref in ((1, r2_ref), (2, r3_ref)):
        dil = DIL_PAIRS[g][1]
        for r in range(dil):
            for c in range(nb):
                r_ref[r, :, c * LANES:(c + 1) * LANES] = (
                    acc_ref[(g - 1) * nb + c, pl.ds(r, DIL_TM // dil, stride=dil), :].astype(BF16))


def _proj_dil(x, w, seq, prev, layer, n_layers):
    m, k = x.shape
    tm = DIL_TM
    nt = m // tm
    tiles = seq // tm
    res = lambda dil: ((m // seq, dil, seq // dil, QKV_G),
                       pl.BlockSpec((None, dil, tm // dil, QKV_G), lambda i: (i // tiles, 0, i % tiles, 0)))
    (s2, b2), (s3, b3) = res(DIL_PAIRS[1][1]), res(DIL_PAIRS[2][1])
    p_shapes, p_specs = [], []
    for window, _ in DIL_PAIRS:
        keep = min(window, seq)
        width = min(keep, tm)
        first = tiles - max(keep // tm, 1)
        p_shapes.append(jax.ShapeDtypeStruct((n_layers, m // seq, 2, DIL_HPG, DIL_HD, keep), F32))
        p_specs.append(pl.BlockSpec(
            (None, None, 2, DIL_HPG, DIL_HD, width),
            lambda i, first=first: (layer, i // tiles, 0, 0, 0, jnp.maximum(i % tiles - first, 0))))
    in_specs = [pl.BlockSpec((tm, k), lambda i: (i, 0)), _resident(w.shape)]
    args = [x, w]
    aliases = {}
    if prev is not None:
        in_specs += [pl.BlockSpec(memory_space=pl.ANY)] * DIL_GROUPS
        aliases = {len(args) + g: g for g in range(DIL_GROUPS)}
        args += list(prev)
    outs = pl.pallas_call(
        functools.partial(_proj_dil_kernel, n_prev=len(aliases), tiles=tiles),
        grid=(nt,),
        in_specs=in_specs,
        out_specs=p_specs + [pl.BlockSpec((tm, QKV_G), lambda i: (i, 0)), b2, b3],
        out_shape=p_shapes + [jax.ShapeDtypeStruct((m, QKV_G), BF16),
                              jax.ShapeDtypeStruct(s2, BF16), jax.ShapeDtypeStruct(s3, BF16)],
        scratch_shapes=[pltpu.VMEM((2 * QKV_G // LANES, tm, LANES), F32)],
        input_output_aliases=aliases,
        compiler_params=_cparams(("arbitrary",), VMEM_LIMIT),
        name="proj_dil",
    )(*args)
    return outs[:DIL_GROUPS], outs[DIL_GROUPS:]


DIL_QB = 4


def _dil_attn_kernel(q_ref, k_ref, v_ref, *rest, group, dil, mode):
    n = DIL_STEPS
    if mode == "chain":
        kp_ref, vp_ref, o_ref, lse_ref = rest
    else:
        o_ref, lse_ref = rest
    nk = 2 * n
    qi = lax.broadcasted_iota(jnp.int32, (n, nk), 0)
    ki = lax.broadcasted_iota(jnp.int32, (n, nk), 1)
    steps = qi + n - ki
    band = (steps >= 0) & (steps <= n)
    dist = (steps * dil).astype(F32)
    bias = [jnp.where(band, -_alibi_slope(group, h) * dist, NEG_INF) for h in range(DIL_HPG)]
    lane = lax.broadcasted_iota(jnp.int32, (n, LANES), 1)
    low = lane < DIL_HD
    for blk in range(DIL_QB):
        cur = slice(blk * n, (blk + 1) * n)
        if mode == "single":
            q = q_ref[blk]
            k = jnp.concatenate([k_ref[blk], k_ref[blk]], axis=0)
            v = jnp.concatenate([v_ref[blk], v_ref[blk]], axis=0)
            own = ki >= n
        else:
            q = q_ref[cur, :]
            if blk > 0:
                k, v = k_ref[(blk - 1) * n:(blk + 1) * n, :], v_ref[(blk - 1) * n:(blk + 1) * n, :]
                own = None
            elif mode == "chain":
                k = jnp.concatenate([kp_ref[...], k_ref[cur, :]], axis=0)
                v = jnp.concatenate([vp_ref[...], v_ref[cur, :]], axis=0)
                own = (ki >= n) | (pl.program_id(1) > 0)
            else:
                k = jnp.concatenate([k_ref[cur, :], k_ref[cur, :]], axis=0)
                v = jnp.concatenate([v_ref[cur, :], v_ref[cur, :]], axis=0)
                own = ki >= n
        o_all = []
        lse = jnp.zeros((n, LANES), F32)
        for j in range(DIL_HPG // 2):
            cols = slice(j * LANES, (j + 1) * LANES)
            q2, k2, v2 = q[:, cols], k[:, cols], v[:, cols]
            outs = []
            for par in range(2):
                qh = jnp.where(low if par == 0 else ~low, q2, jnp.zeros_like(q2))
                s = _dot_nt(qh, k2) + bias[2 * j + par]
                if own is not None:
                    s = jnp.where(own, s, NEG_INF)
                m = jnp.max(s, axis=-1, keepdims=True)
                p = jnp.exp(s - m)
                l = jnp.sum(p, axis=-1, keepdims=True)
                outs.append(_dot(p.astype(BF16), v2) / l)
                lse = jnp.where(lane == 2 * j + par, m + jnp.log(l), lse)
            o_all.append(jnp.where(low, outs[0], outs[1]))
        dst = (blk,) if mode == "single" else (cur, slice(None))
        o_ref[dst] = jnp.concatenate(o_all, axis=1).astype(BF16)
        lse_ref[dst] = lse


def _dil_attn(qkv_r, group):
    n = DIL_STEPS
    batch, dil, ls, _ = qkv_r.shape
    rows = DIL_QB * n
    if ls == n:
        mode, steps = "single", dil // DIL_QB
        spec = lambda c, w=MIX_C: pl.BlockSpec((None, DIL_QB, n, w), lambda b, s: (b, s, 0, c))
        extra = []
    elif ls == rows:
        mode, steps = "whole", dil
        spec = lambda c, w=MIX_C: pl.BlockSpec((None, None, rows, w), lambda b, s: (b, s, 0, c))
        extra = []
    else:
        assert dil == 1 and ls % rows == 0
        mode, steps = "chain", ls // rows
        spec = lambda c, w=MIX_C: pl.BlockSpec((None, None, rows, w), lambda b, s: (b, 0, s, c))
        extra = [pl.BlockSpec((None, None, n, MIX_C), lambda b, s, c=c: (b, 0, jnp.maximum(s * DIL_QB - 1, 0), c))
                 for c in (1, 2)]
    in_specs = [spec(0), spec(1), spec(2)] + extra
    return pl.pallas_call(
        functools.partial(_dil_attn_kernel, group=group, dil=DIL_PAIRS[group][1], mode=mode),
        grid=(batch, steps),
        in_specs=in_specs,
        out_specs=[spec(0), spec(0, LANES)],
        out_shape=[jax.ShapeDtypeStruct((batch, dil, ls, MIX_C), BF16),
                   jax.ShapeDtypeStruct((batch, dil, ls, LANES), F32)],
        compiler_params=_cparams(("parallel", "arbitrary"), VMEM_LIMIT),
        name="dil_attn%d" % group,
    )(*([qkv_r] * len(in_specs)))


def _dil_mix_ln_kernel(o1, o2, o3, l1, l2, l3, e_ref, w_ref, x_ref, g_ref, b_ref, o_ref, so2, so3, sl2, sl3):
    for src, dst, dil in ((o2, so2, DIL_PAIRS[1][1]), (l2, sl2, DIL_PAIRS[1][1]),
                          (o3, so3, DIL_PAIRS[2][1]), (l3, sl3, DIL_PAIRS[2][1])):
        for r in range(dil):
            for c in range(dst.shape[0]):
                dst[c, pl.ds(r, DIL_TM // dil, stride=dil), :] = src[r, :, c * LANES:(c + 1) * LANES].astype(F32)
    whole = lambda s: jnp.concatenate([s[c] for c in range(s.shape[0])], axis=1)
    os_ = [o1[...].astype(F32), whole(so2), whole(so3)]
    ls = [l1[...], sl2[0], sl3[0]]
    mx = jnp.maximum(jnp.maximum(ls[0], ls[1]), ls[2])
    es = [jnp.exp(l - mx) for l in ls]
    den = es[0] + es[1] + es[2]

    def spread(wt):
        hi = wt.astype(BF16)
        lo = (wt - hi.astype(F32)).astype(BF16)
        return _dot(hi, e_ref[...]) + _dot(lo, e_ref[...])

    mix = sum(spread(e / den) * o for e, o in zip(es, os_))
    z = DN_ALPHA * x_ref[...] + _dot(mix.astype(BF16), w_ref[...])
    o_ref[...] = _layer_norm(z, g_ref[...], b_ref[...])


def _dil_mix_ln(os_, lses, w, x, g, b):
    m = x.shape[0]
    tm = DIL_TM
    nt = m // tm
    tiles = nt // os_[0].shape[0]
    row = lambda wd: pl.BlockSpec((tm, wd), lambda i: (i, 0))
    res = lambda a: pl.BlockSpec((None, a.shape[1], tm // a.shape[1], a.shape[3]),
                                 lambda i: (i // tiles, 0, i % tiles, 0))
    flat = lambda a: a.reshape(m, a.shape[3])
    spread = np.kron(np.eye(DIL_HPG), np.ones((1, DIL_HD)))
    spread = jnp.asarray(np.concatenate([spread, np.zeros((LANES - DIL_HPG, MIX_C))]), BF16)
    blocks = lambda wd: pltpu.VMEM((wd // LANES, tm, LANES), F32)
    return pl.pallas_call(
        _dil_mix_ln_kernel,
        grid=(nt,),
        in_specs=[row(MIX_C), res(os_[1]), res(os_[2]), row(LANES), res(lses[1]), res(lses[2]), _resident(spread.shape),
                  _resident(w.shape), row(D_MODEL), _resident(g.shape), _resident(b.shape)],
        out_specs=row(D_MODEL),
        out_shape=jax.ShapeDtypeStruct((m, D_MODEL), F32),
        scratch_shapes=[blocks(MIX_C), blocks(MIX_C), blocks(LANES), blocks(LANES)],
        compiler_params=_cparams(("parallel",), VMEM_LIMIT),
        name="dil_mix_ln",
    )(flat(os_[0]), os_[1], os_[2], flat(lses[0]), lses[1], lses[2], spread, w, x, g, b)


def _mla_pre_s_kernel(h_ref, qn_ref, kvn_ref, wqn_ref, wqp_ref, wukt_ref, ck_ref, sk_ref, rows_ref, q_ref):
    h = h_ref[...]
    qn = _rms_norm(h[:, :MLA_Q_LORA], qn_ref[...]).astype(BF16)
    qnope = _dot(qn, wqn_ref[...]).astype(BF16)
    qp = _dot(qn, wqp_ref[...])
    cq = jnp.concatenate([ck_ref[...]] * MLA_HEADS, axis=1)
    sq = jnp.concatenate([sk_ref[...]] * MLA_HEADS, axis=1)
    qp = qp * cq + _roll_lanes_left(qp, 32) * sq
    ckv = _rms_norm(h[:, MLA_Q_LORA:MLA_Q_LORA + MLA_KV_LORA], kvn_ref[...])
    blk = h[:, 640:768]
    kpe = blk * ck_ref[...] + _roll_lanes_left(blk, 32) * sk_ref[...]
    rows_ref[:, :MLA_KV_LORA] = ckv
    rows_ref[:, MLA_KV_LORA:] = kpe[:, :MLA_ROPE]
    for hd in range(MLA_HEADS):
        cols = slice(hd * LANES, (hd + 1) * LANES)
        q_ref[hd, :, :MLA_KV_LORA] = _dot(qnope[:, cols], wukt_ref[hd])
        q_ref[hd, :, MLA_KV_LORA:] = qp[:, hd * LANES:hd * LANES + MLA_ROPE]


def _mla_pre_s(h, qn, kvn, wqn, wqp, wukt, tabs):
    m = h.shape[0]
    full = lambda a: _resident(a.shape)
    return pl.pallas_call(
        _mla_pre_s_kernel,
        grid=(1,),
        in_specs=[pl.BlockSpec((m, 768), lambda i: (0, 0)), full(qn), full(kvn), full(wqn), full(wqp), full(wukt)]
        + [full(t) for t in tabs[2:]],
        out_specs=[pl.BlockSpec((m, MLA_KV_DIM), lambda i: (0, 0)),
                   pl.BlockSpec((MLA_HEADS, m, MLA_KV_DIM), lambda i: (0, 0, 0))],
        out_shape=[jax.ShapeDtypeStruct((m, MLA_KV_DIM), F32), jax.ShapeDtypeStruct((MLA_HEADS, m, MLA_KV_DIM), F32)],
        compiler_params=_cparams(("arbitrary",), VMEM_LIMIT),
        name="mla_pre_s",
    )(h, qn, kvn, wqn, wqp, wukt, *tabs[2:])


DEC_PAGES = 64


def _mla_dec_kernel(pt_ref, q_ref, new_ref, *rest):
    pages = rest[:DEC_PAGES]
    o_ref, m_scr, l_scr, acc_scr = rest[DEC_PAGES:]
    step = pl.program_id(1)
    q = q_ref[0]
    qb = q.astype(BF16)

    @pl.when(step == 0)
    def _():
        new = new_ref[0]
        m_scr[...] = jnp.sum(q * new, axis=-1, keepdims=True) * MLA_SCALE
        l_scr[...] = jnp.ones_like(l_scr)
        acc_scr[...] = jnp.broadcast_to(new[:, :MLA_KV_LORA], acc_scr.shape)

    kvt = jnp.concatenate([p[...].astype(BF16) for p in pages], axis=1)
    s = _dot(qb, kvt) * MLA_SCALE
    m_old = m_scr[...]
    m_new = jnp.maximum(m_old, jnp.max(s, axis=-1, keepdims=True))
    alpha = jnp.exp(m_old - m_new)
    p = jnp.exp(s - m_new)
    m_scr[...] = m_new
    l_scr[...] = alpha * l_scr[...] + jnp.sum(p, axis=-1, keepdims=True)
    acc_scr[...] = alpha * acc_scr[...] + _dot_nt(p.astype(BF16), kvt[:MLA_KV_LORA, :])

    @pl.when(step == pl.num_programs(1) - 1)
    def _():
        o_ref[0] = acc_scr[...] / l_scr[...]


def _mla_dec(page_table, q_abs, rows_new, cache_t, layer):
    batch = q_abs.shape[0]
    n_pages = page_table.shape[1]
    steps = n_pages // DEC_PAGES
    pt = page_table.reshape(-1)

    def page_spec(j):
        return pl.BlockSpec((None, None, MLA_KV_DIM, PAGE_SIZE),
                            lambda b, s, pt_ref: (layer, pt_ref[b * n_pages + s * DEC_PAGES + j], 0, 0))

    grid_spec = pltpu.PrefetchScalarGridSpec(
        num_scalar_prefetch=1,
        grid=(batch, steps),
        in_specs=[pl.BlockSpec((1, MLA_HEADS, MLA_KV_DIM), lambda b, s, pt_ref: (b, 0, 0)),
                  pl.BlockSpec((1, 1, MLA_KV_DIM), lambda b, s, pt_ref: (b, 0, 0))]
        + [page_spec(j) for j in range(DEC_PAGES)],
        out_specs=pl.BlockSpec((1, MLA_HEADS, MLA_KV_LORA), lambda b, s, pt_ref: (b, 0, 0)),
        scratch_shapes=[pltpu.VMEM((MLA_HEADS, 1), F32), pltpu.VMEM((MLA_HEADS, 1), F32),
                        pltpu.VMEM((MLA_HEADS, MLA_KV_LORA), F32)],
    )
    return pl.pallas_call(
        _mla_dec_kernel,
        grid_spec=grid_spec,
        out_shape=jax.ShapeDtypeStruct((batch, MLA_HEADS, MLA_KV_LORA), F32),
        compiler_params=_cparams(("parallel", "arbitrary"), VMEM_LIMIT),
        name="mla_dec",
    )(pt, q_abs, rows_new.reshape(batch, 1, MLA_KV_DIM), *([cache_t] * DEC_PAGES))


GLA_DEC_B = 8


def _gla_dec_kernel(blk_ref, q_ref, v_ref, r_ref, k_ref, s_ref, wg_ref, bg_ref, gn_ref, o_ref, so_ref):
    pre = _dot(blk_ref[...].astype(BF16), wg_ref[...]) + bg_ref[...]
    a_all = jnp.exp(_log_sigmoid(pre) * (1.0 / GLA_TAU))
    eye = (lax.broadcasted_iota(jnp.int32, (GLA_DK, GLA_DK), 0)
           == lax.broadcasted_iota(jnp.int32, (GLA_DK, GLA_DK), 1))
    diag = lambda row: jnp.where(eye, jnp.broadcast_to(row, (GLA_DK, GLA_DK)), 0.0)
    for t in range(GLA_DEC_B):
        tok = slice(t, t + 1)
        a, k, v, r = a_all[tok], k_ref[tok, :], v_ref[tok, :], r_ref[tok, :]
        q = q_ref[tok, :]
        outs = []
        for h in range(GLA_HEADS):
            kl = slice(h * GLA_DK, (h + 1) * GLA_DK)
            vl = slice(h * GLA_DV, (h + 1) * GLA_DV)
            lhs = jnp.concatenate([diag(a[:, kl]), diag(k[:, kl])], axis=1)
            rhs = jnp.concatenate([s_ref[t, h], jnp.broadcast_to(v[:, vl], (GLA_DK, GLA_DV))], axis=0)
            s_new = jnp.dot(lhs, rhs, preferred_element_type=F32, precision=lax.Precision.HIGHEST)
            so_ref[t, h] = s_new
            o = jnp.dot(jnp.broadcast_to(q[:, kl], (8, GLA_DK)), s_new, preferred_element_type=F32,
                        precision=lax.Precision.HIGHEST)[:1]
            outs.append(_rms_norm(o, gn_ref[...]) * _silu(r[:, vl]))
        o_ref[tok, :] = jnp.concatenate(outs, axis=1)


def _gla_dec(h, state, layer, wg, bg, gn):
    batch = h.shape[0]
    tb = GLA_DEC_B
    rowblk = lambda w, cb: pl.BlockSpec((tb, w), lambda b: (b, cb))
    st_in = pl.BlockSpec((None, tb, GLA_HEADS, GLA_DK, GLA_DV), lambda b: (layer, b, 0, 0, 0))
    st = pl.BlockSpec((tb, GLA_HEADS, GLA_DK, GLA_DV), lambda b: (b, 0, 0, 0))
    return pl.pallas_call(
        _gla_dec_kernel,
        grid=(batch // tb,),
        in_specs=[rowblk(LANES, 5), rowblk(HK, 3), rowblk(HV, 2), rowblk(HV, 3), rowblk(HK, 8), st_in,
                  _resident(wg.shape), _resident(bg.shape), _resident(gn.shape)],
        out_specs=[pl.BlockSpec((tb, HV), lambda b: (b, 0)), st],
        out_shape=[jax.ShapeDtypeStruct((batch, HV), F32), jax.ShapeDtypeStruct(state.shape[1:], F32)],
        compiler_params=_cparams(("parallel",), VMEM_LIMIT),
        name="gla_dec",
    )(h, h, h, h, h, state, wg, bg, gn)


def _dil_dec_kernel(x_ref, c1_ref, c2_ref, c3_ref, o_ref):
    heads = range(DIL_HPG)
    hcol = lax.broadcasted_iota(jnp.int32, (DIL_HPG, 1), 0).astype(F32)
    outs, lses = [], []
    for g, c_ref in enumerate((c1_ref, c2_ref, c3_ref)):
        window, dil = DIL_PAIRS[g]
        pos = lax.broadcasted_iota(jnp.int32, (1, window), 1)
        q, k_new, v_new = (x_ref[0, 3 * g + w] for w in range(3))
        s = jnp.concatenate([jnp.sum(c_ref[0, h] * q[:, h:h + 1], axis=0, keepdims=True) for h in heads], axis=0)
        slope = jnp.exp((g * DIL_HPG + hcol + 1.0) * (-8.0 * math.log(2.0) / (DIL_GROUPS * DIL_HPG)))
        s = s - slope * (window - pos).astype(F32)
        s = jnp.where((pos & (dil - 1)) == 0, s, NEG_INF)
        qk = jnp.sum(q * k_new, axis=0, keepdims=True)
        s_new = jnp.concatenate([qk[:, h:h + 1] for h in heads], axis=0)
        m = jnp.maximum(jnp.max(s, axis=1, keepdims=True), s_new)
        p = jnp.exp(s - m)
        p_new = jnp.exp(s_new - m)
        l = jnp.sum(p, axis=1, keepdims=True) + p_new
        w = p / l
        w_new = p_new / l
        outs.append([jnp.sum(c_ref[1, h] * w[h:h + 1, :], axis=1, keepdims=True)
                     + w_new[h:h + 1, :] * v_new[:, h:h + 1] for h in heads])
        lses.append(m + jnp.log(l))
    mx = jnp.maximum(jnp.maximum(lses[0], lses[1]), lses[2])
    es = [jnp.exp(l - mx) for l in lses]
    den = es[0] + es[1] + es[2]
    ws = [e / den for e in es]
    o_ref[0] = jnp.concatenate(
        [sum(ws[g][h:h + 1, :] * outs[g][h] for g in range(DIL_GROUPS)) for h in heads], axis=1)


def _dil_dec(qkv_t, caches_t, layer):
    batch = qkv_t.shape[0]
    specs = [pl.BlockSpec((1, 3 * DIL_GROUPS, DIL_HD, DIL_HPG), lambda b: (b, 0, 0, 0))]
    for (window, _), c in zip(DIL_PAIRS, caches_t):
        assert c.shape[-1] == window
        specs.append(pl.BlockSpec((None, None, 2, DIL_HPG, DIL_HD, window), lambda b: (layer, b, 0, 0, 0, 0)))
    return pl.pallas_call(
        _dil_dec_kernel,
        grid=(batch,),
        in_specs=specs,
        out_specs=pl.BlockSpec((1, DIL_HD, DIL_HPG), lambda b: (b, 0, 0)),
        out_shape=jax.ShapeDtypeStruct((batch, DIL_HD, DIL_HPG), F32),
        compiler_params=_cparams(("parallel",), VMEM_LIMIT),
        name="dil_dec",
    )(qkv_t, *caches_t)


def _even_in_cols():
    o_cq, o_kv, o_gq, o_gk, o_gv, o_lr, o_gr = 0, 384, 672, 928, 1184, 1696, 1712
    r = np.arange
    kpe = o_kv + MLA_KV_LORA
    idx = np.concatenate([r(o_cq, o_cq + 384), r(o_kv, o_kv + 256), r(kpe, kpe + 32), r(kpe + 16, kpe + 32),
                          r(kpe, kpe + 16), r(o_lr, o_lr + 16), np.full(48, -1), r(o_gq, o_gq + 256),
                          r(o_gv, o_gv + 512), r(o_gr, o_gr + 512), r(o_gk, o_gk + 256)])
    assert idx.shape[0] == IN_A_PAD
    return idx


def _gather_cols(w, idx):
    idx = np.asarray(idx)
    a, b = idx[:-1], idx[1:]
    same_run = ((a >= 0) & (b >= 0) & (b - a == 1)) | ((a < 0) & (b < 0))
    cuts = np.flatnonzero(~same_run) + 1
    parts = []
    for run in np.split(idx, cuts):
        if run[0] < 0:
            parts.append(jnp.zeros((w.shape[0], run.size), w.dtype))
        else:
            parts.append(w[:, int(run[0]):int(run[-1]) + 1])
    return jnp.concatenate(parts, axis=1)


def _uq_cols(kind):
    idx = []
    for h in range(MLA_HEADS):
        b = h * (MLA_NOPE + MLA_ROPE)
        nope = np.arange(b, b + 64)
        x1, x2 = np.arange(b + 64, b + 80), np.arange(b + 80, b + 96)
        if kind == "full":
            idx += [nope, x1, x2, x2, x1]
        elif kind == "nope":
            idx += [nope, np.full(64, -1)]
        else:
            idx += [x1, x2, x2, x1, np.full(64, -1)]
    return np.concatenate(idx)


def _odd_in_cols():
    blk = lambda which, g: np.arange((which * DIL_GROUPS + g) * MIX_C, (which * DIL_GROUPS + g + 1) * MIX_C)
    return np.concatenate([blk(which, g) for g in range(DIL_GROUPS) for which in range(3)])


TM_DENSE = 512
TM_WIDE = 256


def kernel(x_prompt, x_sample, cache_mla, state_gla, cache_dil_w128, cache_dil_w512, cache_dil_w2048, page_table,
           w_in_a, mla_q_norm, mla_w_uq, mla_kv_norm, mla_w_uk, mla_w_uv, gla_w_gate2, gla_b_gate, gla_norm, w_out_a,
           w_in_c, w_out_c, ffn_w_in, ffn_w_out, ln_g, ln_b):
    batch, seq, _ = x_prompt.shape
    dbatch = x_sample.shape[0]
    assert seq % MLA_TQ == 0 and seq % DIL_TM == 0 and seq >= DIL_PAIRS[-1][0]
    n_even, n_odd = (DEPTH + 1) // 2, DEPTH // 2
    cache_mla_t = jnp.transpose(cache_mla, (0, 1, 3, 2))
    dil_caches_t = [jnp.transpose(c, (0, 1, 3, 4, 5, 2)) for c in (cache_dil_w128, cache_dil_w512, cache_dil_w2048)]
    gla_consts = _gla_constants()
    tabs_p = _rope_tables(jnp.arange(seq, dtype=jnp.int32))
    tabs_s = _rope_tables(jnp.full((dbatch,), PAST_LEN, jnp.int32))
    even_cols, odd_cols = _even_in_cols(), _odd_in_cols()
    even_scale = np.ones((1, IN_A_PAD), np.float32)
    even_scale[:, 768:768 + HK] = GLA_DK ** -0.5
    odd_scale = np.ones((1, IN_C), np.float32)
    for g in range(DIL_GROUPS):
        odd_scale[:, g * QKV_G:g * QKV_G + MIX_C] = DIL_HD ** -0.5
    even_scale, odd_scale = jnp.asarray(even_scale), jnp.asarray(odd_scale)
    row2 = lambda v: v.reshape(1, -1).astype(F32)

    xp = x_prompt.reshape(batch * seq, D_MODEL)
    xs = x_sample.reshape(dbatch, D_MODEL)
    mla_s, gla_p, gla_s = [], [], []
    rows_t, dil_t = None, None
    dil_s = [[] for _ in DIL_PAIRS]

    for layer in range(DEPTH):
        i = layer // 2
        g0, b0, g1, b1 = (row2(ln_g[layer, 0]), row2(ln_b[layer, 0]), row2(ln_g[layer, 1]), row2(ln_b[layer, 1]))
        if layer % 2 == 0:
            w_in = (_gather_cols(w_in_a[i], even_cols) * even_scale).astype(BF16)
            wq_full = _gather_cols(mla_w_uq[i], _uq_cols("full")).astype(BF16)
            wq_nope = _gather_cols(mla_w_uq[i], _uq_cols("nope")).astype(BF16)
            wq_rope = _gather_cols(mla_w_uq[i], _uq_cols("rope")).astype(BF16)
            wuk_pad = jnp.pad(mla_w_uk[i], ((0, 0), (0, 0), (0, LANES - MLA_NOPE))).reshape(MLA_KV_LORA, -1).astype(BF16)
            wuk_t = jnp.pad(jnp.transpose(mla_w_uk[i], (1, 2, 0)), ((0, 0), (0, LANES - MLA_NOPE), (0, 0))).astype(BF16)
            wuv_t = mla_w_uv[i].reshape(MLA_KV_LORA, -1).T.astype(BF16)
            wuv_bd = (jnp.eye(MLA_HEADS, dtype=F32)[:, None, :, None]
                      * jnp.transpose(mla_w_uv[i], (1, 0, 2))[:, :, None, :]).reshape(
                          MLA_HEADS * MLA_KV_LORA, MLA_HEADS * MLA_V).astype(BF16)
            wg = jnp.pad(gla_w_gate2[i], ((64, LANES - 64 - GLA_GATE_RANK), (0, 0))).astype(BF16)
            bg, gn = row2(gla_b_gate[i]), row2(gla_norm[i])
            qn, kvn = row2(mla_q_norm[i]), row2(mla_kv_norm[i])
            w_out = w_out_a[i].astype(BF16)

            h = _proj(xp, w_in, TM_DENSE)
            rows_t, qf, kf, vt = _mla_pre(h, qn, kvn, wq_full, wuk_pad, wuv_t, tabs_p, seq, rows_t, i, n_even)
            mla_out = _mla_attn(qf, kf, vt, batch, seq)
            gla_out, s_fin = _gla(h, wg, bg, gn, gla_consts, batch, seq)
            gla_p.append(s_fin)
            xp = _mix_ln([mla_out, gla_out], w_out, xp, g0, b0, TM_DENSE)

            h = _proj(xs, w_in, TM_DENSE)
            rows, q_abs = _mla_pre_s(h, qn, kvn, wq_nope, wq_rope, wuk_t, tabs_s)
            lat = _mla_dec(page_table, jnp.transpose(q_abs, (1, 0, 2)), rows, cache_mla_t, i)
            mla_out = _proj(lat.reshape(dbatch, MLA_HEADS * MLA_KV_LORA), wuv_bd, TM_DENSE)
            gla_out, s_fin = _gla_dec(h, state_gla, i, wg, bg, gn)
            mla_s.append(rows.reshape(dbatch, 1, MLA_KV_DIM))
            gla_s.append(s_fin)
            xs = _mix_ln([mla_out, gla_out], w_out, xs, g0, b0, TM_DENSE)
        else:
            w_in = (_gather_cols(w_in_c[i], odd_cols) * odd_scale).astype(BF16)
            w_out = w_out_c[i].astype(BF16)

            dil_t, (r1, r2, r3) = _proj_dil(xp, w_in, seq, dil_t, i, n_odd)
            res = [r1.reshape(batch, 1, seq, QKV_G), r2, r3]
            os_, lses = zip(*[_dil_attn(res[g], g) for g in range(DIL_GROUPS)])
            xp = _dil_mix_ln(os_, lses, w_out, xp, g0, b0)

            qkv = _proj(xs, w_in, TM_WIDE).reshape(dbatch, 3 * DIL_GROUPS, DIL_HPG, DIL_HD)
            mix = _dil_dec(jnp.transpose(qkv, (0, 1, 3, 2)), dil_caches_t, i)
            mix = jnp.transpose(mix, (0, 2, 1)).reshape(dbatch, MIX_C)
            for g in range(DIL_GROUPS):
                dil_s[g].append(qkv[:, 3 * g + 1:3 * g + 3].reshape(dbatch, 1, 2, DIL_HPG, DIL_HD))
            xs = _mix_ln([mix], w_out, xs, g0, b0, TM_DENSE)

        wi, wo = ffn_w_in[layer].astype(BF16), ffn_w_out[layer].astype(BF16)
        xp = _ffn(xp, wi, wo, g1, b1, TM_FFN)
        xs = _ffn(xs, wi, wo, g1, b1, TM_FFN)

    st = jnp.stack
    mla_p = jnp.transpose(rows_t, (0, 1, 3, 2))
    dil_p = [jnp.transpose(p, (0, 1, 5, 2, 3, 4)) for p in dil_t]
    return (xp.reshape(batch, seq, D_MODEL), xs.reshape(dbatch, 1, D_MODEL), mla_p, st(mla_s), st(gla_p), st(gla_s),
            dil_p[0], st(dil_s[0]), dil_p[1], st(dil_s[1]), dil_p[2], st(dil_s[2]))
```

```python
import functools
import math

import numpy as np
import jax
import jax.numpy as jnp
from jax import lax
from jax.experimental import pallas as pl
from jax.experimental.pallas import tpu as pltpu

F32 = jnp.float32
BF16 = jnp.bfloat16

D_MODEL = 1024
DEPTH = 4
PAST_LEN = 16384
PAGE_SIZE = 128
MLA_HEADS = 8
MLA_Q_LORA = 384
MLA_KV_LORA = 256
MLA_NOPE = 64
MLA_ROPE = 32
MLA_V = 64
MLA_KV_DIM = MLA_KV_LORA + MLA_ROPE
ROPE_BASE = 10000.0
GLA_HEADS = 4
GLA_DK = 64
GLA_DV = 128
GLA_GATE_RANK = 16
GLA_TAU = 16.0
GLA_CHUNK = 64
DIL_PAIRS = ((128, 1), (512, 4), (2048, 16))
DIL_GROUPS = 3
DIL_HPG = 8
DIL_HD = 64
DIL_STEPS = 128
D_FF = -(-8 * D_MODEL // (3 * 256)) * 256
DN_ALPHA = (2.0 * DEPTH) ** 0.25
IN_A_PAD = 2304
MIX_C = DIL_HPG * DIL_HD
IN_C = 3 * DIL_GROUPS * MIX_C
MLA_SCALE = (MLA_NOPE + MLA_ROPE) ** -0.5

V7X_VMEM_BYTES = 64 * 1024 * 1024
VMEM_LIMIT = 56 * 1024 * 1024
LANES = 128

NEG_INF = float("-inf")


def _cparams(sem, limit=None):
    return pltpu.CompilerParams(dimension_semantics=sem, vmem_limit_bytes=limit)


def _resident(shape):
    nd = len(shape)
    return pl.BlockSpec(shape, lambda *_: (0,) * nd, pipeline_mode=pl.Buffered(1))


def _layer_norm(z, g, b):
    mu = jnp.mean(z, axis=-1, keepdims=True)
    zc = z - mu
    var = jnp.mean(zc * zc, axis=-1, keepdims=True)
    return zc * lax.rsqrt(var + 1e-5) * g + b


def _rms_norm(z, g):
    return z * lax.rsqrt(jnp.mean(z * z, axis=-1, keepdims=True) + 1e-6) * g


def _silu(z):
    return z * (1.0 / (1.0 + jnp.exp(-z)))


def _dot(a, b):
    return jnp.dot(a, b, preferred_element_type=F32)


def _dot_nt(a, b):
    return lax.dot_general(a, b, (((1,), (1,)), ((), ())), preferred_element_type=F32)


def _dot_tn(a, b):
    return lax.dot_general(a, b, (((0,), (0,)), ((), ())), preferred_element_type=F32)


def _proj_kernel(x_ref, w_ref, o_ref):
    o_ref[...] = _dot(x_ref[...].astype(BF16), w_ref[...]).astype(o_ref.dtype)


def _proj(x, w, tm, out_dtype=F32):
    m, k = x.shape
    n = w.shape[1]
    tm = min(tm, m)
    return pl.pallas_call(
        _proj_kernel,
        grid=(m // tm,),
        in_specs=[pl.BlockSpec((tm, k), lambda i: (i, 0)), _resident((k, n))],
        out_specs=pl.BlockSpec((tm, n), lambda i: (i, 0)),
        out_shape=jax.ShapeDtypeStruct((m, n), out_dtype),
        compiler_params=_cparams(("parallel",), VMEM_LIMIT),
        name="proj",
    )(x, w)


def _mix_ln_kernel(*refs, n_parts):
    parts = refs[:n_parts]
    w_ref, x_ref, g_ref, b_ref, o_ref = refs[n_parts:]
    a = jnp.concatenate([p[...].astype(BF16) for p in parts], axis=-1) if n_parts > 1 else parts[0][...].astype(BF16)
    z = DN_ALPHA * x_ref[...] + _dot(a, w_ref[...])
    o_ref[...] = _layer_norm(z, g_ref[...], b_ref[...])


def _mix_ln(parts, w, x, g, b, tm):
    m = x.shape[0]
    tm = min(tm, m)
    kern = functools.partial(_mix_ln_kernel, n_parts=len(parts))
    return pl.pallas_call(
        kern,
        grid=(m // tm,),
        in_specs=[pl.BlockSpec((tm, p.shape[1]), lambda i: (i, 0)) for p in parts]
        + [_resident(w.shape), pl.BlockSpec((tm, D_MODEL), lambda i: (i, 0)), _resident(g.shape), _resident(b.shape)],
        out_specs=pl.BlockSpec((tm, D_MODEL), lambda i: (i, 0)),
        out_shape=jax.ShapeDtypeStruct((m, D_MODEL), F32),
        compiler_params=_cparams(("parallel",), VMEM_LIMIT),
        name="mix_ln",
    )(*parts, w, x, g, b)


FF_CHUNK = 256
TM_FFN = 1024


def _ffn_kernel(x_ref, wi_ref, wo_ref, g_ref, b_ref, o_ref, act_ref):
    x = x_ref[...]
    xb = x.astype(BF16)
    for c in range(D_FF // FF_CHUNK):
        lo = c * FF_CHUNK
        gate = _dot(xb, wi_ref[:, lo:lo + FF_CHUNK])
        up = _dot(xb, wi_ref[:, D_FF + lo:D_FF + lo + FF_CHUNK])
        act_ref[:, lo:lo + FF_CHUNK] = (_silu(gate) * up).astype(BF16)
    z = DN_ALPHA * x + _dot(act_ref[...], wo_ref[...])
    o_ref[...] = _layer_norm(z, g_ref[...], b_ref[...])


def _ffn(x, wi, wo, g, b, tm):
    m = x.shape[0]
    tm = min(tm, m)
    return pl.pallas_call(
        _ffn_kernel,
        grid=(m // tm,),
        in_specs=[pl.BlockSpec((tm, D_MODEL), lambda i: (i, 0)), _resident(wi.shape), _resident(wo.shape),
                  _resident(g.shape), _resident(b.shape)],
        out_specs=pl.BlockSpec((tm, D_MODEL), lambda i: (i, 0)),
        out_shape=jax.ShapeDtypeStruct((m, D_MODEL), F32),
        scratch_shapes=[pltpu.VMEM((tm, D_FF), BF16)],
        compiler_params=_cparams(("parallel",), VMEM_LIMIT),
        name="ffn",
    )(x, wi, wo, g, b)


def _rope_tables(pos):
    half = MLA_ROPE // 2
    inv = ROPE_BASE ** (-jnp.arange(half, dtype=F32) / half)
    ang = pos.astype(F32)[:, None] * inv[None, :]
    cos, sin = jnp.cos(ang), jnp.sin(ang)
    n = pos.shape[0]
    z = lambda w: jnp.zeros((n, w), F32)
    cc = jnp.concatenate([cos, cos], axis=1)
    ss = jnp.concatenate([-sin, sin], axis=1)
    cq = jnp.concatenate([jnp.ones((n, MLA_NOPE), F32), cc, z(32)], axis=1)
    sq = jnp.concatenate([z(MLA_NOPE), ss, z(32)], axis=1)
    ck = jnp.concatenate([cc, z(96)], axis=1)
    sk = jnp.concatenate([ss, z(96)], axis=1)
    return cq, sq, ck, sk


def _roll_lanes_left(x, k):
    return pltpu.roll(x, x.shape[-1] - k, axis=x.ndim - 1)


def _mla_pre_kernel(h_ref, qn_ref, kvn_ref, wq_ref, wuk_ref, wuvt_ref, cq_ref, sq_ref, ck_ref, sk_ref, *rest):
    rows_ref, q_ref, k_ref, vt_ref = rest[-4:]
    h = h_ref[...]
    qn = _rms_norm(h[:, :MLA_Q_LORA], qn_ref[...]).astype(BF16)
    q = _dot(qn, wq_ref[...])
    cq = jnp.concatenate([cq_ref[...]] * MLA_HEADS, axis=1)
    sq = jnp.concatenate([sq_ref[...]] * MLA_HEADS, axis=1)
    q_ref[...] = ((q * cq + _roll_lanes_left(q, 32) * sq) * MLA_SCALE).astype(BF16)
    ckv = _rms_norm(h[:, MLA_Q_LORA:MLA_Q_LORA + MLA_KV_LORA], kvn_ref[...])
    blk = h[:, 640:768]
    kpe = blk * ck_ref[...] + _roll_lanes_left(blk, 32) * sk_ref[...]
    rows_ref[:MLA_KV_LORA, :] = ckv.T
    rows_ref[MLA_KV_LORA:, :] = kpe.T[:MLA_ROPE, :]
    ckvb = ckv.astype(BF16)
    kpe_mid = pltpu.roll(kpe, 64, axis=1)
    k_ref[...] = (_dot(ckvb, wuk_ref[...]) + jnp.concatenate([kpe_mid] * MLA_HEADS, axis=1)).astype(BF16)
    vt_ref[0] = _dot_nt(wuvt_ref[...], ckvb).astype(BF16)


MLA_TQ = 512


def _mla_pre(h, qn, kvn, wq, wuk, wuvt, tabs, seq, rows_t, layer, n_layers):
    m = h.shape[0]
    tm = MLA_TQ
    nblk = seq // tm
    tab = pl.BlockSpec((tm, LANES), lambda i: (i % nblk, 0))
    row = lambda w: pl.BlockSpec((tm, w), lambda i: (i, 0))
    hv = MLA_HEADS * MLA_V
    in_specs = [pl.BlockSpec((tm, 768), lambda i: (i, 0)), _resident(qn.shape), _resident(kvn.shape),
                _resident(wq.shape), _resident(wuk.shape), _resident(wuvt.shape), tab, tab, tab, tab]
    args = [h, qn, kvn, wq, wuk, wuvt, *tabs]
    aliases = {}
    if rows_t is not None:
        in_specs.append(pl.BlockSpec(memory_space=pl.ANY))
        args.append(rows_t)
        aliases = {len(args) - 1: 0}
    return pl.pallas_call(
        _mla_pre_kernel,
        grid=(m // tm,),
        in_specs=in_specs,
        out_specs=[pl.BlockSpec((None, None, MLA_KV_DIM, tm), lambda i: (layer, i // nblk, 0, i % nblk)),
                   row(1024), row(1024), pl.BlockSpec((1, hv, tm), lambda i: (i, 0, 0))],
        out_shape=[jax.ShapeDtypeStruct((n_layers, m // seq, MLA_KV_DIM, seq), F32),
                   jax.ShapeDtypeStruct((m, 1024), BF16),
                   jax.ShapeDtypeStruct((m, 1024), BF16), jax.ShapeDtypeStruct((m // tm, hv, tm), BF16)],
        input_output_aliases=aliases,
        compiler_params=_cparams(("arbitrary",), VMEM_LIMIT),
        name="mla_pre",
    )(*args)


def _mla_attn_kernel(q_ref, k_ref, vt_ref, o_ref):
    t = MLA_TQ
    nq = q_ref.shape[0] // t
    krow = lax.broadcasted_iota(jnp.int32, (t, t), 0)
    qcol = lax.broadcasted_iota(jnp.int32, (t, t), 1)
    ones = jnp.ones((8, t), BF16)
    for qi in range(nq):
        outs = []
        for j in range(2):
            qh = q_ref[qi * t:(qi + 1) * t, j * LANES:(j + 1) * LANES]
            m = acc = None
            for kb in range(qi + 1):
                s = _dot_nt(k_ref[kb * t:(kb + 1) * t, j * LANES:(j + 1) * LANES], qh)
                if kb == qi:
                    s = jnp.where(krow <= qcol, s, NEG_INF)
                bm = jnp.max(s, axis=0, keepdims=True)
                m_new = bm if m is None else jnp.maximum(m, bm)
                p = jnp.exp(s - m_new).astype(BF16)
                vt1 = jnp.concatenate([vt_ref[kb, j * MLA_V:(j + 1) * MLA_V, :], ones], axis=0)
                pv = _dot(vt1, p)
                acc = pv if m is None else jnp.exp(m - m_new) * acc + pv
                m = m_new
            outs.append(acc[:MLA_V] / acc[MLA_V:MLA_V + 1])
        o_ref[qi * t:(qi + 1) * t, :] = jnp.concatenate(outs, axis=0).T.astype(BF16)


def _mla_attn(q, k, vt, batch, seq):
    nq = seq // MLA_TQ
    vt4 = vt.reshape(batch, nq, MLA_HEADS * MLA_V, MLA_TQ)
    return pl.pallas_call(
        _mla_attn_kernel,
        grid=(batch, MLA_HEADS // 2),
        in_specs=[pl.BlockSpec((seq, 256), lambda b, p: (b, p)),
                  pl.BlockSpec((seq, 256), lambda b, p: (b, p)),
                  pl.BlockSpec((None, nq, 2 * MLA_V, MLA_TQ), lambda b, p: (b, 0, p, 0))],
        out_specs=pl.BlockSpec((seq, LANES), lambda b, p: (b, p)),
        out_shape=jax.ShapeDtypeStruct((batch * seq, MLA_HEADS * MLA_V), BF16),
        compiler_params=_cparams(("parallel", "parallel"), VMEM_LIMIT),
        name="mla_attn",
    )(q, k, vt4)


GLA_TB = 512
GLA_SLOTS = 12
HK = GLA_HEADS * GLA_DK
HV = GLA_HEADS * GLA_DV


def _gla_constants():
    c = GLA_CHUNK
    t = np.arange(c)[:, None]
    s = np.arange(c)[None, :]
    slots = []
    zero = np.zeros((c, c), bool)
    for p in (1, 2, 3):
        slots.append((t // 16 == p) & (s < 16 * p))
    slots.append(zero)
    for p in (1, 2, 3):
        slots.append((t // 16 == s // 16) & ((t % 16) // 4 == p) & (s % 16 < 4 * p))
    slots.append(zero)
    for d in (1, 2, 3):
        slots.append((t == s + d) & (t // 4 == s // 4))
    slots.append(t == s)
    mask = np.concatenate(slots, axis=1).astype(np.float32)
    assert (sum(slots) == (t >= s)).all()
    mask = np.tile(mask, (GLA_HEADS, 1))
    tri = (t >= s).astype(np.float32)
    hm = np.kron(np.eye(GLA_HEADS), np.ones((c, GLA_DK))).astype(np.float32)
    bd = np.kron(np.eye(GLA_HEADS), np.ones((GLA_DV, GLA_DK))).astype(np.float32)
    return (jnp.asarray(tri, BF16), jnp.asarray(hm, BF16), jnp.asarray(mask, BF16), jnp.asarray(bd, F32))


def _log_sigmoid(x):
    return jnp.minimum(x, 0.0) - jnp.log1p(jnp.exp(-jnp.abs(x)))


def _split3(x):
    hi = x.astype(BF16)
    r1 = x - hi.astype(F32)
    mid = r1.astype(BF16)
    lo = (r1 - mid.astype(F32)).astype(BF16)
    return hi, mid, lo


def _bc_rows(row, n):
    return jnp.broadcast_to(row, (n, row.shape[1]))


def _gla_kernel(blk_ref, q_ref, v_ref, r_ref, k_ref, wg_ref, bg_ref, gn_ref, tri_ref, hm_ref, mask_ref, bd_ref,
                o_ref, st_ref, st_scr):
    c = GLA_CHUNK

    @pl.when(pl.program_id(1) == 0)
    def _():
        st_scr[...] = jnp.zeros_like(st_scr)

    rows = lax.broadcasted_iota(jnp.int32, (c, HK), 0)
    rm4 = rows & 3

    def chunk(ci, st):
        sl = slice(ci * c, (ci + 1) * c)
        pre = _dot(blk_ref[sl, :].astype(BF16), wg_ref[...]) + bg_ref[...]
        g = _log_sigmoid(pre) * (1.0 / GLA_TAU)
        cum3 = _dot(tri_ref[...], jnp.concatenate(_split3(g), axis=1))
        cum = cum3[:, :HK] + cum3[:, HK:2 * HK] + cum3[:, 2 * HK:]
        q = q_ref[sl, :]
        k = k_ref[sl, :]
        vb = v_ref[sl, :].astype(BF16)

        def kvar(e):
            return (k * jnp.exp(jnp.minimum(e, 0.0))).astype(BF16)

        zeros_k = jnp.zeros((c, HK), BF16)
        bnd = [cum[16 * p - 1:16 * p, :] for p in (1, 2, 3)]
        base1 = jnp.concatenate([jnp.zeros((16, HK), F32)] + [_bc_rows(b, 16) for b in bnd], axis=0)
        q1 = q * jnp.exp(cum - base1)
        k1 = jnp.concatenate([kvar(_bc_rows(b, c) - cum) for b in bnd] + [zeros_k], axis=0)
        sh = [pltpu.roll(cum, d, axis=0) for d in (1, 2, 3, 4)]
        base2 = jnp.where(rm4 == 0, sh[0], jnp.where(rm4 == 1, sh[1], jnp.where(rm4 == 2, sh[2], sh[3])))
        base2 = jnp.where(rows < 4, 0.0, base2)
        q2 = q * jnp.exp(cum - base2)
        k2 = []
        for p in (1, 2, 3):
            bk = jnp.concatenate([_bc_rows(cum[16 * i + 4 * p - 1:16 * i + 4 * p, :], 16) for i in range(4)], axis=0)
            k2.append(kvar(bk - cum))
        k2 = jnp.concatenate(k2 + [zeros_k], axis=0)
        k3 = jnp.concatenate([kvar(pltpu.roll(cum, c - d, axis=0) - cum) for d in (1, 2, 3)] + [k.astype(BF16)],
                             axis=0)
        hm = hm_ref[...]

        def stack(x):
            return jnp.concatenate([x.astype(BF16)] * GLA_HEADS, axis=0) * hm

        p_all = jnp.concatenate([_dot_nt(stack(q1), k1), _dot_nt(stack(q2), k2), _dot_nt(stack(q), k3)], axis=1)
        pm = p_all.astype(BF16) * mask_ref[...]
        o = _dot_nt((q * jnp.exp(cum)).astype(BF16), st.astype(BF16))
        intra = []
        for h in range(GLA_HEADS):
            vh = vb[:, h * GLA_DV:(h + 1) * GLA_DV]
            intra.append(_dot(pm[h * c:(h + 1) * c, :], jnp.concatenate([vh] * GLA_SLOTS, axis=0)))
        o = o + jnp.concatenate(intra, axis=1)
        last = cum[c - 1:c, :]
        st = st * jnp.exp(last) + _dot_tn(vb, kvar(_bc_rows(last, c) - cum)) * bd_ref[...]
        r = r_ref[sl, :]
        outs = []
        for h in range(GLA_HEADS):
            lanes = slice(h * GLA_DV, (h + 1) * GLA_DV)
            outs.append(_rms_norm(o[:, lanes], gn_ref[...]) * _silu(r[:, lanes]))
        o_ref[sl, :] = jnp.concatenate(outs, axis=1).astype(BF16)
        return st

    st = st_scr[...]
    for ci in range(GLA_TB // c):
        st = chunk(ci, st)
    st_scr[...] = st

    @pl.when(pl.program_id(1) == pl.num_programs(1) - 1)
    def _():
        for h in range(GLA_HEADS):
            blk = st_scr[h * GLA_DV:(h + 1) * GLA_DV, :]
            st_ref[0, h] = blk.T[h * GLA_DK:(h + 1) * GLA_DK, :]


def _gla(h, wg, bg, gn, consts, batch, seq):
    nt = seq // GLA_TB
    tri, hm, mask, bd = consts
    rowblk = lambda w, cb: pl.BlockSpec((GLA_TB, w), lambda b, t: (b * nt + t, cb))
    return pl.pallas_call(
        _gla_kernel,
        grid=(batch, nt),
        in_specs=[rowblk(LANES, 5), rowblk(HK, 3), rowblk(HV, 2), rowblk(HV, 3), rowblk(HK, 8),
                  _resident(wg.shape), _resident(bg.shape), _resident(gn.shape),
                  _resident(tri.shape), _resident(hm.shape), _resident(mask.shape), _resident(bd.shape)],
        out_specs=[pl.BlockSpec((GLA_TB, HV), lambda b, t: (b * nt + t, 0)),
                   pl.BlockSpec((1, GLA_HEADS, GLA_DK, GLA_DV), lambda b, t: (b, 0, 0, 0))],
        out_shape=[jax.ShapeDtypeStruct((batch * seq, HV), BF16),
                   jax.ShapeDtypeStruct((batch, GLA_HEADS, GLA_DK, GLA_DV), F32)],
        scratch_shapes=[pltpu.VMEM((HV, HK), F32)],
        compiler_params=_cparams(("parallel", "arbitrary"), VMEM_LIMIT),
        name="gla",
    )(h, h, h, h, h, wg, bg, gn, tri, hm, mask, bd)


def _alibi_slope(group, head):
    return 2.0 ** (-8.0 * (group * DIL_HPG + head + 1) / (DIL_GROUPS * DIL_HPG))


DIL_TM = 512
QKV_G = 3 * MIX_C


def _proj_dil_kernel(x_ref, w_ref, *rest, n_prev, tiles):
    p_refs = rest[n_prev:n_prev + DIL_GROUPS]
    r1_ref, r2_ref, r3_ref, acc_ref = rest[n_prev + DIL_GROUPS:]
    t = pl.program_id(0) % tiles
    xb = x_ref[...].astype(BF16)
    nb = QKV_G // LANES
    for g, r_ref in enumerate((r1_ref, r2_ref, r3_ref)):
        window, dil = DIL_PAIRS[g]
        res = _dot(xb, w_ref[:, g * QKV_G:(g + 1) * QKV_G])

        def put_rows(lo, width, g=g, res=res):
            for which in range(2):
                c0 = (1 + which) * MIX_C
                p_refs[g][which] = res[lo:lo + width, c0:c0 + MIX_C].T.reshape(DIL_HPG, DIL_HD, width)

        if window >= DIL_TM:
            keep_tiles = window // DIL_TM
            if keep_tiles >= tiles:
                put_rows(0, DIL_TM)
            else:
                pl.when(t >= tiles - keep_tiles)(lambda f=put_rows: f(0, DIL_TM))
        else:
            pl.when(t == tiles - 1)(lambda f=put_rows, w=window: f(DIL_TM - w, w))
        if dil == 1:
            r_ref[...] = res.astype(BF16)
            continue
        for c in range(nb):
            acc_ref[c] = res[:, c * LANES:(c + 1) * LANES]
        for r in range(dil):
            for c in range(nb):
                r_ref[r, :, c * LANES:(c + 1) * LANES] = acc_ref[c, pl.ds(r, DIL_TM // dil, stride=dil), :].astype(BF16)


def _proj_dil(x, w, seq, prev, layer, n_layers):
    m, k = x.shape
    tm = DIL_TM
    nt = m // tm
    tiles = seq // tm
    res = lambda dil: ((m // seq, dil, seq // dil, QKV_G),
                       pl.BlockSpec((None, dil, tm // dil, QKV_G), lambda i: (i // tiles, 0, i % tiles, 0)))
    (s2, b2), (s3, b3) = res(DIL_PAIRS[1][1]), res(DIL_PAIRS[2][1])
    p_shapes, p_specs = [], []
    for window, _ in DIL_PAIRS:
        keep = min(window, seq)
        width = min(keep, tm)
        first = tiles - max(keep // tm, 1)
        p_shapes.append(jax.ShapeDtypeStruct((n_layers, m // seq, 2, DIL_HPG, DIL_HD, keep), F32))
        p_specs.append(pl.BlockSpec(
            (None, None, 2, DIL_HPG, DIL_HD, width),
            lambda i, first=first: (layer, i // tiles, 0, 0, 0, jnp.maximum(i % tiles - first, 0))))
    in_specs = [pl.BlockSpec((tm, k), lambda i: (i, 0)), _resident(w.shape)]
    args = [x, w]
    aliases = {}
    if prev is not None:
        in_specs += [pl.BlockSpec(memory_space=pl.ANY)] * DIL_GROUPS
        aliases = {len(args) + g: g for g in range(DIL_GROUPS)}
        args += list(prev)
    outs = pl.pallas_call(
        functools.partial(_proj_dil_kernel, n_prev=len(aliases), tiles=tiles),
        grid=(nt,),
        in_specs=in_specs,
        out_specs=p_specs + [pl.BlockSpec((tm, QKV_G), lambda i: (i, 0)), b2, b3],
        out_shape=p_shapes + [jax.ShapeDtypeStruct((m, QKV_G), BF16),
                              jax.ShapeDtypeStruct(s2, BF16), jax.ShapeDtypeStruct(s3, BF16)],
        scratch_shapes=[pltpu.VMEM((QKV_G // LANES, tm, LANES), F32)],
        input_output_aliases=aliases,
        compiler_params=_cparams(("arbitrary",), VMEM_LIMIT),
        name="proj_dil",
    )(*args)
    return outs[:DIL_GROUPS], outs[DIL_GROUPS:]


DIL_QB = 4


def _dil_attn_kernel(q_ref, k_ref, v_ref, *rest, group, dil, mode):
    n = DIL_STEPS
    if mode == "chain":
        kp_ref, vp_ref, o_ref, lse_ref = rest
    else:
        o_ref, lse_ref = rest
    nk = 2 * n
    qi = lax.broadcasted_iota(jnp.int32, (n, nk), 0)
    ki = lax.broadcasted_iota(jnp.int32, (n, nk), 1)
    steps = qi + n - ki
    band = (steps >= 0) & (steps <= n)
    dist = (steps * dil).astype(F32)
    bias = [jnp.where(band, -_alibi_slope(group, h) * dist, NEG_INF) for h in range(DIL_HPG)]
    lane = lax.broadcasted_iota(jnp.int32, (n, LANES), 1)
    low = lane < DIL_HD
    for blk in range(DIL_QB):
        cur = slice(blk * n, (blk + 1) * n)
        if mode == "single":
            q = q_ref[blk]
            k = jnp.concatenate([k_ref[blk], k_ref[blk]], axis=0)
            v = jnp.concatenate([v_ref[blk], v_ref[blk]], axis=0)
            own = ki >= n
        else:
            q = q_ref[cur, :]
            if blk > 0:
                k, v = k_ref[(blk - 1) * n:(blk + 1) * n, :], v_ref[(blk - 1) * n:(blk + 1) * n, :]
                own = None
            elif mode == "chain":
                k = jnp.concatenate([kp_ref[...], k_ref[cur, :]], axis=0)
                v = jnp.concatenate([vp_ref[...], v_ref[cur, :]], axis=0)
                own = (ki >= n) | (pl.program_id(1) > 0)
            else:
                k = jnp.concatenate([k_ref[cur, :], k_ref[cur, :]], axis=0)
                v = jnp.concatenate([v_ref[cur, :], v_ref[cur, :]], axis=0)
                own = ki >= n
        o_all = []
        lse = jnp.zeros((n, LANES), F32)
        for j in range(DIL_HPG // 2):
            cols = slice(j * LANES, (j + 1) * LANES)
            q2, k2, v2 = q[:, cols], k[:, cols], v[:, cols]
            outs = []
            for par in range(2):
                qh = jnp.where(low if par == 0 else ~low, q2, jnp.zeros_like(q2))
                s = _dot_nt(qh, k2) + bias[2 * j + par]
                if own is not None:
                    s = jnp.where(own, s, NEG_INF)
                m = jnp.max(s, axis=-1, keepdims=True)
                p = jnp.exp(s - m)
                l = jnp.sum(p, axis=-1, keepdims=True)
                outs.append(_dot(p.astype(BF16), v2) / l)
                lse = jnp.where(lane == 2 * j + par, m + jnp.log(l), lse)
            o_all.append(jnp.where(low, outs[0], outs[1]))
        dst = (blk,) if mode == "single" else (cur, slice(None))
        o_ref[dst] = jnp.concatenate(o_all, axis=1).astype(BF16)
        lse_ref[dst] = lse


def _dil_attn(qkv_r, group):
    n = DIL_STEPS
    batch, dil, ls, _ = qkv_r.shape
    rows = DIL_QB * n
    if ls == n:
        mode, steps = "single", dil // DIL_QB
        spec = lambda c, w=MIX_C: pl.BlockSpec((None, DIL_QB, n, w), lambda b, s: (b, s, 0, c))
        extra = []
    elif ls == rows:
        mode, steps = "whole", dil
        spec = lambda c, w=MIX_C: pl.BlockSpec((None, None, rows, w), lambda b, s: (b, s, 0, c))
        extra = []
    else:
        assert dil == 1 and ls % rows == 0
        mode, steps = "chain", ls // rows
        spec = lambda c, w=MIX_C: pl.BlockSpec((None, None, rows, w), lambda b, s: (b, 0, s, c))
        extra = [pl.BlockSpec((None, None, n, MIX_C), lambda b, s, c=c: (b, 0, jnp.maximum(s * DIL_QB - 1, 0), c))
                 for c in (1, 2)]
    in_specs = [spec(0), spec(1), spec(2)] + extra
    return pl.pallas_call(
        functools.partial(_dil_attn_kernel, group=group, dil=DIL_PAIRS[group][1], mode=mode),
        grid=(batch, steps),
        in_specs=in_specs,
        out_specs=[spec(0), spec(0, LANES)],
        out_shape=[jax.ShapeDtypeStruct((batch, dil, ls, MIX_C), BF16),
                   jax.ShapeDtypeStruct((batch, dil, ls, LANES), F32)],
        compiler_params=_cparams(("parallel", "arbitrary"), VMEM_LIMIT),
        name="dil_attn%d" % group,
    )(*([qkv_r] * len(in_specs)))


def _dil_mix_ln_kernel(o1, o2, o3, l1, l2, l3, e_ref, w_ref, x_ref, g_ref, b_ref, o_ref, so2, so3, sl2, sl3):
    for src, dst, dil in ((o2, so2, DIL_PAIRS[1][1]), (l2, sl2, DIL_PAIRS[1][1]),
                          (o3, so3, DIL_PAIRS[2][1]), (l3, sl3, DIL_PAIRS[2][1])):
        for r in range(dil):
            for c in range(dst.shape[0]):
                dst[c, pl.ds(r, DIL_TM // dil, stride=dil), :] = src[r, :, c * LANES:(c + 1) * LANES].astype(F32)
    whole = lambda s: jnp.concatenate([s[c] for c in range(s.shape[0])], axis=1)
    os_ = [o1[...].astype(F32), whole(so2), whole(so3)]
    ls = [l1[...], sl2[0], sl3[0]]
    mx = jnp.maximum(jnp.maximum(ls[0], ls[1]), ls[2])
    es = [jnp.exp(l - mx) for l in ls]
    den = es[0] + es[1] + es[2]

    def spread(wt):
        hi = wt.astype(BF16)
        lo = (wt - hi.astype(F32)).astype(BF16)
        return _dot(hi, e_ref[...]) + _dot(lo, e_ref[...])

    mix = sum(spread(e / den) * o for e, o in zip(es, os_))
    z = DN_ALPHA * x_ref[...] + _dot(mix.astype(BF16), w_ref[...])
    o_ref[...] = _layer_norm(z, g_ref[...], b_ref[...])


def _dil_mix_ln(os_, lses, w, x, g, b):
    m = x.shape[0]
    tm = DIL_TM
    nt = m // tm
    tiles = nt // os_[0].shape[0]
    row = lambda wd: pl.BlockSpec((tm, wd), lambda i: (i, 0))
    res = lambda a: pl.BlockSpec((None, a.shape[1], tm // a.shape[1], a.shape[3]),
                                 lambda i: (i // tiles, 0, i % tiles, 0))
    flat = lambda a: a.reshape(m, a.shape[3])
    spread = np.kron(np.eye(DIL_HPG), np.ones((1, DIL_HD)))
    spread = jnp.asarray(np.concatenate([spread, np.zeros((LANES - DIL_HPG, MIX_C))]), BF16)
    blocks = lambda wd: pltpu.VMEM((wd // LANES, tm, LANES), F32)
    return pl.pallas_call(
        _dil_mix_ln_kernel,
        grid=(nt,),
        in_specs=[row(MIX_C), res(os_[1]), res(os_[2]), row(LANES), res(lses[1]), res(lses[2]), _resident(spread.shape),
                  _resident(w.shape), row(D_MODEL), _resident(g.shape), _resident(b.shape)],
        out_specs=row(D_MODEL),
        out_shape=jax.ShapeDtypeStruct((m, D_MODEL), F32),
        scratch_shapes=[blocks(MIX_C), blocks(MIX_C), blocks(LANES), blocks(LANES)],
        compiler_params=_cparams(("parallel",), VMEM_LIMIT),
        name="dil_mix_ln",
    )(flat(os_[0]), os_[1], os_[2], flat(lses[0]), lses[1], lses[2], spread, w, x, g, b)


def _mla_pre_s_kernel(h_ref, qn_ref, kvn_ref, wqn_ref, wqp_ref, wukt_ref, ck_ref, sk_ref, rows_ref, q_ref):
    h = h_ref[...]
    qn = _rms_norm(h[:, :MLA_Q_LORA], qn_ref[...]).astype(BF16)
    qnope = _dot(qn, wqn_ref[...]).astype(BF16)
    qp = _dot(qn, wqp_ref[...])
    cq = jnp.concatenate([ck_ref[...]] * MLA_HEADS, axis=1)
    sq = jnp.concatenate([sk_ref[...]] * MLA_HEADS, axis=1)
    qp = qp * cq + _roll_lanes_left(qp, 32) * sq
    ckv = _rms_norm(h[:, MLA_Q_LORA:MLA_Q_LORA + MLA_KV_LORA], kvn_ref[...])
    blk = h[:, 640:768]
    kpe = blk * ck_ref[...] + _roll_lanes_left(blk, 32) * sk_ref[...]
    rows_ref[:, :MLA_KV_LORA] = ckv
    rows_ref[:, MLA_KV_LORA:] = kpe[:, :MLA_ROPE]
    for hd in range(MLA_HEADS):
        cols = slice(hd * LANES, (hd + 1) * LANES)
        q_ref[hd, :, :MLA_KV_LORA] = _dot(qnope[:, cols], wukt_ref[hd])
        q_ref[hd, :, MLA_KV_LORA:] = qp[:, hd * LANES:hd * LANES + MLA_ROPE]


def _mla_pre_s(h, qn, kvn, wqn, wqp, wukt, tabs):
    m = h.shape[0]
    full = lambda a: _resident(a.shape)
    return pl.pallas_call(
        _mla_pre_s_kernel,
        grid=(1,),
        in_specs=[pl.BlockSpec((m, 768), lambda i: (0, 0)), full(qn), full(kvn), full(wqn), full(wqp), full(wukt)]
        + [full(t) for t in tabs[2:]],
        out_specs=[pl.BlockSpec((m, MLA_KV_DIM), lambda i: (0, 0)),
                   pl.BlockSpec((MLA_HEADS, m, MLA_KV_DIM), lambda i: (0, 0, 0))],
        out_shape=[jax.ShapeDtypeStruct((m, MLA_KV_DIM), F32), jax.ShapeDtypeStruct((MLA_HEADS, m, MLA_KV_DIM), F32)],
        compiler_params=_cparams(("arbitrary",), VMEM_LIMIT),
        name="mla_pre_s",
    )(h, qn, kvn, wqn, wqp, wukt, *tabs[2:])


DEC_PAGES = 64


def _mla_dec_kernel(pt_ref, q_ref, new_ref, *rest):
    pages = rest[:DEC_PAGES]
    o_ref, m_scr, l_scr, acc_scr = rest[DEC_PAGES:]
    step = pl.program_id(1)
    q = q_ref[0]
    qb = q.astype(BF16)

    @pl.when(step == 0)
    def _():
        new = new_ref[0]
        m_scr[...] = jnp.sum(q * new, axis=-1, keepdims=True) * MLA_SCALE
        l_scr[...] = jnp.ones_like(l_scr)
        acc_scr[...] = jnp.broadcast_to(new[:, :MLA_KV_LORA], acc_scr.shape)

    kvt = jnp.concatenate([p[...].astype(BF16) for p in pages], axis=1)
    s = _dot(qb, kvt) * MLA_SCALE
    m_old = m_scr[...]
    m_new = jnp.maximum(m_old, jnp.max(s, axis=-1, keepdims=True))
    alpha = jnp.exp(m_old - m_new)
    p = jnp.exp(s - m_new)
    m_scr[...] = m_new
    l_scr[...] = alpha * l_scr[...] + jnp.sum(p, axis=-1, keepdims=True)
    acc_scr[...] = alpha * acc_scr[...] + _dot_nt(p.astype(BF16), kvt[:MLA_KV_LORA, :])

    @pl.when(step == pl.num_programs(1) - 1)
    def _():
        o_ref[0] = acc_scr[...] / l_scr[...]


def _mla_dec(page_table, q_abs, rows_new, cache_t, layer):
    batch = q_abs.shape[0]
    n_pages = page_table.shape[1]
    steps = n_pages // DEC_PAGES
    pt = page_table.reshape(-1)

    def page_spec(j):
        return pl.BlockSpec((None, None, MLA_KV_DIM, PAGE_SIZE),
                            lambda b, s, pt_ref: (layer, pt_ref[b * n_pages + s * DEC_PAGES + j], 0, 0))

    grid_spec = pltpu.PrefetchScalarGridSpec(
        num_scalar_prefetch=1,
        grid=(batch, steps),
        in_specs=[pl.BlockSpec((1, MLA_HEADS, MLA_KV_DIM), lambda b, s, pt_ref: (b, 0, 0)),
                  pl.BlockSpec((1, 1, MLA_KV_DIM), lambda b, s, pt_ref: (b, 0, 0))]
        + [page_spec(j) for j in range(DEC_PAGES)],
        out_specs=pl.BlockSpec((1, MLA_HEADS, MLA_KV_LORA), lambda b, s, pt_ref: (b, 0, 0)),
        scratch_shapes=[pltpu.VMEM((MLA_HEADS, 1), F32), pltpu.VMEM((MLA_HEADS, 1), F32),
                        pltpu.VMEM((MLA_HEADS, MLA_KV_LORA), F32)],
    )
    return pl.pallas_call(
        _mla_dec_kernel,
        grid_spec=grid_spec,
        out_shape=jax.ShapeDtypeStruct((batch, MLA_HEADS, MLA_KV_LORA), F32),
        compiler_params=_cparams(("parallel", "arbitrary"), VMEM_LIMIT),
        name="mla_dec",
    )(pt, q_abs, rows_new.reshape(batch, 1, MLA_KV_DIM), *([cache_t] * DEC_PAGES))


GLA_DEC_B = 8


def _gla_dec_kernel(blk_ref, q_ref, v_ref, r_ref, k_ref, s_ref, wg_ref, bg_ref, gn_ref, o_ref, so_ref):
    pre = _dot(blk_ref[...].astype(BF16), wg_ref[...]) + bg_ref[...]
    a_all = jnp.exp(_log_sigmoid(pre) * (1.0 / GLA_TAU))
    eye = (lax.broadcasted_iota(jnp.int32, (GLA_DK, GLA_DK), 0)
           == lax.broadcasted_iota(jnp.int32, (GLA_DK, GLA_DK), 1))
    diag = lambda row: jnp.where(eye, jnp.broadcast_to(row, (GLA_DK, GLA_DK)), 0.0)
    for t in range(GLA_DEC_B):
        tok = slice(t, t + 1)
        a, k, v, r = a_all[tok], k_ref[tok, :], v_ref[tok, :], r_ref[tok, :]
        q = q_ref[tok, :]
        outs = []
        for h in range(GLA_HEADS):
            kl = slice(h * GLA_DK, (h + 1) * GLA_DK)
            vl = slice(h * GLA_DV, (h + 1) * GLA_DV)
            lhs = jnp.concatenate([diag(a[:, kl]), diag(k[:, kl])], axis=1)
            rhs = jnp.concatenate([s_ref[t, h], jnp.broadcast_to(v[:, vl], (GLA_DK, GLA_DV))], axis=0)
            s_new = jnp.dot(lhs, rhs, preferred_element_type=F32, precision=lax.Precision.HIGHEST)
            so_ref[t, h] = s_new
            o = jnp.dot(jnp.broadcast_to(q[:, kl], (8, GLA_DK)), s_new, preferred_element_type=F32,
                        precision=lax.Precision.HIGHEST)[:1]
            outs.append(_rms_norm(o, gn_ref[...]) * _silu(r[:, vl]))
        o_ref[tok, :] = jnp.concatenate(outs, axis=1)


def _gla_dec(h, state, layer, wg, bg, gn):
    batch = h.shape[0]
    tb = GLA_DEC_B
    rowblk = lambda w, cb: pl.BlockSpec((tb, w), lambda b: (b, cb))
    st_in = pl.BlockSpec((None, tb, GLA_HEADS, GLA_DK, GLA_DV), lambda b: (layer, b, 0, 0, 0))
    st = pl.BlockSpec((tb, GLA_HEADS, GLA_DK, GLA_DV), lambda b: (b, 0, 0, 0))
    return pl.pallas_call(
        _gla_dec_kernel,
        grid=(batch // tb,),
        in_specs=[rowblk(LANES, 5), rowblk(HK, 3), rowblk(HV, 2), rowblk(HV, 3), rowblk(HK, 8), st_in,
                  _resident(wg.shape), _resident(bg.shape), _resident(gn.shape)],
        out_specs=[pl.BlockSpec((tb, HV), lambda b: (b, 0)), st],
        out_shape=[jax.ShapeDtypeStruct((batch, HV), F32), jax.ShapeDtypeStruct(state.shape[1:], F32)],
        compiler_params=_cparams(("parallel",), VMEM_LIMIT),
        name="gla_dec",
    )(h, h, h, h, h, state, wg, bg, gn)


def _dil_dec_kernel(x_ref, c1_ref, c2_ref, c3_ref, o_ref):
    heads = range(DIL_HPG)
    hcol = lax.broadcasted_iota(jnp.int32, (DIL_HPG, 1), 0).astype(F32)
    outs, lses = [], []
    for g, c_ref in enumerate((c1_ref, c2_ref, c3_ref)):
        window, dil = DIL_PAIRS[g]
        pos = lax.broadcasted_iota(jnp.int32, (1, window), 1)
        q, k_new, v_new = (x_ref[0, 3 * g + w] for w in range(3))
        s = jnp.concatenate([jnp.sum(c_ref[0, h] * q[:, h:h + 1], axis=0, keepdims=True) for h in heads], axis=0)
        slope = jnp.exp((g * DIL_HPG + hcol + 1.0) * (-8.0 * math.log(2.0) / (DIL_GROUPS * DIL_HPG)))
        s = s - slope * (window - pos).astype(F32)
        s = jnp.where((pos & (dil - 1)) == 0, s, NEG_INF)
        qk = jnp.sum(q * k_new, axis=0, keepdims=True)
        s_new = jnp.concatenate([qk[:, h:h + 1] for h in heads], axis=0)
        m = jnp.maximum(jnp.max(s, axis=1, keepdims=True), s_new)
        p = jnp.exp(s - m)
        p_new = jnp.exp(s_new - m)
        l = jnp.sum(p, axis=1, keepdims=True) + p_new
        w = p / l
        w_new = p_new / l
        outs.append([jnp.sum(c_ref[1, h] * w[h:h + 1, :], axis=1, keepdims=True)
                     + w_new[h:h + 1, :] * v_new[:, h:h + 1] for h in heads])
        lses.append(m + jnp.log(l))
    mx = jnp.maximum(jnp.maximum(lses[0], lses[1]), lses[2])
    es = [jnp.exp(l - mx) for l in lses]
    den = es[0] + es[1] + es[2]
    ws = [e / den for e in es]
    o_ref[0] = jnp.concatenate(
        [sum(ws[g][h:h + 1, :] * outs[g][h] for g in range(DIL_GROUPS)) for h in heads], axis=1)


def _dil_dec(qkv_t, caches_t, layer):
    batch = qkv_t.shape[0]
    specs = [pl.BlockSpec((1, 3 * DIL_GROUPS, DIL_HD, DIL_HPG), lambda b: (b, 0, 0, 0))]
    for (window, _), c in zip(DIL_PAIRS, caches_t):
        assert c.shape[-1] == window
        specs.append(pl.BlockSpec((None, None, 2, DIL_HPG, DIL_HD, window), lambda b: (layer, b, 0, 0, 0, 0)))
    return pl.pallas_call(
        _dil_dec_kernel,
        grid=(batch,),
        in_specs=specs,
        out_specs=pl.BlockSpec((1, DIL_HD, DIL_HPG), lambda b: (b, 0, 0)),
        out_shape=jax.ShapeDtypeStruct((batch, DIL_HD, DIL_HPG), F32),
        compiler_params=_cparams(("parallel",), VMEM_LIMIT),
        name="dil_dec",
    )(qkv_t, *caches_t)


def _even_in_cols():
    o_cq, o_kv, o_gq, o_gk, o_gv, o_lr, o_gr = 0, 384, 672, 928, 1184, 1696, 1712
    r = np.arange
    kpe = o_kv + MLA_KV_LORA
    idx = np.concatenate([r(o_cq, o_cq + 384), r(o_kv, o_kv + 256), r(kpe, kpe + 32), r(kpe + 16, kpe + 32),
                          r(kpe, kpe + 16), r(o_lr, o_lr + 16), np.full(48, -1), r(o_gq, o_gq + 256),
                          r(o_gv, o_gv + 512), r(o_gr, o_gr + 512), r(o_gk, o_gk + 256)])
    assert idx.shape[0] == IN_A_PAD
    return idx


def _gather_cols(w, idx):
    idx = np.asarray(idx)
    a, b = idx[:-1], idx[1:]
    same_run = ((a >= 0) & (b >= 0) & (b - a == 1)) | ((a < 0) & (b < 0))
    cuts = np.flatnonzero(~same_run) + 1
    parts = []
    for run in np.split(idx, cuts):
        if run[0] < 0:
            parts.append(jnp.zeros((w.shape[0], run.size), w.dtype))
        else:
            parts.append(w[:, int(run[0]):int(run[-1]) + 1])
    return jnp.concatenate(parts, axis=1)


def _uq_cols(kind):
    idx = []
    for h in range(MLA_HEADS):
        b = h * (MLA_NOPE + MLA_ROPE)
        nope = np.arange(b, b + 64)
        x1, x2 = np.arange(b + 64, b + 80), np.arange(b + 80, b + 96)
        if kind == "full":
            idx += [nope, x1, x2, x2, x1]
        elif kind == "nope":
            idx += [nope, np.full(64, -1)]
        else:
            idx += [x1, x2, x2, x1, np.full(64, -1)]
    return np.concatenate(idx)


def _odd_in_cols():
    blk = lambda which, g: np.arange((which * DIL_GROUPS + g) * MIX_C, (which * DIL_GROUPS + g + 1) * MIX_C)
    return np.concatenate([blk(which, g) for g in range(DIL_GROUPS) for which in range(3)])


TM_DENSE = 1024
TM_WIDE = 256


def kernel(x_prompt, x_sample, cache_mla, state_gla, cache_dil_w128, cache_dil_w512, cache_dil_w2048, page_table,
           w_in_a, mla_q_norm, mla_w_uq, mla_kv_norm, mla_w_uk, mla_w_uv, gla_w_gate2, gla_b_gate, gla_norm, w_out_a,
           w_in_c, w_out_c, ffn_w_in, ffn_w_out, ln_g, ln_b):
    batch, seq, _ = x_prompt.shape
    dbatch = x_sample.shape[0]
    assert seq % MLA_TQ == 0 and seq % DIL_TM == 0 and seq >= DIL_PAIRS[-1][0]
    n_even, n_odd = (DEPTH + 1) // 2, DEPTH // 2
    cache_mla_t = jnp.transpose(cache_mla, (0, 1, 3, 2))
    dil_caches_t = [jnp.transpose(c, (0, 1, 3, 4, 5, 2)) for c in (cache_dil_w128, cache_dil_w512, cache_dil_w2048)]
    gla_consts = _gla_constants()
    tabs_p = _rope_tables(jnp.arange(seq, dtype=jnp.int32))
    tabs_s = _rope_tables(jnp.full((dbatch,), PAST_LEN, jnp.int32))
    even_cols, odd_cols = _even_in_cols(), _odd_in_cols()
    even_scale = np.ones((1, IN_A_PAD), np.float32)
    even_scale[:, 768:768 + HK] = GLA_DK ** -0.5
    odd_scale = np.ones((1, IN_C), np.float32)
    for g in range(DIL_GROUPS):
        odd_scale[:, g * QKV_G:g * QKV_G + MIX_C] = DIL_HD ** -0.5
    even_scale, odd_scale = jnp.asarray(even_scale), jnp.asarray(odd_scale)
    row2 = lambda v: v.reshape(1, -1).astype(F32)

    xp = x_prompt.reshape(batch * seq, D_MODEL)
    xs = x_sample.reshape(dbatch, D_MODEL)
    mla_s, gla_p, gla_s = [], [], []
    rows_t, dil_t = None, None
    dil_s = [[] for _ in DIL_PAIRS]

    for layer in range(DEPTH):
        i = layer // 2
        g0, b0, g1, b1 = (row2(ln_g[layer, 0]), row2(ln_b[layer, 0]), row2(ln_g[layer, 1]), row2(ln_b[layer, 1]))
        if layer % 2 == 0:
            w_in = (_gather_cols(w_in_a[i], even_cols) * even_scale).astype(BF16)
            wq_full = _gather_cols(mla_w_uq[i], _uq_cols("full")).astype(BF16)
            wq_nope = _gather_cols(mla_w_uq[i], _uq_cols("nope")).astype(BF16)
            wq_rope = _gather_cols(mla_w_uq[i], _uq_cols("rope")).astype(BF16)
            wuk_pad = jnp.pad(mla_w_uk[i], ((0, 0), (0, 0), (0, LANES - MLA_NOPE))).reshape(MLA_KV_LORA, -1).astype(BF16)
            wuk_t = jnp.pad(jnp.transpose(mla_w_uk[i], (1, 2, 0)), ((0, 0), (0, LANES - MLA_NOPE), (0, 0))).astype(BF16)
            wuv_t = mla_w_uv[i].reshape(MLA_KV_LORA, -1).T.astype(BF16)
            wuv_bd = (jnp.eye(MLA_HEADS, dtype=F32)[:, None, :, None]
                      * jnp.transpose(mla_w_uv[i], (1, 0, 2))[:, :, None, :]).reshape(
                          MLA_HEADS * MLA_KV_LORA, MLA_HEADS * MLA_V).astype(BF16)
            wg = jnp.pad(gla_w_gate2[i], ((64, LANES - 64 - GLA_GATE_RANK), (0, 0))).astype(BF16)
            bg, gn = row2(gla_b_gate[i]), row2(gla_norm[i])
            qn, kvn = row2(mla_q_norm[i]), row2(mla_kv_norm[i])
            w_out = w_out_a[i].astype(BF16)

            h = _proj(xp, w_in, TM_DENSE)
            rows_t, qf, kf, vt = _mla_pre(h, qn, kvn, wq_full, wuk_pad, wuv_t, tabs_p, seq, rows_t, i, n_even)
            mla_out = _mla_attn(qf, kf, vt, batch, seq)
            gla_out, s_fin = _gla(h, wg, bg, gn, gla_consts, batch, seq)
            gla_p.append(s_fin)
            xp = _mix_ln([mla_out, gla_out], w_out, xp, g0, b0, TM_DENSE)

            h = _proj(xs, w_in, TM_DENSE)
            rows, q_abs = _mla_pre_s(h, qn, kvn, wq_nope, wq_rope, wuk_t, tabs_s)
            lat = _mla_dec(page_table, jnp.transpose(q_abs, (1, 0, 2)), rows, cache_mla_t, i)
            mla_out = _proj(lat.reshape(dbatch, MLA_HEADS * MLA_KV_LORA), wuv_bd, TM_DENSE)
            gla_out, s_fin = _gla_dec(h, state_gla, i, wg, bg, gn)
            mla_s.append(rows.reshape(dbatch, 1, MLA_KV_DIM))
            gla_s.append(s_fin)
            xs = _mix_ln([mla_out, gla_out], w_out, xs, g0, b0, TM_DENSE)
        else:
            w_in = (_gather_cols(w_in_c[i], odd_cols) * odd_scale).astype(BF16)
            w_out = w_out_c[i].astype(BF16)

            dil_t, (r1, r2, r3) = _proj_dil(xp, w_in, seq, dil_t, i, n_odd)
            res = [r1.reshape(batch, 1, seq, QKV_G), r2, r3]
            os_, lses = zip(*[_dil_attn(res[g], g) for g in range(DIL_GROUPS)])
            xp = _dil_mix_ln(os_, lses, w_out, xp, g0, b0)

            qkv = _proj(xs, w_in, TM_WIDE).reshape(dbatch, 3 * DIL_GROUPS, DIL_HPG, DIL_HD)
            mix = _dil_dec(jnp.transpose(qkv, (0, 1, 3, 2)), dil_caches_t, i)
            mix = jnp.transpose(mix, (0, 2, 1)).reshape(dbatch, MIX_C)
            for g in range(DIL_GROUPS):
                dil_s[g].append(qkv[:, 3 * g + 1:3 * g + 3].reshape(dbatch, 1, 2, DIL_HPG, DIL_HD))
            xs = _mix_ln([mix], w_out, xs, g0, b0, TM_DENSE)

        wi, wo = ffn_w_in[layer].astype(BF16), ffn_w_out[layer].astype(BF16)
        xp = _ffn(xp, wi, wo, g1, b1, TM_FFN)
        xs = _ffn(xs, wi, wo, g1, b1, TM_FFN)

    st = jnp.stack
    mla_p = jnp.transpose(rows_t, (0, 1, 3, 2))
    dil_p = [jnp.transpose(p, (0, 1, 5, 2, 3, 4)) for p in dil_t]
    return (xp.reshape(batch, seq, D_MODEL), xs.reshape(dbatch, 1, D_MODEL), mla_p, st(mla_s), st(gla_p), st(gla_s),
            dil_p[0], st(dil_s[0]), dil_p[1], st(dil_s[1]), dil_p[2], st(dil_s[2]))
```

```python
import functools
import math

import numpy as np
import jax
import jax.numpy as jnp
from jax import lax
from jax.experimental import pallas as pl
from jax.experimental.pallas import tpu as pltpu

F32 = jnp.float32
BF16 = jnp.bfloat16

D_MODEL = 1024
DEPTH = 4
PAST_LEN = 16384
PAGE_SIZE = 128
MLA_HEADS = 8
MLA_Q_LORA = 384
MLA_KV_LORA = 256
MLA_NOPE = 64
MLA_ROPE = 32
MLA_V = 64
MLA_KV_DIM = MLA_KV_LORA + MLA_ROPE
ROPE_BASE = 10000.0
GLA_HEADS = 4
GLA_DK = 64
GLA_DV = 128
GLA_GATE_RANK = 16
GLA_TAU = 16.0
GLA_CHUNK = 64
DIL_PAIRS = ((128, 1), (512, 4), (2048, 16))
DIL_GROUPS = 3
DIL_HPG = 8
DIL_HD = 64
DIL_STEPS = 128
D_FF = -(-8 * D_MODEL // (3 * 256)) * 256
DN_ALPHA = (2.0 * DEPTH) ** 0.25
IN_A_PAD = 2304
MIX_C = DIL_HPG * DIL_HD
IN_C = 3 * DIL_GROUPS * MIX_C
MLA_SCALE = (MLA_NOPE + MLA_ROPE) ** -0.5

V7X_VMEM_BYTES = 64 * 1024 * 1024
VMEM_LIMIT = 56 * 1024 * 1024
LANES = 128

NEG_INF = float("-inf")


def _cparams(sem, limit=None):
    return pltpu.CompilerParams(dimension_semantics=sem, vmem_limit_bytes=limit)


def _resident(shape):
    nd = len(shape)
    return pl.BlockSpec(shape, lambda *_: (0,) * nd, pipeline_mode=pl.Buffered(1))


def _row_of(p):
    stack, idx = p
    return pl.BlockSpec((None, 1, stack.shape[-1]), lambda *_: (idx, 0, 0), pipeline_mode=pl.Buffered(1))


def _layer_norm(z, g, b):
    mu = jnp.mean(z, axis=-1, keepdims=True)
    zc = z - mu
    var = jnp.mean(zc * zc, axis=-1, keepdims=True)
    return zc * lax.rsqrt(var + 1e-5) * g + b


def _rms_norm(z, g):
    return z * lax.rsqrt(jnp.mean(z * z, axis=-1, keepdims=True) + 1e-6) * g


def _silu(z):
    return z * (1.0 / (1.0 + jnp.exp(-z)))


def _dot(a, b):
    return jnp.dot(a, b, preferred_element_type=F32)


def _dot_nt(a, b):
    return lax.dot_general(a, b, (((1,), (1,)), ((), ())), preferred_element_type=F32)


def _dot_tn(a, b):
    return lax.dot_general(a, b, (((0,), (0,)), ((), ())), preferred_element_type=F32)


def _proj_kernel(x_ref, w_ref, o_ref):
    o_ref[...] = _dot(x_ref[...].astype(BF16), w_ref[...]).astype(o_ref.dtype)


def _proj(x, w, tm, out_dtype=F32):
    m, k = x.shape
    n = w.shape[1]
    tm = min(tm, m)
    return pl.pallas_call(
        _proj_kernel,
        grid=(m // tm,),
        in_specs=[pl.BlockSpec((tm, k), lambda i: (i, 0)), _resident((k, n))],
        out_specs=pl.BlockSpec((tm, n), lambda i: (i, 0)),
        out_shape=jax.ShapeDtypeStruct((m, n), out_dtype),
        compiler_params=_cparams(("parallel",), VMEM_LIMIT),
        name="proj",
    )(x, w)


def _mix_ln_kernel(*refs, n_parts):
    parts = refs[:n_parts]
    w_ref, x_ref, g_ref, b_ref, o_ref = refs[n_parts:]
    a = jnp.concatenate([p[...].astype(BF16) for p in parts], axis=-1) if n_parts > 1 else parts[0][...].astype(BF16)
    z = DN_ALPHA * x_ref[...] + _dot(a, w_ref[...])
    o_ref[...] = _layer_norm(z, g_ref[...], b_ref[...])


def _mix_ln(parts, w, x, g, b, tm):
    m = x.shape[0]
    tm = min(tm, m)
    kern = functools.partial(_mix_ln_kernel, n_parts=len(parts))
    return pl.pallas_call(
        kern,
        grid=(m // tm,),
        in_specs=[pl.BlockSpec((tm, p.shape[1]), lambda i: (i, 0)) for p in parts]
        + [_resident(w.shape), pl.BlockSpec((tm, D_MODEL), lambda i: (i, 0)), _row_of(g), _row_of(b)],
        out_specs=pl.BlockSpec((tm, D_MODEL), lambda i: (i, 0)),
        out_shape=jax.ShapeDtypeStruct((m, D_MODEL), F32),
        compiler_params=_cparams(("parallel",), VMEM_LIMIT),
        name="mix_ln",
    )(*parts, w, x, g[0], b[0])


FF_CHUNK = 256
TM_FFN = 1024


def _ffn_kernel(x_ref, wi_ref, wo_ref, g_ref, b_ref, o_ref, act_ref):
    x = x_ref[...]
    xb = x.astype(BF16)
    for c in range(D_FF // FF_CHUNK):
        lo = c * FF_CHUNK
        gate = _dot(xb, wi_ref[:, lo:lo + FF_CHUNK])
        up = _dot(xb, wi_ref[:, D_FF + lo:D_FF + lo + FF_CHUNK])
        act_ref[:, lo:lo + FF_CHUNK] = (_silu(gate) * up).astype(BF16)
    z = DN_ALPHA * x + _dot(act_ref[...], wo_ref[...])
    o_ref[...] = _layer_norm(z, g_ref[...], b_ref[...])


def _ffn(x, wi, wo, g, b, tm):
    m = x.shape[0]
    tm = min(tm, m)
    return pl.pallas_call(
        _ffn_kernel,
        grid=(m // tm,),
        in_specs=[pl.BlockSpec((tm, D_MODEL), lambda i: (i, 0)), _resident(wi.shape), _resident(wo.shape),
                  _row_of(g), _row_of(b)],
        out_specs=pl.BlockSpec((tm, D_MODEL), lambda i: (i, 0)),
        out_shape=jax.ShapeDtypeStruct((m, D_MODEL), F32),
        scratch_shapes=[pltpu.VMEM((tm, D_FF), BF16)],
        compiler_params=_cparams(("parallel",), VMEM_LIMIT),
        name="ffn",
    )(x, wi, wo, g[0], b[0])


def _rope_tables(pos):
    half = MLA_ROPE // 2
    inv = ROPE_BASE ** (-jnp.arange(half, dtype=F32) / half)
    ang = pos.astype(F32)[:, None] * inv[None, :]
    cos, sin = jnp.cos(ang), jnp.sin(ang)
    n = pos.shape[0]
    z = lambda w: jnp.zeros((n, w), F32)
    cc = jnp.concatenate([cos, cos], axis=1)
    ss = jnp.concatenate([-sin, sin], axis=1)
    cq = jnp.concatenate([jnp.ones((n, MLA_NOPE), F32), cc, z(32)], axis=1)
    sq = jnp.concatenate([z(MLA_NOPE), ss, z(32)], axis=1)
    ck = jnp.concatenate([cc, z(96)], axis=1)
    sk = jnp.concatenate([ss, z(96)], axis=1)
    return cq, sq, ck, sk


def _roll_lanes_left(x, k):
    return pltpu.roll(x, x.shape[-1] - k, axis=x.ndim - 1)


def _mla_pre_kernel(h_ref, qn_ref, kvn_ref, wq_ref, wuk_ref, wuvt_ref, cq_ref, sq_ref, ck_ref, sk_ref, *rest):
    rows_ref, q_ref, k_ref, vt_ref = rest[-4:]
    h = h_ref[...]
    qn = _rms_norm(h[:, :MLA_Q_LORA], qn_ref[...]).astype(BF16)
    q = _dot(qn, wq_ref[...])
    cq = jnp.concatenate([cq_ref[...]] * MLA_HEADS, axis=1)
    sq = jnp.concatenate([sq_ref[...]] * MLA_HEADS, axis=1)
    q_ref[...] = ((q * cq + _roll_lanes_left(q, 32) * sq) * MLA_SCALE).astype(BF16)
    ckv = _rms_norm(h[:, MLA_Q_LORA:MLA_Q_LORA + MLA_KV_LORA], kvn_ref[...])
    blk = h[:, 640:768]
    kpe = blk * ck_ref[...] + _roll_lanes_left(blk, 32) * sk_ref[...]
    rows_ref[:MLA_KV_LORA, :] = ckv.T
    rows_ref[MLA_KV_LORA:, :] = kpe.T[:MLA_ROPE, :]
    ckvb = ckv.astype(BF16)
    kpe_mid = pltpu.roll(kpe, 64, axis=1)
    k_ref[...] = (_dot(ckvb, wuk_ref[...]) + jnp.concatenate([kpe_mid] * MLA_HEADS, axis=1)).astype(BF16)
    vt_ref[0] = _dot_nt(wuvt_ref[...], ckvb).astype(BF16)


MLA_TQ = 512


def _mla_pre(h, qn, kvn, wq, wuk, wuvt, tabs, seq, rows_t, layer, n_layers):
    m = h.shape[0]
    tm = MLA_TQ
    nblk = seq // tm
    tab = pl.BlockSpec((tm, LANES), lambda i: (i % nblk, 0))
    row = lambda w: pl.BlockSpec((tm, w), lambda i: (i, 0))
    hv = MLA_HEADS * MLA_V
    in_specs = [pl.BlockSpec((tm, 768), lambda i: (i, 0)), _resident(qn.shape), _resident(kvn.shape),
                _resident(wq.shape), _resident(wuk.shape), _resident(wuvt.shape), tab, tab, tab, tab]
    args = [h, qn, kvn, wq, wuk, wuvt, *tabs]
    aliases = {}
    if rows_t is not None:
        in_specs.append(pl.BlockSpec(memory_space=pl.ANY))
        args.append(rows_t)
        aliases = {len(args) - 1: 0}
    return pl.pallas_call(
        _mla_pre_kernel,
        grid=(m // tm,),
        in_specs=in_specs,
        out_specs=[pl.BlockSpec((None, None, MLA_KV_DIM, tm), lambda i: (layer, i // nblk, 0, i % nblk)),
                   row(1024), row(1024), pl.BlockSpec((1, hv, tm), lambda i: (i, 0, 0))],
        out_shape=[jax.ShapeDtypeStruct((n_layers, m // seq, MLA_KV_DIM, seq), F32),
                   jax.ShapeDtypeStruct((m, 1024), BF16),
                   jax.ShapeDtypeStruct((m, 1024), BF16), jax.ShapeDtypeStruct((m // tm, hv, tm), BF16)],
        input_output_aliases=aliases,
        compiler_params=_cparams(("arbitrary",), VMEM_LIMIT),
        name="mla_pre",
    )(*args)


def _mla_attn_kernel(q_ref, k_ref, vt_ref, o_ref):
    t = MLA_TQ
    nq = q_ref.shape[0] // t
    krow = lax.broadcasted_iota(jnp.int32, (t, t), 0)
    qcol = lax.broadcasted_iota(jnp.int32, (t, t), 1)
    ones = jnp.ones((8, t), BF16)
    for qi in range(nq):
        outs = []
        for j in range(2):
            qh = q_ref[qi * t:(qi + 1) * t, j * LANES:(j + 1) * LANES]
            m = acc = None
            for kb in range(qi + 1):
                s = _dot_nt(k_ref[kb * t:(kb + 1) * t, j * LANES:(j + 1) * LANES], qh)
                if kb == qi:
                    s = jnp.where(krow <= qcol, s, NEG_INF)
                bm = jnp.max(s, axis=0, keepdims=True)
                m_new = bm if m is None else jnp.maximum(m, bm)
                p = jnp.exp(s - m_new).astype(BF16)
                vt1 = jnp.concatenate([vt_ref[kb, j * MLA_V:(j + 1) * MLA_V, :], ones], axis=0)
                pv = _dot(vt1, p)
                acc = pv if m is None else jnp.exp(m - m_new) * acc + pv
                m = m_new
            outs.append(acc[:MLA_V] / acc[MLA_V:MLA_V + 1])
        o_ref[qi * t:(qi + 1) * t, :] = jnp.concatenate(outs, axis=0).T.astype(BF16)


def _mla_attn(q, k, vt, batch, seq):
    nq = seq // MLA_TQ
    vt4 = vt.reshape(batch, nq, MLA_HEADS * MLA_V, MLA_TQ)
    return pl.pallas_call(
        _mla_attn_kernel,
        grid=(batch, MLA_HEADS // 2),
        in_specs=[pl.BlockSpec((seq, 256), lambda b, p: (b, p)),
                  pl.BlockSpec((seq, 256), lambda b, p: (b, p)),
                  pl.BlockSpec((None, nq, 2 * MLA_V, MLA_TQ), lambda b, p: (b, 0, p, 0))],
        out_specs=pl.BlockSpec((seq, LANES), lambda b, p: (b, p)),
        out_shape=jax.ShapeDtypeStruct((batch * seq, MLA_HEADS * MLA_V), BF16),
        compiler_params=_cparams(("parallel", "parallel"), VMEM_LIMIT),
        name="mla_attn",
    )(q, k, vt4)


GLA_TB = 512
GLA_SLOTS = 12
HK = GLA_HEADS * GLA_DK
HV = GLA_HEADS * GLA_DV


def _gla_constants():
    c = GLA_CHUNK
    t = np.arange(c)[:, None]
    s = np.arange(c)[None, :]
    slots = []
    zero = np.zeros((c, c), bool)
    for p in (1, 2, 3):
        slots.append((t // 16 == p) & (s < 16 * p))
    slots.append(zero)
    for p in (1, 2, 3):
        slots.append((t // 16 == s // 16) & ((t % 16) // 4 == p) & (s % 16 < 4 * p))
    slots.append(zero)
    for d in (1, 2, 3):
        slots.append((t == s + d) & (t // 4 == s // 4))
    slots.append(t == s)
    mask = np.concatenate(slots, axis=1).astype(np.float32)
    assert (sum(slots) == (t >= s)).all()
    mask = np.tile(mask, (GLA_HEADS, 1))
    tri = (t >= s).astype(np.float32)
    hm = np.kron(np.eye(GLA_HEADS), np.ones((c, GLA_DK))).astype(np.float32)
    bd = np.kron(np.eye(GLA_HEADS), np.ones((GLA_DV, GLA_DK))).astype(np.float32)
    return (jnp.asarray(tri, BF16), jnp.asarray(hm, BF16), jnp.asarray(mask, BF16), jnp.asarray(bd, F32))


def _log_sigmoid(x):
    return jnp.minimum(x, 0.0) - jnp.log1p(jnp.exp(-jnp.abs(x)))


def _split3(x):
    hi = x.astype(BF16)
    r1 = x - hi.astype(F32)
    mid = r1.astype(BF16)
    lo = (r1 - mid.astype(F32)).astype(BF16)
    return hi, mid, lo


def _bc_rows(row, n):
    return jnp.broadcast_to(row, (n, row.shape[1]))


def _gla_kernel(blk_ref, q_ref, v_ref, r_ref, k_ref, wg_ref, bg_ref, gn_ref, tri_ref, hm_ref, mask_ref, bd_ref,
                o_ref, st_ref, st_scr):
    c = GLA_CHUNK

    @pl.when(pl.program_id(1) == 0)
    def _():
        st_scr[...] = jnp.zeros_like(st_scr)

    rows = lax.broadcasted_iota(jnp.int32, (c, HK), 0)
    rm4 = rows & 3

    def chunk(ci, st):
        sl = slice(ci * c, (ci + 1) * c)
        pre = _dot(blk_ref[sl, :].astype(BF16), wg_ref[...]) + bg_ref[...]
        g = _log_sigmoid(pre) * (1.0 / GLA_TAU)
        cum3 = _dot(tri_ref[...], jnp.concatenate(_split3(g), axis=1))
        cum = cum3[:, :HK] + cum3[:, HK:2 * HK] + cum3[:, 2 * HK:]
        q = q_ref[sl, :]
        k = k_ref[sl, :]
        vb = v_ref[sl, :].astype(BF16)

        def kvar(e):
            return (k * jnp.exp(jnp.minimum(e, 0.0))).astype(BF16)

        zeros_k = jnp.zeros((c, HK), BF16)
        bnd = [cum[16 * p - 1:16 * p, :] for p in (1, 2, 3)]
        base1 = jnp.concatenate([jnp.zeros((16, HK), F32)] + [_bc_rows(b, 16) for b in bnd], axis=0)
        q1 = q * jnp.exp(cum - base1)
        k1 = jnp.concatenate([kvar(_bc_rows(b, c) - cum) for b in bnd] + [zeros_k], axis=0)
        sh = [pltpu.roll(cum, d, axis=0) for d in (1, 2, 3, 4)]
        base2 = jnp.where(rm4 == 0, sh[0], jnp.where(rm4 == 1, sh[1], jnp.where(rm4 == 2, sh[2], sh[3])))
        base2 = jnp.where(rows < 4, 0.0, base2)
        q2 = q * jnp.exp(cum - base2)
        k2 = []
        for p in (1, 2, 3):
            bk = jnp.concatenate([_bc_rows(cum[16 * i + 4 * p - 1:16 * i + 4 * p, :], 16) for i in range(4)], axis=0)
            k2.append(kvar(bk - cum))
        k2 = jnp.concatenate(k2 + [zeros_k], axis=0)
        k3 = jnp.concatenate([kvar(pltpu.roll(cum, c - d, axis=0) - cum) for d in (1, 2, 3)] + [k.astype(BF16)],
                             axis=0)
        hm = hm_ref[...]

        def stack(x):
            return jnp.concatenate([x.astype(BF16)] * GLA_HEADS, axis=0) * hm

        p_all = jnp.concatenate([_dot_nt(stack(q1), k1), _dot_nt(stack(q2), k2), _dot_nt(stack(q), k3)], axis=1)
        pm = p_all.astype(BF16) * mask_ref[...]
        o = _dot_nt((q * jnp.exp(cum)).astype(BF16), st.astype(BF16))
        intra = []
        for h in range(GLA_HEADS):
            vh = vb[:, h * GLA_DV:(h + 1) * GLA_DV]
            intra.append(_dot(pm[h * c:(h + 1) * c, :], jnp.concatenate([vh] * GLA_SLOTS, axis=0)))
        o = o + jnp.concatenate(intra, axis=1)
        last = cum[c - 1:c, :]
        st = st * jnp.exp(last) + _dot_tn(vb, kvar(_bc_rows(last, c) - cum)) * bd_ref[...]
        r = r_ref[sl, :]
        outs = []
        for h in range(GLA_HEADS):
            lanes = slice(h * GLA_DV, (h + 1) * GLA_DV)
            outs.append(_rms_norm(o[:, lanes], gn_ref[...]) * _silu(r[:, lanes]))
        o_ref[sl, :] = jnp.concatenate(outs, axis=1).astype(BF16)
        return st

    st = st_scr[...]
    for ci in range(GLA_TB // c):
        st = chunk(ci, st)
    st_scr[...] = st

    @pl.when(pl.program_id(1) == pl.num_programs(1) - 1)
    def _():
        for h in range(GLA_HEADS):
            blk = st_scr[h * GLA_DV:(h + 1) * GLA_DV, :]
            st_ref[0, h] = blk.T[h * GLA_DK:(h + 1) * GLA_DK, :]


def _gla(h, wg, bg, gn, consts, batch, seq):
    nt = seq // GLA_TB
    tri, hm, mask, bd = consts
    rowblk = lambda w, cb: pl.BlockSpec((GLA_TB, w), lambda b, t: (b * nt + t, cb))
    return pl.pallas_call(
        _gla_kernel,
        grid=(batch, nt),
        in_specs=[rowblk(LANES, 5), rowblk(HK, 3), rowblk(HV, 2), rowblk(HV, 3), rowblk(HK, 8),
                  _resident(wg.shape), _resident(bg.shape), _resident(gn.shape),
                  _resident(tri.shape), _resident(hm.shape), _resident(mask.shape), _resident(bd.shape)],
        out_specs=[pl.BlockSpec((GLA_TB, HV), lambda b, t: (b * nt + t, 0)),
                   pl.BlockSpec((1, GLA_HEADS, GLA_DK, GLA_DV), lambda b, t: (b, 0, 0, 0))],
        out_shape=[jax.ShapeDtypeStruct((batch * seq, HV), BF16),
                   jax.ShapeDtypeStruct((batch, GLA_HEADS, GLA_DK, GLA_DV), F32)],
        scratch_shapes=[pltpu.VMEM((HV, HK), F32)],
        compiler_params=_cparams(("parallel", "arbitrary"), VMEM_LIMIT),
        name="gla",
    )(h, h, h, h, h, wg, bg, gn, tri, hm, mask, bd)


def _alibi_slope(group, head):
    return 2.0 ** (-8.0 * (group * DIL_HPG + head + 1) / (DIL_GROUPS * DIL_HPG))


DIL_TM = 512
QKV_G = 3 * MIX_C


def _proj_dil_kernel(x_ref, w_ref, *rest, n_prev, tiles):
    p_refs = rest[n_prev:n_prev + DIL_GROUPS]
    r1_ref, r2_ref, r3_ref, acc_ref = rest[n_prev + DIL_GROUPS:]
    t = pl.program_id(0) % tiles
    xb = x_ref[...].astype(BF16)
    nb = QKV_G // LANES
    for g, r_ref in enumerate((r1_ref, r2_ref, r3_ref)):
        window, dil = DIL_PAIRS[g]
        qkv = [_dot(xb, w_ref[:, (which * DIL_GROUPS + g) * MIX_C:(which * DIL_GROUPS + g + 1) * MIX_C])
               for which in range(3)]

        def put_rows(lo, width, g=g, qkv=qkv):
            for which in range(2):
                p_refs[g][which] = qkv[1 + which][lo:lo + width, :].T.reshape(DIL_HPG, DIL_HD, width)

        if window >= DIL_TM:
            keep_tiles = window // DIL_TM
            if keep_tiles >= tiles:
                put_rows(0, DIL_TM)
            else:
                pl.when(t >= tiles - keep_tiles)(lambda f=put_rows: f(0, DIL_TM))
        else:
            pl.when(t == tiles - 1)(lambda f=put_rows, w=window: f(DIL_TM - w, w))
        if dil == 1:
            for which in range(3):
                r_ref[:, which * MIX_C:(which + 1) * MIX_C] = qkv[which].astype(BF16)
            continue
        per = MIX_C // LANES
        for c in range(nb):
            acc_ref[c] = qkv[c // per][:, (c % per) * LANES:(c % per + 1) * LANES]
        for r in range(dil):
            for c in range(nb):
                r_ref[r, :, c * LANES:(c + 1) * LANES] = acc_ref[c, pl.ds(r, DIL_TM // dil, stride=dil), :].astype(BF16)


def _proj_dil(x, w, seq, prev, layer, n_layers):
    m, k = x.shape
    tm = DIL_TM
    nt = m // tm
    tiles = seq // tm
    res = lambda dil: ((m // seq, dil, seq // dil, QKV_G),
                       pl.BlockSpec((None, dil, tm // dil, QKV_G), lambda i: (i // tiles, 0, i % tiles, 0)))
    (s2, b2), (s3, b3) = res(DIL_PAIRS[1][1]), res(DIL_PAIRS[2][1])
    p_shapes, p_specs = [], []
    for window, _ in DIL_PAIRS:
        keep = min(window, seq)
        width = min(keep, tm)
        first = tiles - max(keep // tm, 1)
        p_shapes.append(jax.ShapeDtypeStruct((n_layers, m // seq, 2, DIL_HPG, DIL_HD, keep), F32))
        p_specs.append(pl.BlockSpec(
            (None, None, 2, DIL_HPG, DIL_HD, width),
            lambda i, first=first: (layer, i // tiles, 0, 0, 0, jnp.maximum(i % tiles - first, 0))))
    in_specs = [pl.BlockSpec((tm, k), lambda i: (i, 0)), _resident(w.shape)]
    args = [x, w]
    aliases = {}
    if prev is not None:
        in_specs += [pl.BlockSpec(memory_space=pl.ANY)] * DIL_GROUPS
        aliases = {len(args) + g: g for g in range(DIL_GROUPS)}
        args += list(prev)
    outs = pl.pallas_call(
        functools.partial(_proj_dil_kernel, n_prev=len(aliases), tiles=tiles),
        grid=(nt,),
        in_specs=in_specs,
        out_specs=p_specs + [pl.BlockSpec((tm, QKV_G), lambda i: (i, 0)), b2, b3],
        out_shape=p_shapes + [jax.ShapeDtypeStruct((m, QKV_G), BF16),
                              jax.ShapeDtypeStruct(s2, BF16), jax.ShapeDtypeStruct(s3, BF16)],
        scratch_shapes=[pltpu.VMEM((QKV_G // LANES, tm, LANES), F32)],
        input_output_aliases=aliases,
        compiler_params=_cparams(("arbitrary",), VMEM_LIMIT),
        name="proj_dil",
    )(*args)
    return outs[:DIL_GROUPS], outs[DIL_GROUPS:]


DIL_QB = 4


def _dil_attn_kernel(q_ref, k_ref, v_ref, *rest, group, dil, mode):
    n = DIL_STEPS
    if mode == "chain":
        kp_ref, vp_ref, o_ref, lse_ref = rest
    else:
        o_ref, lse_ref = rest
    nk = 2 * n
    qi = lax.broadcasted_iota(jnp.int32, (n, nk), 0)
    ki = lax.broadcasted_iota(jnp.int32, (n, nk), 1)
    steps = qi + n - ki
    band = (steps >= 0) & (steps <= n)
    dist = (steps * dil).astype(F32)
    bias = [jnp.where(band, -_alibi_slope(group, h) * dist, NEG_INF) for h in range(DIL_HPG)]
    lane = lax.broadcasted_iota(jnp.int32, (n, LANES), 1)
    low = lane < DIL_HD
    for blk in range(DIL_QB):
        cur = slice(blk * n, (blk + 1) * n)
        if mode == "single":
            q = q_ref[blk]
            k = jnp.concatenate([k_ref[blk], k_ref[blk]], axis=0)
            v = jnp.concatenate([v_ref[blk], v_ref[blk]], axis=0)
            own = ki >= n
        else:
            q = q_ref[cur, :]
            if blk > 0:
                k, v = k_ref[(blk - 1) * n:(blk + 1) * n, :], v_ref[(blk - 1) * n:(blk + 1) * n, :]
                own = None
            elif mode == "chain":
                k = jnp.concatenate([kp_ref[...], k_ref[cur, :]], axis=0)
                v = jnp.concatenate([vp_ref[...], v_ref[cur, :]], axis=0)
                own = (ki >= n) | (pl.program_id(1) > 0)
            else:
                k = jnp.concatenate([k_ref[cur, :], k_ref[cur, :]], axis=0)
                v = jnp.concatenate([v_ref[cur, :], v_ref[cur, :]], axis=0)
                own = ki >= n
        o_all = []
        lse = jnp.zeros((n, LANES), F32)
        for j in range(DIL_HPG // 2):
            cols = slice(j * LANES, (j + 1) * LANES)
            q2, k2, v2 = q[:, cols], k[:, cols], v[:, cols]
            outs = []
            for par in range(2):
                qh = jnp.where(low if par == 0 else ~low, q2, jnp.zeros_like(q2))
                s = _dot_nt(qh, k2) + bias[2 * j + par]
                if own is not None:
                    s = jnp.where(own, s, NEG_INF)
                m = jnp.max(s, axis=-1, keepdims=True)
                p = jnp.exp(s - m)
                l = jnp.sum(p, axis=-1, keepdims=True)
                outs.append(_dot(p.astype(BF16), v2) / l)
                lse = jnp.where(lane == 2 * j + par, m + jnp.log(l), lse)
            o_all.append(jnp.where(low, outs[0], outs[1]))
        dst = (blk,) if mode == "single" else (cur, slice(None))
        o_ref[dst] = jnp.concatenate(o_all, axis=1).astype(BF16)
        lse_ref[dst] = lse


def _dil_attn(qkv_r, group):
    n = DIL_STEPS
    batch, dil, ls, _ = qkv_r.shape
    rows = DIL_QB * n
    if ls == n:
        mode, steps = "single", dil // DIL_QB
        spec = lambda c, w=MIX_C: pl.BlockSpec((None, DIL_QB, n, w), lambda b, s: (b, s, 0, c))
        extra = []
    elif ls == rows:
        mode, steps = "whole", dil
        spec = lambda c, w=MIX_C: pl.BlockSpec((None, None, rows, w), lambda b, s: (b, s, 0, c))
        extra = []
    else:
        assert dil == 1 and ls % rows == 0
        mode, steps = "chain", ls // rows
        spec = lambda c, w=MIX_C: pl.BlockSpec((None, None, rows, w), lambda b, s: (b, 0, s, c))
        extra = [pl.BlockSpec((None, None, n, MIX_C), lambda b, s, c=c: (b, 0, jnp.maximum(s * DIL_QB - 1, 0), c))
                 for c in (1, 2)]
    in_specs = [spec(0), spec(1), spec(2)] + extra
    return pl.pallas_call(
        functools.partial(_dil_attn_kernel, group=group, dil=DIL_PAIRS[group][1], mode=mode),
        grid=(batch, steps),
        in_specs=in_specs,
        out_specs=[spec(0), spec(0, LANES)],
        out_shape=[jax.ShapeDtypeStruct((batch, dil, ls, MIX_C), BF16),
                   jax.ShapeDtypeStruct((batch, dil, ls, LANES), F32)],
        compiler_params=_cparams(("parallel", "arbitrary"), VMEM_LIMIT),
        name="dil_attn%d" % group,
    )(*([qkv_r] * len(in_specs)))


def _dil_mix_ln_kernel(o1, o2, o3, l1, l2, l3, e_ref, w_ref, x_ref, g_ref, b_ref, o_ref, so2, so3, sl2, sl3):
    for src, dst, dil in ((o2, so2, DIL_PAIRS[1][1]), (l2, sl2, DIL_PAIRS[1][1]),
                          (o3, so3, DIL_PAIRS[2][1]), (l3, sl3, DIL_PAIRS[2][1])):
        for r in range(dil):
            for c in range(dst.shape[0]):
                dst[c, pl.ds(r, DIL_TM // dil, stride=dil), :] = src[r, :, c * LANES:(c + 1) * LANES].astype(F32)
    whole = lambda s: jnp.concatenate([s[c] for c in range(s.shape[0])], axis=1)
    os_ = [o1[...].astype(F32), whole(so2), whole(so3)]
    ls = [l1[...], sl2[0], sl3[0]]
    mx = jnp.maximum(jnp.maximum(ls[0], ls[1]), ls[2])
    es = [jnp.exp(l - mx) for l in ls]
    den = es[0] + es[1] + es[2]

    def spread(wt):
        hi = wt.astype(BF16)
        lo = (wt - hi.astype(F32)).astype(BF16)
        return _dot(hi, e_ref[...]) + _dot(lo, e_ref[...])

    mix = sum(spread(e / den) * o for e, o in zip(es, os_))
    z = DN_ALPHA * x_ref[...] + _dot(mix.astype(BF16), w_ref[...])
    o_ref[...] = _layer_norm(z, g_ref[...], b_ref[...])


def _dil_mix_ln(os_, lses, w, x, g, b):
    m = x.shape[0]
    tm = DIL_TM
    nt = m // tm
    tiles = nt // os_[0].shape[0]
    row = lambda wd: pl.BlockSpec((tm, wd), lambda i: (i, 0))
    res = lambda a: pl.BlockSpec((None, a.shape[1], tm // a.shape[1], a.shape[3]),
                                 lambda i: (i // tiles, 0, i % tiles, 0))
    flat = lambda a: a.reshape(m, a.shape[3])
    spread = np.kron(np.eye(DIL_HPG), np.ones((1, DIL_HD)))
    spread = jnp.asarray(np.concatenate([spread, np.zeros((LANES - DIL_HPG, MIX_C))]), BF16)
    blocks = lambda wd: pltpu.VMEM((wd // LANES, tm, LANES), F32)
    return pl.pallas_call(
        _dil_mix_ln_kernel,
        grid=(nt,),
        in_specs=[row(MIX_C), res(os_[1]), res(os_[2]), row(LANES), res(lses[1]), res(lses[2]), _resident(spread.shape),
                  _resident(w.shape), row(D_MODEL), _row_of(g), _row_of(b)],
        out_specs=row(D_MODEL),
        out_shape=jax.ShapeDtypeStruct((m, D_MODEL), F32),
        scratch_shapes=[blocks(MIX_C), blocks(MIX_C), blocks(LANES), blocks(LANES)],
        compiler_params=_cparams(("parallel",), VMEM_LIMIT),
        name="dil_mix_ln",
    )(flat(os_[0]), os_[1], os_[2], flat(lses[0]), lses[1], lses[2], spread, w, x, g[0], b[0])


def _mla_pre_s_kernel(h_ref, qn_ref, kvn_ref, wqn_ref, wqp_ref, wukt_ref, ck_ref, sk_ref, rows_ref, q_ref):
    h = h_ref[...]
    qn = _rms_norm(h[:, :MLA_Q_LORA], qn_ref[...]).astype(BF16)
    qnope = _dot(qn, wqn_ref[...]).astype(BF16)
    qp = _dot(qn, wqp_ref[...])
    cq = jnp.concatenate([ck_ref[...]] * MLA_HEADS, axis=1)
    sq = jnp.concatenate([sk_ref[...]] * MLA_HEADS, axis=1)
    qp = qp * cq + _roll_lanes_left(qp, 32) * sq
    ckv = _rms_norm(h[:, MLA_Q_LORA:MLA_Q_LORA + MLA_KV_LORA], kvn_ref[...])
    blk = h[:, 640:768]
    kpe = blk * ck_ref[...] + _roll_lanes_left(blk, 32) * sk_ref[...]
    rows_ref[:, :MLA_KV_LORA] = ckv
    rows_ref[:, MLA_KV_LORA:] = kpe[:, :MLA_ROPE]
    for hd in range(MLA_HEADS):
        cols = slice(hd * LANES, (hd + 1) * LANES)
        q_ref[hd, :, :MLA_KV_LORA] = _dot(qnope[:, cols], wukt_ref[hd])
        q_ref[hd, :, MLA_KV_LORA:] = qp[:, hd * LANES:hd * LANES + MLA_ROPE]


def _mla_pre_s(h, qn, kvn, wqn, wqp, wukt, tabs):
    m = h.shape[0]
    full = lambda a: _resident(a.shape)
    return pl.pallas_call(
        _mla_pre_s_kernel,
        grid=(1,),
        in_specs=[pl.BlockSpec((m, 768), lambda i: (0, 0)), full(qn), full(kvn), full(wqn), full(wqp), full(wukt)]
        + [full(t) for t in tabs[2:]],
        out_specs=[pl.BlockSpec((m, MLA_KV_DIM), lambda i: (0, 0)),
                   pl.BlockSpec((MLA_HEADS, m, MLA_KV_DIM), lambda i: (0, 0, 0))],
        out_shape=[jax.ShapeDtypeStruct((m, MLA_KV_DIM), F32), jax.ShapeDtypeStruct((MLA_HEADS, m, MLA_KV_DIM), F32)],
        compiler_params=_cparams(("arbitrary",), VMEM_LIMIT),
        name="mla_pre_s",
    )(h, qn, kvn, wqn, wqp, wukt, *tabs[2:])


DEC_PAGES = 64


def _mla_dec_kernel(pt_ref, q_ref, new_ref, *rest):
    pages = rest[:DEC_PAGES]
    o_ref, m_scr, l_scr, acc_scr = rest[DEC_PAGES:]
    step = pl.program_id(1)
    q = q_ref[0]
    qb = q.astype(BF16)

    @pl.when(step == 0)
    def _():
        new = new_ref[0]
        m_scr[...] = jnp.sum(q * new, axis=-1, keepdims=True) * MLA_SCALE
        l_scr[...] = jnp.ones_like(l_scr)
        acc_scr[...] = jnp.broadcast_to(new[:, :MLA_KV_LORA], acc_scr.shape)

    kvt = jnp.concatenate([p[...].astype(BF16) for p in pages], axis=1)
    s = _dot(qb, kvt) * MLA_SCALE
    m_old = m_scr[...]
    m_new = jnp.maximum(m_old, jnp.max(s, axis=-1, keepdims=True))
    alpha = jnp.exp(m_old - m_new)
    p = jnp.exp(s - m_new)
    m_scr[...] = m_new
    l_scr[...] = alpha * l_scr[...] + jnp.sum(p, axis=-1, keepdims=True)
    acc_scr[...] = alpha * acc_scr[...] + _dot_nt(p.astype(BF16), kvt[:MLA_KV_LORA, :])

    @pl.when(step == pl.num_programs(1) - 1)
    def _():
        o_ref[0] = acc_scr[...] / l_scr[...]


def _mla_dec(page_table, q_abs, rows_new, cache_t, layer):
    batch = q_abs.shape[0]
    n_pages = page_table.shape[1]
    steps = n_pages // DEC_PAGES
    pt = page_table.reshape(-1)

    def page_spec(j):
        return pl.BlockSpec((None, None, MLA_KV_DIM, PAGE_SIZE),
                            lambda b, s, pt_ref: (layer, pt_ref[b * n_pages + s * DEC_PAGES + j], 0, 0))

    grid_spec = pltpu.PrefetchScalarGridSpec(
        num_scalar_prefetch=1,
        grid=(batch, steps),
        in_specs=[pl.BlockSpec((1, MLA_HEADS, MLA_KV_DIM), lambda b, s, pt_ref: (b, 0, 0)),
                  pl.BlockSpec((1, 1, MLA_KV_DIM), lambda b, s, pt_ref: (b, 0, 0))]
        + [page_spec(j) for j in range(DEC_PAGES)],
        out_specs=pl.BlockSpec((1, MLA_HEADS, MLA_KV_LORA), lambda b, s, pt_ref: (b, 0, 0)),
        scratch_shapes=[pltpu.VMEM((MLA_HEADS, 1), F32), pltpu.VMEM((MLA_HEADS, 1), F32),
                        pltpu.VMEM((MLA_HEADS, MLA_KV_LORA), F32)],
    )
    return pl.pallas_call(
        _mla_dec_kernel,
        grid_spec=grid_spec,
        out_shape=jax.ShapeDtypeStruct((batch, MLA_HEADS, MLA_KV_LORA), F32),
        compiler_params=_cparams(("parallel", "arbitrary"), VMEM_LIMIT),
        name="mla_dec",
    )(pt, q_abs, rows_new.reshape(batch, 1, MLA_KV_DIM), *([cache_t] * DEC_PAGES))


GLA_DEC_B = 8


def _gla_dec_kernel(blk_ref, q_ref, v_ref, r_ref, k_ref, s_ref, wg_ref, bg_ref, gn_ref, o_ref, so_ref):
    pre = _dot(blk_ref[...].astype(BF16), wg_ref[...]) + bg_ref[...]
    a_all = jnp.exp(_log_sigmoid(pre) * (1.0 / GLA_TAU))
    eye = (lax.broadcasted_iota(jnp.int32, (GLA_DK, GLA_DK), 0)
           == lax.broadcasted_iota(jnp.int32, (GLA_DK, GLA_DK), 1))
    diag = lambda row: jnp.where(eye, jnp.broadcast_to(row, (GLA_DK, GLA_DK)), 0.0)
    for t in range(GLA_DEC_B):
        tok = slice(t, t + 1)
        a, k, v, r = a_all[tok], k_ref[tok, :], v_ref[tok, :], r_ref[tok, :]
        q = q_ref[tok, :]
        outs = []
        for h in range(GLA_HEADS):
            kl = slice(h * GLA_DK, (h + 1) * GLA_DK)
            vl = slice(h * GLA_DV, (h + 1) * GLA_DV)
            lhs = jnp.concatenate([diag(a[:, kl]), diag(k[:, kl])], axis=1)
            rhs = jnp.concatenate([s_ref[t, h], jnp.broadcast_to(v[:, vl], (GLA_DK, GLA_DV))], axis=0)
            s_new = jnp.dot(lhs, rhs, preferred_element_type=F32, precision=lax.Precision.HIGHEST)
            so_ref[t, h] = s_new
            o = jnp.dot(jnp.broadcast_to(q[:, kl], (8, GLA_DK)), s_new, preferred_element_type=F32,
                        precision=lax.Precision.HIGHEST)[:1]
            outs.append(_rms_norm(o, gn_ref[...]) * _silu(r[:, vl]))
        o_ref[tok, :] = jnp.concatenate(outs, axis=1)


def _gla_dec(h, state, layer, wg, bg, gn):
    batch = h.shape[0]
    tb = GLA_DEC_B
    rowblk = lambda w, cb: pl.BlockSpec((tb, w), lambda b: (b, cb))
    st_in = pl.BlockSpec((None, tb, GLA_HEADS, GLA_DK, GLA_DV), lambda b: (layer, b, 0, 0, 0))
    st = pl.BlockSpec((tb, GLA_HEADS, GLA_DK, GLA_DV), lambda b: (b, 0, 0, 0))
    return pl.pallas_call(
        _gla_dec_kernel,
        grid=(batch // tb,),
        in_specs=[rowblk(LANES, 5), rowblk(HK, 3), rowblk(HV, 2), rowblk(HV, 3), rowblk(HK, 8), st_in,
                  _resident(wg.shape), _resident(bg.shape), _resident(gn.shape)],
        out_specs=[pl.BlockSpec((tb, HV), lambda b: (b, 0)), st],
        out_shape=[jax.ShapeDtypeStruct((batch, HV), F32), jax.ShapeDtypeStruct(state.shape[1:], F32)],
        compiler_params=_cparams(("parallel",), VMEM_LIMIT),
        name="gla_dec",
    )(h, h, h, h, h, state, wg, bg, gn)


def _dil_dec_kernel(x_ref, c1_ref, c2_ref, c3_ref, o_ref):
    heads = range(DIL_HPG)
    hcol = lax.broadcasted_iota(jnp.int32, (DIL_HPG, 1), 0).astype(F32)
    outs, lses = [], []
    for g, c_ref in enumerate((c1_ref, c2_ref, c3_ref)):
        window, dil = DIL_PAIRS[g]
        pos = lax.broadcasted_iota(jnp.int32, (1, window), 1)
        q, k_new, v_new = (x_ref[0, w * DIL_GROUPS + g] for w in range(3))
        s = jnp.concatenate([jnp.sum(c_ref[0, h] * q[:, h:h + 1], axis=0, keepdims=True) for h in heads], axis=0)
        slope = jnp.exp((g * DIL_HPG + hcol + 1.0) * (-8.0 * math.log(2.0) / (DIL_GROUPS * DIL_HPG)))
        s = s - slope * (window - pos).astype(F32)
        s = jnp.where((pos & (dil - 1)) == 0, s, NEG_INF)
        qk = jnp.sum(q * k_new, axis=0, keepdims=True)
        s_new = jnp.concatenate([qk[:, h:h + 1] for h in heads], axis=0)
        m = jnp.maximum(jnp.max(s, axis=1, keepdims=True), s_new)
        p = jnp.exp(s - m)
        p_new = jnp.exp(s_new - m)
        l = jnp.sum(p, axis=1, keepdims=True) + p_new
        w = p / l
        w_new = p_new / l
        outs.append([jnp.sum(c_ref[1, h] * w[h:h + 1, :], axis=1, keepdims=True)
                     + w_new[h:h + 1, :] * v_new[:, h:h + 1] for h in heads])
        lses.append(m + jnp.log(l))
    mx = jnp.maximum(jnp.maximum(lses[0], lses[1]), lses[2])
    es = [jnp.exp(l - mx) for l in lses]
    den = es[0] + es[1] + es[2]
    ws = [e / den for e in es]
    o_ref[0] = jnp.concatenate(
        [sum(ws[g][h:h + 1, :] * outs[g][h] for g in range(DIL_GROUPS)) for h in heads], axis=1)


def _dil_dec(qkv_t, caches_t, layer):
    batch = qkv_t.shape[0]
    specs = [pl.BlockSpec((1, 3 * DIL_GROUPS, DIL_HD, DIL_HPG), lambda b: (b, 0, 0, 0))]
    for (window, _), c in zip(DIL_PAIRS, caches_t):
        assert c.shape[-1] == window
        specs.append(pl.BlockSpec((None, None, 2, DIL_HPG, DIL_HD, window), lambda b: (layer, b, 0, 0, 0, 0)))
    return pl.pallas_call(
        _dil_dec_kernel,
        grid=(batch,),
        in_specs=specs,
        out_specs=pl.BlockSpec((1, DIL_HD, DIL_HPG), lambda b: (b, 0, 0)),
        out_shape=jax.ShapeDtypeStruct((batch, DIL_HD, DIL_HPG), F32),
        compiler_params=_cparams(("parallel",), VMEM_LIMIT),
        name="dil_dec",
    )(qkv_t, *caches_t)


def _even_in_cols():
    o_cq, o_kv, o_gq, o_gk, o_gv, o_lr, o_gr = 0, 384, 672, 928, 1184, 1696, 1712
    r = np.arange
    kpe = o_kv + MLA_KV_LORA
    idx = np.concatenate([r(o_cq, o_cq + 384), r(o_kv, o_kv + 256), r(kpe, kpe + 32), r(kpe + 16, kpe + 32),
                          r(kpe, kpe + 16), r(o_lr, o_lr + 16), np.full(48, -1), r(o_gq, o_gq + 256),
                          r(o_gv, o_gv + 512), r(o_gr, o_gr + 512), r(o_gk, o_gk + 256)])
    assert idx.shape[0] == IN_A_PAD
    return idx


def _gather_cols(w, idx):
    idx = np.asarray(idx)
    a, b = idx[:-1], idx[1:]
    same_run = ((a >= 0) & (b >= 0) & (b - a == 1)) | ((a < 0) & (b < 0))
    cuts = np.flatnonzero(~same_run) + 1
    parts = []
    for run in np.split(idx, cuts):
        if run[0] < 0:
            parts.append(jnp.zeros((w.shape[0], run.size), w.dtype))
        else:
            parts.append(w[:, int(run[0]):int(run[-1]) + 1])
    return jnp.concatenate(parts, axis=1)


def _uq_cols(kind):
    idx = []
    for h in range(MLA_HEADS):
        b = h * (MLA_NOPE + MLA_ROPE)
        nope = np.arange(b, b + 64)
        x1, x2 = np.arange(b + 64, b + 80), np.arange(b + 80, b + 96)
        if kind == "full":
            idx += [nope, x1, x2, x2, x1]
        elif kind == "nope":
            idx += [nope, np.full(64, -1)]
        else:
            idx += [x1, x2, x2, x1, np.full(64, -1)]
    return np.concatenate(idx)


TM_DENSE = 1024
TM_WIDE = 256


def kernel(x_prompt, x_sample, cache_mla, state_gla, cache_dil_w128, cache_dil_w512, cache_dil_w2048, page_table,
           w_in_a, mla_q_norm, mla_w_uq, mla_kv_norm, mla_w_uk, mla_w_uv, gla_w_gate2, gla_b_gate, gla_norm, w_out_a,
           w_in_c, w_out_c, ffn_w_in, ffn_w_out, ln_g, ln_b):
    batch, seq, _ = x_prompt.shape
    dbatch = x_sample.shape[0]
    assert seq % MLA_TQ == 0 and seq % DIL_TM == 0 and seq >= DIL_PAIRS[-1][0]
    n_even, n_odd = (DEPTH + 1) // 2, DEPTH // 2
    cache_mla_t = jnp.transpose(cache_mla, (0, 1, 3, 2))
    dil_caches_t = [jnp.transpose(c, (0, 1, 3, 4, 5, 2)) for c in (cache_dil_w128, cache_dil_w512, cache_dil_w2048)]
    gla_consts = _gla_constants()
    tabs_p = _rope_tables(jnp.arange(seq, dtype=jnp.int32))
    tabs_s = _rope_tables(jnp.full((dbatch,), PAST_LEN, jnp.int32))
    even_cols = _even_in_cols()
    even_scale = np.ones((1, IN_A_PAD), np.float32)
    even_scale[:, 768:768 + HK] = GLA_DK ** -0.5
    odd_scale = np.ones((1, IN_C), np.float32)
    odd_scale[:, :DIL_GROUPS * MIX_C] = DIL_HD ** -0.5
    even_scale, odd_scale = jnp.asarray(even_scale), jnp.asarray(odd_scale)
    row2 = lambda v: v.reshape(1, -1).astype(F32)
    ln_g3 = ln_g.reshape(2 * DEPTH, 1, D_MODEL)
    ln_b3 = ln_b.reshape(2 * DEPTH, 1, D_MODEL)

    xp = x_prompt.reshape(batch * seq, D_MODEL)
    xs = x_sample.reshape(dbatch, D_MODEL)
    mla_s, gla_p, gla_s = [], [], []
    rows_t, dil_t = None, None
    dil_s = [[] for _ in DIL_PAIRS]

    for layer in range(DEPTH):
        i = layer // 2
        g0, b0, g1, b1 = (ln_g3, 2 * layer), (ln_b3, 2 * layer), (ln_g3, 2 * layer + 1), (ln_b3, 2 * layer + 1)
        if layer % 2 == 0:
            w_in = (_gather_cols(w_in_a[i], even_cols) * even_scale).astype(BF16)
            wq_full = _gather_cols(mla_w_uq[i], _uq_cols("full")).astype(BF16)
            wq_nope = _gather_cols(mla_w_uq[i], _uq_cols("nope")).astype(BF16)
            wq_rope = _gather_cols(mla_w_uq[i], _uq_cols("rope")).astype(BF16)
            wuk_pad = jnp.pad(mla_w_uk[i], ((0, 0), (0, 0), (0, LANES - MLA_NOPE))).reshape(MLA_KV_LORA, -1).astype(BF16)
            wuk_t = jnp.pad(jnp.transpose(mla_w_uk[i], (1, 2, 0)), ((0, 0), (0, LANES - MLA_NOPE), (0, 0))).astype(BF16)
            wuv_t = mla_w_uv[i].reshape(MLA_KV_LORA, -1).T.astype(BF16)
            wuv_bd = (jnp.eye(MLA_HEADS, dtype=F32)[:, None, :, None]
                      * jnp.transpose(mla_w_uv[i], (1, 0, 2))[:, :, None, :]).reshape(
                          MLA_HEADS * MLA_KV_LORA, MLA_HEADS * MLA_V).astype(BF16)
            wg = jnp.pad(gla_w_gate2[i], ((64, LANES - 64 - GLA_GATE_RANK), (0, 0))).astype(BF16)
            bg, gn = row2(gla_b_gate[i]), row2(gla_norm[i])
            qn, kvn = row2(mla_q_norm[i]), row2(mla_kv_norm[i])
            w_out = w_out_a[i].astype(BF16)

            h = _proj(xp, w_in, TM_DENSE)
            rows_t, qf, kf, vt = _mla_pre(h, qn, kvn, wq_full, wuk_pad, wuv_t, tabs_p, seq, rows_t, i, n_even)
            mla_out = _mla_attn(qf, kf, vt, batch, seq)
            gla_out, s_fin = _gla(h, wg, bg, gn, gla_consts, batch, seq)
            gla_p.append(s_fin)
            xp = _mix_ln([mla_out, gla_out], w_out, xp, g0, b0, TM_DENSE)

            h = _proj(xs, w_in, TM_DENSE)
            rows, q_abs = _mla_pre_s(h, qn, kvn, wq_nope, wq_rope, wuk_t, tabs_s)
            lat = _mla_dec(page_table, jnp.transpose(q_abs, (1, 0, 2)), rows, cache_mla_t, i)
            mla_out = _proj(lat.reshape(dbatch, MLA_HEADS * MLA_KV_LORA), wuv_bd, TM_DENSE)
            gla_out, s_fin = _gla_dec(h, state_gla, i, wg, bg, gn)
            mla_s.append(rows.reshape(dbatch, 1, MLA_KV_DIM))
            gla_s.append(s_fin)
            xs = _mix_ln([mla_out, gla_out], w_out, xs, g0, b0, TM_DENSE)
        else:
            w_in = (w_in_c[i] * odd_scale).astype(BF16)
            w_out = w_out_c[i].astype(BF16)

            dil_t, (r1, r2, r3) = _proj_dil(xp, w_in, seq, dil_t, i, n_odd)
            res = [r1.reshape(batch, 1, seq, QKV_G), r2, r3]
            os_, lses = zip(*[_dil_attn(res[g], g) for g in range(DIL_GROUPS)])
            xp = _dil_mix_ln(os_, lses, w_out, xp, g0, b0)

            qkv = _proj(xs, w_in, TM_WIDE).reshape(dbatch, 3 * DIL_GROUPS, DIL_HPG, DIL_HD)
            mix = _dil_dec(jnp.transpose(qkv, (0, 1, 3, 2)), dil_caches_t, i)
            mix = jnp.transpose(mix, (0, 2, 1)).reshape(dbatch, MIX_C)
            for g in range(DIL_GROUPS):
                dil_s[g].append(jnp.stack([qkv[:, DIL_GROUPS + g], qkv[:, 2 * DIL_GROUPS + g]], axis=1)
                                .reshape(dbatch, 1, 2, DIL_HPG, DIL_HD))
            xs = _mix_ln([mix], w_out, xs, g0, b0, TM_DENSE)

        wi, wo = ffn_w_in[layer].astype(BF16), ffn_w_out[layer].astype(BF16)
        xp = _ffn(xp, wi, wo, g1, b1, TM_FFN)
        xs = _ffn(xs, wi, wo, g1, b1, TM_FFN)

    st = jnp.stack
    mla_p = jnp.transpose(rows_t, (0, 1, 3, 2))
    dil_p = [jnp.transpose(p, (0, 1, 5, 2, 3, 4)) for p in dil_t]
    return (xp.reshape(batch, seq, D_MODEL), xs.reshape(dbatch, 1, D_MODEL), mla_p, st(mla_s), st(gla_p), st(gla_s),
            dil_p[0], st(dil_s[0]), dil_p[1], st(dil_s[1]), dil_p[2], st(dil_s[2]))
```

```python
import functools
import math

import numpy as np
import jax
import jax.numpy as jnp
from jax import lax
from jax.experimental import pallas as pl
from jax.experimental.pallas import tpu as pltpu

F32 = jnp.float32
BF16 = jnp.bfloat16

D_MODEL = 1024
DEPTH = 4
PAST_LEN = 16384
PAGE_SIZE = 128
MLA_HEADS = 8
MLA_Q_LORA = 384
MLA_KV_LORA = 256
MLA_NOPE = 64
MLA_ROPE = 32
MLA_V = 64
MLA_KV_DIM = MLA_KV_LORA + MLA_ROPE
ROPE_BASE = 10000.0
GLA_HEADS = 4
GLA_DK = 64
GLA_DV = 128
GLA_GATE_RANK = 16
GLA_TAU = 16.0
GLA_CHUNK = 64
DIL_PAIRS = ((128, 1), (512, 4), (2048, 16))
DIL_GROUPS = 3
DIL_HPG = 8
DIL_HD = 64
DIL_STEPS = 128
D_FF = -(-8 * D_MODEL // (3 * 256)) * 256
DN_ALPHA = (2.0 * DEPTH) ** 0.25
IN_A_PAD = 2304
MIX_C = DIL_HPG * DIL_HD
IN_C = 3 * DIL_GROUPS * MIX_C
MLA_SCALE = (MLA_NOPE + MLA_ROPE) ** -0.5

V7X_VMEM_BYTES = 64 * 1024 * 1024
VMEM_LIMIT = 56 * 1024 * 1024
LANES = 128

NEG_INF = float("-inf")


def _cparams(sem, limit=None):
    return pltpu.CompilerParams(dimension_semantics=sem, vmem_limit_bytes=limit)


def _resident(shape):
    nd = len(shape)
    return pl.BlockSpec(shape, lambda *_: (0,) * nd, pipeline_mode=pl.Buffered(1))


def _row_of(p):
    stack, idx = p
    return pl.BlockSpec((None, 1, stack.shape[-1]), lambda *_: (idx, 0, 0), pipeline_mode=pl.Buffered(1))


def _layer_norm(z, g, b):
    mu = jnp.mean(z, axis=-1, keepdims=True)
    zc = z - mu
    var = jnp.mean(zc * zc, axis=-1, keepdims=True)
    return zc * lax.rsqrt(var + 1e-5) * g + b


def _rms_norm(z, g):
    return z * lax.rsqrt(jnp.mean(z * z, axis=-1, keepdims=True) + 1e-6) * g


def _silu(z):
    return z * (1.0 / (1.0 + jnp.exp(-z)))


def _dot(a, b):
    return jnp.dot(a, b, preferred_element_type=F32)


def _dot_nt(a, b):
    return lax.dot_general(a, b, (((1,), (1,)), ((), ())), preferred_element_type=F32)


def _dot_tn(a, b):
    return lax.dot_general(a, b, (((0,), (0,)), ((), ())), preferred_element_type=F32)


def _proj_kernel(x_ref, w_ref, o_ref):
    o_ref[...] = _dot(x_ref[...].astype(BF16), w_ref[...]).astype(o_ref.dtype)


def _proj(x, w, tm, out_dtype=F32):
    m, k = x.shape
    n = w.shape[1]
    tm = min(tm, m)
    return pl.pallas_call(
        _proj_kernel,
        grid=(m // tm,),
        in_specs=[pl.BlockSpec((tm, k), lambda i: (i, 0)), _resident((k, n))],
        out_specs=pl.BlockSpec((tm, n), lambda i: (i, 0)),
        out_shape=jax.ShapeDtypeStruct((m, n), out_dtype),
        compiler_params=_cparams(("parallel",), VMEM_LIMIT),
        name="proj",
    )(x, w)


def _mix_ln_kernel(*refs, n_parts):
    parts = refs[:n_parts]
    w_ref, x_ref, g_ref, b_ref, o_ref = refs[n_parts:]
    a = jnp.concatenate([p[...].astype(BF16) for p in parts], axis=-1) if n_parts > 1 else parts[0][...].astype(BF16)
    z = DN_ALPHA * x_ref[...] + _dot(a, w_ref[...])
    o_ref[...] = _layer_norm(z, g_ref[...], b_ref[...])


def _mix_ln(parts, w, x, g, b, tm):
    m = x.shape[0]
    tm = min(tm, m)
    kern = functools.partial(_mix_ln_kernel, n_parts=len(parts))
    return pl.pallas_call(
        kern,
        grid=(m // tm,),
        in_specs=[pl.BlockSpec((tm, p.shape[1]), lambda i: (i, 0)) for p in parts]
        + [_resident(w.shape), pl.BlockSpec((tm, D_MODEL), lambda i: (i, 0)), _row_of(g), _row_of(b)],
        out_specs=pl.BlockSpec((tm, D_MODEL), lambda i: (i, 0)),
        out_shape=jax.ShapeDtypeStruct((m, D_MODEL), F32),
        compiler_params=_cparams(("parallel",), VMEM_LIMIT),
        name="mix_ln",
    )(*parts, w, x, g[0], b[0])


FF_CHUNK = 256
TM_FFN = 1024


def _ffn_kernel(x_ref, wi_ref, wo_ref, g_ref, b_ref, o_ref, act_ref):
    x = x_ref[...]
    xb = x.astype(BF16)
    for c in range(D_FF // FF_CHUNK):
        lo = c * FF_CHUNK
        gate = _dot(xb, wi_ref[:, lo:lo + FF_CHUNK])
        up = _dot(xb, wi_ref[:, D_FF + lo:D_FF + lo + FF_CHUNK])
        act_ref[:, lo:lo + FF_CHUNK] = (_silu(gate) * up).astype(BF16)
    z = DN_ALPHA * x + _dot(act_ref[...], wo_ref[...])
    o_ref[...] = _layer_norm(z, g_ref[...], b_ref[...])


def _ffn(x, wi, wo, g, b, tm):
    m = x.shape[0]
    tm = min(tm, m)
    return pl.pallas_call(
        _ffn_kernel,
        grid=(m // tm,),
        in_specs=[pl.BlockSpec((tm, D_MODEL), lambda i: (i, 0)), _resident(wi.shape), _resident(wo.shape),
                  _row_of(g), _row_of(b)],
        out_specs=pl.BlockSpec((tm, D_MODEL), lambda i: (i, 0)),
        out_shape=jax.ShapeDtypeStruct((m, D_MODEL), F32),
        scratch_shapes=[pltpu.VMEM((tm, D_FF), BF16)],
        compiler_params=_cparams(("parallel",), VMEM_LIMIT),
        name="ffn",
    )(x, wi, wo, g[0], b[0])


def _rope_tables(pos):
    half = MLA_ROPE // 2
    inv = ROPE_BASE ** (-jnp.arange(half, dtype=F32) / half)
    ang = pos.astype(F32)[:, None] * inv[None, :]
    cos, sin = jnp.cos(ang), jnp.sin(ang)
    n = pos.shape[0]
    z = lambda w: jnp.zeros((n, w), F32)
    cc = jnp.concatenate([cos, cos], axis=1)
    ss = jnp.concatenate([-sin, sin], axis=1)
    cq = jnp.concatenate([jnp.ones((n, MLA_NOPE), F32), cc, z(32)], axis=1)
    sq = jnp.concatenate([z(MLA_NOPE), ss, z(32)], axis=1)
    ck = jnp.concatenate([cc, z(96)], axis=1)
    sk = jnp.concatenate([ss, z(96)], axis=1)
    return cq, sq, ck, sk


def _roll_lanes_left(x, k):
    return pltpu.roll(x, x.shape[-1] - k, axis=x.ndim - 1)


def _mla_pre_kernel(h_ref, qn_ref, kvn_ref, wq_ref, wuk_ref, wuvt_ref, cq_ref, sq_ref, ck_ref, sk_ref, *rest):
    rows_ref, q_ref, k_ref, vt_ref = rest[-4:]
    h = h_ref[...]
    qn = _rms_norm(h[:, :MLA_Q_LORA], qn_ref[...]).astype(BF16)
    q = _dot(qn, wq_ref[...])
    cq = jnp.concatenate([cq_ref[...]] * MLA_HEADS, axis=1)
    sq = jnp.concatenate([sq_ref[...]] * MLA_HEADS, axis=1)
    q_ref[...] = ((q * cq + _roll_lanes_left(q, 32) * sq) * MLA_SCALE).astype(BF16)
    ckv = _rms_norm(h[:, MLA_Q_LORA:MLA_Q_LORA + MLA_KV_LORA], kvn_ref[...])
    blk = h[:, 640:768]
    kpe = blk * ck_ref[...] + _roll_lanes_left(blk, 32) * sk_ref[...]
    rows_ref[:MLA_KV_LORA, :] = ckv.T
    rows_ref[MLA_KV_LORA:, :] = kpe.T[:MLA_ROPE, :]
    ckvb = ckv.astype(BF16)
    kpe_mid = pltpu.roll(kpe, 64, axis=1)
    k_ref[...] = (_dot(ckvb, wuk_ref[...]) + jnp.concatenate([kpe_mid] * MLA_HEADS, axis=1)).astype(BF16)
    vt_ref[0] = _dot_nt(wuvt_ref[...], ckvb).astype(BF16)


MLA_TQ = 512


def _mla_pre(h, qn, kvn, wq, wuk, wuvt, tabs, seq, rows_t, layer, n_layers):
    m = h.shape[0]
    tm = MLA_TQ
    nblk = seq // tm
    tab = pl.BlockSpec((tm, LANES), lambda i: (i % nblk, 0))
    row = lambda w: pl.BlockSpec((tm, w), lambda i: (i, 0))
    hv = MLA_HEADS * MLA_V
    in_specs = [pl.BlockSpec((tm, 768), lambda i: (i, 0)), _resident(qn.shape), _resident(kvn.shape),
                _resident(wq.shape), _resident(wuk.shape), _resident(wuvt.shape), tab, tab, tab, tab]
    args = [h, qn, kvn, wq, wuk, wuvt, *tabs]
    aliases = {}
    if rows_t is not None:
        in_specs.append(pl.BlockSpec(memory_space=pl.ANY))
        args.append(rows_t)
        aliases = {len(args) - 1: 0}
    return pl.pallas_call(
        _mla_pre_kernel,
        grid=(m // tm,),
        in_specs=in_specs,
        out_specs=[pl.BlockSpec((None, None, MLA_KV_DIM, tm), lambda i: (layer, i // nblk, 0, i % nblk)),
                   row(1024), row(1024), pl.BlockSpec((1, hv, tm), lambda i: (i, 0, 0))],
        out_shape=[jax.ShapeDtypeStruct((n_layers, m // seq, MLA_KV_DIM, seq), F32),
                   jax.ShapeDtypeStruct((m, 1024), BF16),
                   jax.ShapeDtypeStruct((m, 1024), BF16), jax.ShapeDtypeStruct((m // tm, hv, tm), BF16)],
        input_output_aliases=aliases,
        compiler_params=_cparams(("arbitrary",), VMEM_LIMIT),
        name="mla_pre",
    )(*args)


def _mla_attn_kernel(q_ref, k_ref, vt_ref, o_ref):
    t = MLA_TQ
    nq = q_ref.shape[0] // t
    krow = lax.broadcasted_iota(jnp.int32, (t, t), 0)
    qcol = lax.broadcasted_iota(jnp.int32, (t, t), 1)
    ones = jnp.ones((8, t), BF16)
    for qi in range(nq):
        outs = []
        for j in range(2):
            qh = q_ref[qi * t:(qi + 1) * t, j * LANES:(j + 1) * LANES]
            m = acc = None
            for kb in range(qi + 1):
                s = _dot_nt(k_ref[kb * t:(kb + 1) * t, j * LANES:(j + 1) * LANES], qh)
                if kb == qi:
                    s = jnp.where(krow <= qcol, s, NEG_INF)
                bm = jnp.max(s, axis=0, keepdims=True)
                m_new = bm if m is None else jnp.maximum(m, bm)
                p = jnp.exp(s - m_new).astype(BF16)
                vt1 = jnp.concatenate([vt_ref[kb, j * MLA_V:(j + 1) * MLA_V, :], ones], axis=0)
                pv = _dot(vt1, p)
                acc = pv if m is None else jnp.exp(m - m_new) * acc + pv
                m = m_new
            outs.append(acc[:MLA_V] / acc[MLA_V:MLA_V + 1])
        o_ref[qi * t:(qi + 1) * t, :] = jnp.concatenate(outs, axis=0).T.astype(BF16)


def _mla_attn(q, k, vt, batch, seq):
    nq = seq // MLA_TQ
    vt4 = vt.reshape(batch, nq, MLA_HEADS * MLA_V, MLA_TQ)
    return pl.pallas_call(
        _mla_attn_kernel,
        grid=(batch, MLA_HEADS // 2),
        in_specs=[pl.BlockSpec((seq, 256), lambda b, p: (b, p)),
                  pl.BlockSpec((seq, 256), lambda b, p: (b, p)),
                  pl.BlockSpec((None, nq, 2 * MLA_V, MLA_TQ), lambda b, p: (b, 0, p, 0))],
        out_specs=pl.BlockSpec((seq, LANES), lambda b, p: (b, p)),
        out_shape=jax.ShapeDtypeStruct((batch * seq, MLA_HEADS * MLA_V), BF16),
        compiler_params=_cparams(("parallel", "parallel"), VMEM_LIMIT),
        name="mla_attn",
    )(q, k, vt4)


GLA_TB = 1024
GLA_SLOTS = 12
HK = GLA_HEADS * GLA_DK
HV = GLA_HEADS * GLA_DV


def _gla_constants():
    c = GLA_CHUNK
    t = np.arange(c)[:, None]
    s = np.arange(c)[None, :]
    slots = []
    zero = np.zeros((c, c), bool)
    for p in (1, 2, 3):
        slots.append((t // 16 == p) & (s < 16 * p))
    slots.append(zero)
    for p in (1, 2, 3):
        slots.append((t // 16 == s // 16) & ((t % 16) // 4 == p) & (s % 16 < 4 * p))
    slots.append(zero)
    for d in (1, 2, 3):
        slots.append((t == s + d) & (t // 4 == s // 4))
    slots.append(t == s)
    mask = np.concatenate(slots, axis=1).astype(np.float32)
    assert (sum(slots) == (t >= s)).all()
    mask = np.tile(mask, (GLA_HEADS, 1))
    tri = (t >= s).astype(np.float32)
    hm = np.kron(np.eye(GLA_HEADS), np.ones((c, GLA_DK))).astype(np.float32)
    bd = np.kron(np.eye(GLA_HEADS), np.ones((GLA_DV, GLA_DK))).astype(np.float32)
    return (jnp.asarray(tri, BF16), jnp.asarray(hm, BF16), jnp.asarray(mask, BF16), jnp.asarray(bd, F32))


def _log_sigmoid(x):
    return jnp.minimum(x, 0.0) - jnp.log1p(jnp.exp(-jnp.abs(x)))


def _split3(x):
    hi = x.astype(BF16)
    r1 = x - hi.astype(F32)
    mid = r1.astype(BF16)
    lo = (r1 - mid.astype(F32)).astype(BF16)
    return hi, mid, lo


def _bc_rows(row, n):
    return jnp.broadcast_to(row, (n, row.shape[1]))


def _gla_kernel(blk_ref, q_ref, v_ref, r_ref, k_ref, wg_ref, bg_ref, gn_ref, tri_ref, hm_ref, mask_ref, bd_ref,
                o_ref, st_ref, st_scr):
    c = GLA_CHUNK

    @pl.when(pl.program_id(1) == 0)
    def _():
        st_scr[...] = jnp.zeros_like(st_scr)

    rows = lax.broadcasted_iota(jnp.int32, (c, HK), 0)
    rm4 = rows & 3

    def chunk(ci, st):
        sl = slice(ci * c, (ci + 1) * c)
        pre = _dot(blk_ref[sl, :].astype(BF16), wg_ref[...]) + bg_ref[...]
        g = _log_sigmoid(pre) * (1.0 / GLA_TAU)
        cum3 = _dot(tri_ref[...], jnp.concatenate(_split3(g), axis=1))
        cum = cum3[:, :HK] + cum3[:, HK:2 * HK] + cum3[:, 2 * HK:]
        q = q_ref[sl, :]
        k = k_ref[sl, :]
        vb = v_ref[sl, :].astype(BF16)

        def kvar(e):
            return (k * jnp.exp(jnp.minimum(e, 0.0))).astype(BF16)

        zeros_k = jnp.zeros((c, HK), BF16)
        bnd = [cum[16 * p - 1:16 * p, :] for p in (1, 2, 3)]
        base1 = jnp.concatenate([jnp.zeros((16, HK), F32)] + [_bc_rows(b, 16) for b in bnd], axis=0)
        q1 = q * jnp.exp(cum - base1)
        k1 = jnp.concatenate([kvar(_bc_rows(b, c) - cum) for b in bnd] + [zeros_k], axis=0)
        sh = [pltpu.roll(cum, d, axis=0) for d in (1, 2, 3, 4)]
        base2 = jnp.where(rm4 == 0, sh[0], jnp.where(rm4 == 1, sh[1], jnp.where(rm4 == 2, sh[2], sh[3])))
        base2 = jnp.where(rows < 4, 0.0, base2)
        q2 = q * jnp.exp(cum - base2)
        k2 = []
        for p in (1, 2, 3):
            bk = jnp.concatenate([_bc_rows(cum[16 * i + 4 * p - 1:16 * i + 4 * p, :], 16) for i in range(4)], axis=0)
            k2.append(kvar(bk - cum))
        k2 = jnp.concatenate(k2 + [zeros_k], axis=0)
        k3 = jnp.concatenate([kvar(pltpu.roll(cum, c - d, axis=0) - cum) for d in (1, 2, 3)] + [k.astype(BF16)],
                             axis=0)
        hm = hm_ref[...]

        def stack(x):
            return jnp.concatenate([x.astype(BF16)] * GLA_HEADS, axis=0) * hm

        p_all = jnp.concatenate([_dot_nt(stack(q1), k1), _dot_nt(stack(q2), k2), _dot_nt(stack(q), k3)], axis=1)
        pm = p_all.astype(BF16) * mask_ref[...]
        o = _dot_nt((q * jnp.exp(cum)).astype(BF16), st.astype(BF16))
        intra = []
        for h in range(GLA_HEADS):
            vh = vb[:, h * GLA_DV:(h + 1) * GLA_DV]
            intra.append(_dot(pm[h * c:(h + 1) * c, :], jnp.concatenate([vh] * GLA_SLOTS, axis=0)))
        o = o + jnp.concatenate(intra, axis=1)
        last = cum[c - 1:c, :]
        st = st * jnp.exp(last) + _dot_tn(vb, kvar(_bc_rows(last, c) - cum)) * bd_ref[...]
        r = r_ref[sl, :]
        outs = []
        for h in range(GLA_HEADS):
            lanes = slice(h * GLA_DV, (h + 1) * GLA_DV)
            outs.append(_rms_norm(o[:, lanes], gn_ref[...]) * _silu(r[:, lanes]))
        o_ref[sl, :] = jnp.concatenate(outs, axis=1).astype(BF16)
        return st

    st = st_scr[...]
    for ci in range(GLA_TB // c):
        st = chunk(ci, st)
    st_scr[...] = st

    @pl.when(pl.program_id(1) == pl.num_programs(1) - 1)
    def _():
        for h in range(GLA_HEADS):
            blk = st_scr[h * GLA_DV:(h + 1) * GLA_DV, :]
            st_ref[0, h] = blk.T[h * GLA_DK:(h + 1) * GLA_DK, :]


def _gla(h, wg, bg, gn, consts, batch, seq):
    nt = seq // GLA_TB
    tri, hm, mask, bd = consts
    rowblk = lambda w, cb: pl.BlockSpec((GLA_TB, w), lambda b, t: (b * nt + t, cb))
    return pl.pallas_call(
        _gla_kernel,
        grid=(batch, nt),
        in_specs=[rowblk(LANES, 5), rowblk(HK, 3), rowblk(HV, 2), rowblk(HV, 3), rowblk(HK, 8),
                  _resident(wg.shape), _resident(bg.shape), _resident(gn.shape),
                  _resident(tri.shape), _resident(hm.shape), _resident(mask.shape), _resident(bd.shape)],
        out_specs=[pl.BlockSpec((GLA_TB, HV), lambda b, t: (b * nt + t, 0)),
                   pl.BlockSpec((1, GLA_HEADS, GLA_DK, GLA_DV), lambda b, t: (b, 0, 0, 0))],
        out_shape=[jax.ShapeDtypeStruct((batch * seq, HV), BF16),
                   jax.ShapeDtypeStruct((batch, GLA_HEADS, GLA_DK, GLA_DV), F32)],
        scratch_shapes=[pltpu.VMEM((HV, HK), F32)],
        compiler_params=_cparams(("parallel", "arbitrary"), VMEM_LIMIT),
        name="gla",
    )(h, h, h, h, h, wg, bg, gn, tri, hm, mask, bd)


def _alibi_slope(group, head):
    return 2.0 ** (-8.0 * (group * DIL_HPG + head + 1) / (DIL_GROUPS * DIL_HPG))


DIL_TM = 512
QKV_G = 3 * MIX_C


def _proj_dil_kernel(x_ref, w_ref, *rest, n_prev, tiles):
    p_refs = rest[n_prev:n_prev + DIL_GROUPS]
    r1_ref, r2_ref, r3_ref, acc_ref = rest[n_prev + DIL_GROUPS:]
    t = pl.program_id(0) % tiles
    xb = x_ref[...].astype(BF16)
    nb = QKV_G // LANES
    for g, r_ref in enumerate((r1_ref, r2_ref, r3_ref)):
        window, dil = DIL_PAIRS[g]
        qkv = [_dot(xb, w_ref[:, (which * DIL_GROUPS + g) * MIX_C:(which * DIL_GROUPS + g + 1) * MIX_C])
               for which in range(3)]

        def put_rows(lo, width, g=g, qkv=qkv):
            for which in range(2):
                p_refs[g][which] = qkv[1 + which][lo:lo + width, :].T.reshape(DIL_HPG, DIL_HD, width)

        if window >= DIL_TM:
            keep_tiles = window // DIL_TM
            if keep_tiles >= tiles:
                put_rows(0, DIL_TM)
            else:
                pl.when(t >= tiles - keep_tiles)(lambda f=put_rows: f(0, DIL_TM))
        else:
            pl.when(t == tiles - 1)(lambda f=put_rows, w=window: f(DIL_TM - w, w))
        if dil == 1:
            for which in range(3):
                r_ref[:, which * MIX_C:(which + 1) * MIX_C] = qkv[which].astype(BF16)
            continue
        per = MIX_C // LANES
        for c in range(nb):
            acc_ref[c] = qkv[c // per][:, (c % per) * LANES:(c % per + 1) * LANES]
        for r in range(dil):
            for c in range(nb):
                r_ref[r, :, c * LANES:(c + 1) * LANES] = acc_ref[c, pl.ds(r, DIL_TM // dil, stride=dil), :].astype(BF16)


def _proj_dil(x, w, seq, prev, layer, n_layers):
    m, k = x.shape
    tm = DIL_TM
    nt = m // tm
    tiles = seq // tm
    res = lambda dil: ((m // seq, dil, seq // dil, QKV_G),
                       pl.BlockSpec((None, dil, tm // dil, QKV_G), lambda i: (i // tiles, 0, i % tiles, 0)))
    (s2, b2), (s3, b3) = res(DIL_PAIRS[1][1]), res(DIL_PAIRS[2][1])
    p_shapes, p_specs = [], []
    for window, _ in DIL_PAIRS:
        keep = min(window, seq)
        width = min(keep, tm)
        first = tiles - max(keep // tm, 1)
        p_shapes.append(jax.ShapeDtypeStruct((n_layers, m // seq, 2, DIL_HPG, DIL_HD, keep), F32))
        p_specs.append(pl.BlockSpec(
            (None, None, 2, DIL_HPG, DIL_HD, width),
            lambda i, first=first: (layer, i // tiles, 0, 0, 0, jnp.maximum(i % tiles - first, 0))))
    in_specs = [pl.BlockSpec((tm, k), lambda i: (i, 0)), _resident(w.shape)]
    args = [x, w]
    aliases = {}
    if prev is not None:
        in_specs += [pl.BlockSpec(memory_space=pl.ANY)] * DIL_GROUPS
        aliases = {len(args) + g: g for g in range(DIL_GROUPS)}
        args += list(prev)
    outs = pl.pallas_call(
        functools.partial(_proj_dil_kernel, n_prev=len(aliases), tiles=tiles),
        grid=(nt,),
        in_specs=in_specs,
        out_specs=p_specs + [pl.BlockSpec((tm, QKV_G), lambda i: (i, 0)), b2, b3],
        out_shape=p_shapes + [jax.ShapeDtypeStruct((m, QKV_G), BF16),
                              jax.ShapeDtypeStruct(s2, BF16), jax.ShapeDtypeStruct(s3, BF16)],
        scratch_shapes=[pltpu.VMEM((QKV_G // LANES, tm, LANES), F32)],
        input_output_aliases=aliases,
        compiler_params=_cparams(("arbitrary",), VMEM_LIMIT),
        name="proj_dil",
    )(*args)
    return outs[:DIL_GROUPS], outs[DIL_GROUPS:]


DIL_QB = 4


def _dil_attn_kernel(q_ref, k_ref, v_ref, *rest, group, dil, mode):
    n = DIL_STEPS
    if mode == "chain":
        kp_ref, vp_ref, o_ref, lse_ref = rest
    else:
        o_ref, lse_ref = rest
    nk = 2 * n
    qi = lax.broadcasted_iota(jnp.int32, (n, nk), 0)
    ki = lax.broadcasted_iota(jnp.int32, (n, nk), 1)
    steps = qi + n - ki
    band = (steps >= 0) & (steps <= n)
    dist = (steps * dil).astype(F32)
    bias = [jnp.where(band, -_alibi_slope(group, h) * dist, NEG_INF) for h in range(DIL_HPG)]
    lane = lax.broadcasted_iota(jnp.int32, (n, LANES), 1)
    low = lane < DIL_HD
    for blk in range(DIL_QB):
        cur = slice(blk * n, (blk + 1) * n)
        if mode == "single":
            q = q_ref[blk]
            k = jnp.concatenate([k_ref[blk], k_ref[blk]], axis=0)
            v = jnp.concatenate([v_ref[blk], v_ref[blk]], axis=0)
            own = ki >= n
        else:
            q = q_ref[cur, :]
            if blk > 0:
                k, v = k_ref[(blk - 1) * n:(blk + 1) * n, :], v_ref[(blk - 1) * n:(blk + 1) * n, :]
                own = None
            elif mode == "chain":
                k = jnp.concatenate([kp_ref[...], k_ref[cur, :]], axis=0)
                v = jnp.concatenate([vp_ref[...], v_ref[cur, :]], axis=0)
                own = (ki >= n) | (pl.program_id(1) > 0)
            else:
                k = jnp.concatenate([k_ref[cur, :], k_ref[cur, :]], axis=0)
                v = jnp.concatenate([v_ref[cur, :], v_ref[cur, :]], axis=0)
                own = ki >= n
        o_all = []
        lse = jnp.zeros((n, LANES), F32)
        for j in range(DIL_HPG // 2):
            cols = slice(j * LANES, (j + 1) * LANES)
            q2, k2, v2 = q[:, cols], k[:, cols], v[:, cols]
            outs = []
            for par in range(2):
                qh = jnp.where(low if par == 0 else ~low, q2, jnp.zeros_like(q2))
                s = _dot_nt(qh, k2) + bias[2 * j + par]
                if own is not None:
                    s = jnp.where(own, s, NEG_INF)
                m = jnp.max(s, axis=-1, keepdims=True)
                p = jnp.exp(s - m)
                l = jnp.sum(p, axis=-1, keepdims=True)
                outs.append(_dot(p.astype(BF16), v2) / l)
                lse = jnp.where(lane == 2 * j + par, m + jnp.log(l), lse)
            o_all.append(jnp.where(low, outs[0], outs[1]))
        dst = (blk,) if mode == "single" else (cur, slice(None))
        o_ref[dst] = jnp.concatenate(o_all, axis=1).astype(BF16)
        lse_ref[dst] = lse


def _dil_attn(qkv_r, group):
    n = DIL_STEPS
    batch, dil, ls, _ = qkv_r.shape
    rows = DIL_QB * n
    if ls == n:
        mode, steps = "single", dil // DIL_QB
        spec = lambda c, w=MIX_C: pl.BlockSpec((None, DIL_QB, n, w), lambda b, s: (b, s, 0, c))
        extra = []
    elif ls == rows:
        mode, steps = "whole", dil
        spec = lambda c, w=MIX_C: pl.BlockSpec((None, None, rows, w), lambda b, s: (b, s, 0, c))
        extra = []
    else:
        assert dil == 1 and ls % rows == 0
        mode, steps = "chain", ls // rows
        spec = lambda c, w=MIX_C: pl.BlockSpec((None, None, rows, w), lambda b, s: (b, 0, s, c))
        extra = [pl.BlockSpec((None, None, n, MIX_C), lambda b, s, c=c: (b, 0, jnp.maximum(s * DIL_QB - 1, 0), c))
                 for c in (1, 2)]
    in_specs = [spec(0), spec(1), spec(2)] + extra
    return pl.pallas_call(
        functools.partial(_dil_attn_kernel, group=group, dil=DIL_PAIRS[group][1], mode=mode),
        grid=(batch, steps),
        in_specs=in_specs,
        out_specs=[spec(0), spec(0, LANES)],
        out_shape=[jax.ShapeDtypeStruct((batch, dil, ls, MIX_C), BF16),
                   jax.ShapeDtypeStruct((batch, dil, ls, LANES), F32)],
        compiler_params=_cparams(("parallel", "arbitrary"), VMEM_LIMIT),
        name="dil_attn%d" % group,
    )(*([qkv_r] * len(in_specs)))


def _dil_mix_ln_kernel(o1, o2, o3, l1, l2, l3, e_ref, w_ref, x_ref, g_ref, b_ref, o_ref, so2, so3, sl2, sl3):
    for src, dst, dil in ((o2, so2, DIL_PAIRS[1][1]), (l2, sl2, DIL_PAIRS[1][1]),
                          (o3, so3, DIL_PAIRS[2][1]), (l3, sl3, DIL_PAIRS[2][1])):
        for r in range(dil):
            for c in range(dst.shape[0]):
                dst[c, pl.ds(r, DIL_TM // dil, stride=dil), :] = src[r, :, c * LANES:(c + 1) * LANES].astype(F32)
    whole = lambda s: jnp.concatenate([s[c] for c in range(s.shape[0])], axis=1)
    os_ = [o1[...].astype(F32), whole(so2), whole(so3)]
    ls = [l1[...], sl2[0], sl3[0]]
    mx = jnp.maximum(jnp.maximum(ls[0], ls[1]), ls[2])
    es = [jnp.exp(l - mx) for l in ls]
    den = es[0] + es[1] + es[2]

    def spread(wt):
        hi = wt.astype(BF16)
        lo = (wt - hi.astype(F32)).astype(BF16)
        return _dot(hi, e_ref[...]) + _dot(lo, e_ref[...])

    mix = sum(spread(e / den) * o for e, o in zip(es, os_))
    z = DN_ALPHA * x_ref[...] + _dot(mix.astype(BF16), w_ref[...])
    o_ref[...] = _layer_norm(z, g_ref[...], b_ref[...])


def _dil_mix_ln(os_, lses, w, x, g, b):
    m = x.shape[0]
    tm = DIL_TM
    nt = m // tm
    tiles = nt // os_[0].shape[0]
    row = lambda wd: pl.BlockSpec((tm, wd), lambda i: (i, 0))
    res = lambda a: pl.BlockSpec((None, a.shape[1], tm // a.shape[1], a.shape[3]),
                                 lambda i: (i // tiles, 0, i % tiles, 0))
    flat = lambda a: a.reshape(m, a.shape[3])
    spread = np.kron(np.eye(DIL_HPG), np.ones((1, DIL_HD)))
    spread = jnp.asarray(np.concatenate([spread, np.zeros((LANES - DIL_HPG, MIX_C))]), BF16)
    blocks = lambda wd: pltpu.VMEM((wd // LANES, tm, LANES), F32)
    return pl.pallas_call(
        _dil_mix_ln_kernel,
        grid=(nt,),
        in_specs=[row(MIX_C), res(os_[1]), res(os_[2]), row(LANES), res(lses[1]), res(lses[2]), _resident(spread.shape),
                  _resident(w.shape), row(D_MODEL), _row_of(g), _row_of(b)],
        out_specs=row(D_MODEL),
        out_shape=jax.ShapeDtypeStruct((m, D_MODEL), F32),
        scratch_shapes=[blocks(MIX_C), blocks(MIX_C), blocks(LANES), blocks(LANES)],
        compiler_params=_cparams(("parallel",), VMEM_LIMIT),
        name="dil_mix_ln",
    )(flat(os_[0]), os_[1], os_[2], flat(lses[0]), lses[1], lses[2], spread, w, x, g[0], b[0])


def _mla_pre_s_kernel(h_ref, qn_ref, kvn_ref, wqn_ref, wqp_ref, wukt_ref, ck_ref, sk_ref, rows_ref, q_ref):
    h = h_ref[...]
    qn = _rms_norm(h[:, :MLA_Q_LORA], qn_ref[...]).astype(BF16)
    qnope = _dot(qn, wqn_ref[...]).astype(BF16)
    qp = _dot(qn, wqp_ref[...])
    cq = jnp.concatenate([ck_ref[...]] * MLA_HEADS, axis=1)
    sq = jnp.concatenate([sk_ref[...]] * MLA_HEADS, axis=1)
    qp = qp * cq + _roll_lanes_left(qp, 32) * sq
    ckv = _rms_norm(h[:, MLA_Q_LORA:MLA_Q_LORA + MLA_KV_LORA], kvn_ref[...])
    blk = h[:, 640:768]
    kpe = blk * ck_ref[...] + _roll_lanes_left(blk, 32) * sk_ref[...]
    rows_ref[:, :MLA_KV_LORA] = ckv
    rows_ref[:, MLA_KV_LORA:] = kpe[:, :MLA_ROPE]
    for hd in range(MLA_HEADS):
        cols = slice(hd * LANES, (hd + 1) * LANES)
        q_ref[hd, :, :MLA_KV_LORA] = _dot(qnope[:, cols], wukt_ref[hd])
        q_ref[hd, :, MLA_KV_LORA:] = qp[:, hd * LANES:hd * LANES + MLA_ROPE]


def _mla_pre_s(h, qn, kvn, wqn, wqp, wukt, tabs):
    m = h.shape[0]
    full = lambda a: _resident(a.shape)
    return pl.pallas_call(
        _mla_pre_s_kernel,
        grid=(1,),
        in_specs=[pl.BlockSpec((m, 768), lambda i: (0, 0)), full(qn), full(kvn), full(wqn), full(wqp), full(wukt)]
        + [full(t) for t in tabs[2:]],
        out_specs=[pl.BlockSpec((m, MLA_KV_DIM), lambda i: (0, 0)),
                   pl.BlockSpec((MLA_HEADS, m, MLA_KV_DIM), lambda i: (0, 0, 0))],
        out_shape=[jax.ShapeDtypeStruct((m, MLA_KV_DIM), F32), jax.ShapeDtypeStruct((MLA_HEADS, m, MLA_KV_DIM), F32)],
        compiler_params=_cparams(("arbitrary",), VMEM_LIMIT),
        name="mla_pre_s",
    )(h, qn, kvn, wqn, wqp, wukt, *tabs[2:])


DEC_PAGES = 128


def _mla_dec_kernel(pt_ref, q_ref, new_ref, *rest):
    pages = rest[:DEC_PAGES]
    o_ref, m_scr, l_scr, acc_scr = rest[DEC_PAGES:]
    step = pl.program_id(1)
    q = q_ref[0]
    qb = q.astype(BF16)

    @pl.when(step == 0)
    def _():
        new = new_ref[0]
        m_scr[...] = jnp.sum(q * new, axis=-1, keepdims=True) * MLA_SCALE
        l_scr[...] = jnp.ones_like(l_scr)
        acc_scr[...] = jnp.broadcast_to(new[:, :MLA_KV_LORA], acc_scr.shape)

    kvt = jnp.concatenate([p[...].astype(BF16) for p in pages], axis=1)
    s = _dot(qb, kvt) * MLA_SCALE
    m_old = m_scr[...]
    m_new = jnp.maximum(m_old, jnp.max(s, axis=-1, keepdims=True))
    alpha = jnp.exp(m_old - m_new)
    p = jnp.exp(s - m_new)
    m_scr[...] = m_new
    l_scr[...] = alpha * l_scr[...] + jnp.sum(p, axis=-1, keepdims=True)
    acc_scr[...] = alpha * acc_scr[...] + _dot_nt(p.astype(BF16), kvt[:MLA_KV_LORA, :])

    @pl.when(step == pl.num_programs(1) - 1)
    def _():
        o_ref[0] = acc_scr[...] / l_scr[...]


def _mla_dec(page_table, q_abs, rows_new, cache_t, layer):
    batch = q_abs.shape[0]
    n_pages = page_table.shape[1]
    steps = n_pages // DEC_PAGES
    pt = page_table.reshape(-1)

    def page_spec(j):
        return pl.BlockSpec((None, None, MLA_KV_DIM, PAGE_SIZE),
                            lambda b, s, pt_ref: (layer, pt_ref[b * n_pages + s * DEC_PAGES + j], 0, 0))

    grid_spec = pltpu.PrefetchScalarGridSpec(
        num_scalar_prefetch=1,
        grid=(batch, steps),
        in_specs=[pl.BlockSpec((1, MLA_HEADS, MLA_KV_DIM), lambda b, s, pt_ref: (b, 0, 0)),
                  pl.BlockSpec((1, 1, MLA_KV_DIM), lambda b, s, pt_ref: (b, 0, 0))]
        + [page_spec(j) for j in range(DEC_PAGES)],
        out_specs=pl.BlockSpec((1, MLA_HEADS, MLA_KV_LORA), lambda b, s, pt_ref: (b, 0, 0)),
        scratch_shapes=[pltpu.VMEM((MLA_HEADS, 1), F32), pltpu.VMEM((MLA_HEADS, 1), F32),
                        pltpu.VMEM((MLA_HEADS, MLA_KV_LORA), F32)],
    )
    return pl.pallas_call(
        _mla_dec_kernel,
        grid_spec=grid_spec,
        out_shape=jax.ShapeDtypeStruct((batch, MLA_HEADS, MLA_KV_LORA), F32),
        compiler_params=_cparams(("parallel", "arbitrary"), VMEM_LIMIT),
        name="mla_dec",
    )(pt, q_abs, rows_new.reshape(batch, 1, MLA_KV_DIM), *([cache_t] * DEC_PAGES))


GLA_DEC_B = 8


def _gla_dec_kernel(blk_ref, q_ref, v_ref, r_ref, k_ref, s_ref, wg_ref, bg_ref, gn_ref, o_ref, so_ref):
    pre = _dot(blk_ref[...].astype(BF16), wg_ref[...]) + bg_ref[...]
    a_all = jnp.exp(_log_sigmoid(pre) * (1.0 / GLA_TAU))
    eye = (lax.broadcasted_iota(jnp.int32, (GLA_DK, GLA_DK), 0)
           == lax.broadcasted_iota(jnp.int32, (GLA_DK, GLA_DK), 1))
    diag = lambda row: jnp.where(eye, jnp.broadcast_to(row, (GLA_DK, GLA_DK)), 0.0)
    for t in range(GLA_DEC_B):
        tok = slice(t, t + 1)
        a, k, v, r = a_all[tok], k_ref[tok, :], v_ref[tok, :], r_ref[tok, :]
        q = q_ref[tok, :]
        outs = []
        for h in range(GLA_HEADS):
            kl = slice(h * GLA_DK, (h + 1) * GLA_DK)
            vl = slice(h * GLA_DV, (h + 1) * GLA_DV)
            lhs = jnp.concatenate([diag(a[:, kl]), diag(k[:, kl])], axis=1)
            rhs = jnp.concatenate([s_ref[t, h], jnp.broadcast_to(v[:, vl], (GLA_DK, GLA_DV))], axis=0)
            s_new = jnp.dot(lhs, rhs, preferred_element_type=F32, precision=lax.Precision.HIGHEST)
            so_ref[t, h] = s_new
            o = jnp.dot(jnp.broadcast_to(q[:, kl], (8, GLA_DK)), s_new, preferred_element_type=F32,
                        precision=lax.Precision.HIGHEST)[:1]
            outs.append(_rms_norm(o, gn_ref[...]) * _silu(r[:, vl]))
        o_ref[tok, :] = jnp.concatenate(outs, axis=1)


def _gla_dec(h, state, layer, wg, bg, gn):
    batch = h.shape[0]
    tb = GLA_DEC_B
    rowblk = lambda w, cb: pl.BlockSpec((tb, w), lambda b: (b, cb))
    st_in = pl.BlockSpec((None, tb, GLA_HEADS, GLA_DK, GLA_DV), lambda b: (layer, b, 0, 0, 0))
    st = pl.BlockSpec((tb, GLA_HEADS, GLA_DK, GLA_DV), lambda b: (b, 0, 0, 0))
    return pl.pallas_call(
        _gla_dec_kernel,
        grid=(batch // tb,),
        in_specs=[rowblk(LANES, 5), rowblk(HK, 3), rowblk(HV, 2), rowblk(HV, 3), rowblk(HK, 8), st_in,
                  _resident(wg.shape), _resident(bg.shape), _resident(gn.shape)],
        out_specs=[pl.BlockSpec((tb, HV), lambda b: (b, 0)), st],
        out_shape=[jax.ShapeDtypeStruct((batch, HV), F32), jax.ShapeDtypeStruct(state.shape[1:], F32)],
        compiler_params=_cparams(("parallel",), VMEM_LIMIT),
        name="gla_dec",
    )(h, h, h, h, h, state, wg, bg, gn)


def _dil_dec_kernel(x_ref, c1_ref, c2_ref, c3_ref, o_ref):
    heads = range(DIL_HPG)
    hcol = lax.broadcasted_iota(jnp.int32, (DIL_HPG, 1), 0).astype(F32)
    outs, lses = [], []
    for g, c_ref in enumerate((c1_ref, c2_ref, c3_ref)):
        window, dil = DIL_PAIRS[g]
        pos = lax.broadcasted_iota(jnp.int32, (1, window), 1)
        q, k_new, v_new = (x_ref[0, w * DIL_GROUPS + g] for w in range(3))
        s = jnp.concatenate([jnp.sum(c_ref[0, h] * q[:, h:h + 1], axis=0, keepdims=True) for h in heads], axis=0)
        slope = jnp.exp((g * DIL_HPG + hcol + 1.0) * (-8.0 * math.log(2.0) / (DIL_GROUPS * DIL_HPG)))
        s = s - slope * (window - pos).astype(F32)
        s = jnp.where((pos & (dil - 1)) == 0, s, NEG_INF)
        qk = jnp.sum(q * k_new, axis=0, keepdims=True)
        s_new = jnp.concatenate([qk[:, h:h + 1] for h in heads], axis=0)
        m = jnp.maximum(jnp.max(s, axis=1, keepdims=True), s_new)
        p = jnp.exp(s - m)
        p_new = jnp.exp(s_new - m)
        l = jnp.sum(p, axis=1, keepdims=True) + p_new
        w = p / l
        w_new = p_new / l
        outs.append([jnp.sum(c_ref[1, h] * w[h:h + 1, :], axis=1, keepdims=True)
                     + w_new[h:h + 1, :] * v_new[:, h:h + 1] for h in heads])
        lses.append(m + jnp.log(l))
    mx = jnp.maximum(jnp.maximum(lses[0], lses[1]), lses[2])
    es = [jnp.exp(l - mx) for l in lses]
    den = es[0] + es[1] + es[2]
    ws = [e / den for e in es]
    o_ref[0] = jnp.concatenate(
        [sum(ws[g][h:h + 1, :] * outs[g][h] for g in range(DIL_GROUPS)) for h in heads], axis=1)


def _dil_dec(qkv_t, caches_t, layer):
    batch = qkv_t.shape[0]
    specs = [pl.BlockSpec((1, 3 * DIL_GROUPS, DIL_HD, DIL_HPG), lambda b: (b, 0, 0, 0))]
    for (window, _), c in zip(DIL_PAIRS, caches_t):
        assert c.shape[-1] == window
        specs.append(pl.BlockSpec((None, None, 2, DIL_HPG, DIL_HD, window), lambda b: (layer, b, 0, 0, 0, 0)))
    return pl.pallas_call(
        _dil_dec_kernel,
        grid=(batch,),
        in_specs=specs,
        out_specs=pl.BlockSpec((1, DIL_HD, DIL_HPG), lambda b: (b, 0, 0)),
        out_shape=jax.ShapeDtypeStruct((batch, DIL_HD, DIL_HPG), F32),
        compiler_params=_cparams(("parallel",), VMEM_LIMIT),
        name="dil_dec",
    )(qkv_t, *caches_t)


def _even_in_cols():
    o_cq, o_kv, o_gq, o_gk, o_gv, o_lr, o_gr = 0, 384, 672, 928, 1184, 1696, 1712
    r = np.arange
    kpe = o_kv + MLA_KV_LORA
    idx = np.concatenate([r(o_cq, o_cq + 384), r(o_kv, o_kv + 256), r(kpe, kpe + 32), r(kpe + 16, kpe + 32),
                          r(kpe, kpe + 16), r(o_lr, o_lr + 16), np.full(48, -1), r(o_gq, o_gq + 256),
                          r(o_gv, o_gv + 512), r(o_gr, o_gr + 512), r(o_gk, o_gk + 256)])
    assert idx.shape[0] == IN_A_PAD
    return idx


def _gather_cols(w, idx):
    idx = np.asarray(idx)
    a, b = idx[:-1], idx[1:]
    same_run = ((a >= 0) & (b >= 0) & (b - a == 1)) | ((a < 0) & (b < 0))
    cuts = np.flatnonzero(~same_run) + 1
    parts = []
    for run in np.split(idx, cuts):
        if run[0] < 0:
            parts.append(jnp.zeros((w.shape[0], run.size), w.dtype))
        else:
            parts.append(w[:, int(run[0]):int(run[-1]) + 1])
    return jnp.concatenate(parts, axis=1)


def _uq_cols(kind):
    idx = []
    for h in range(MLA_HEADS):
        b = h * (MLA_NOPE + MLA_ROPE)
        nope = np.arange(b, b + 64)
        x1, x2 = np.arange(b + 64, b + 80), np.arange(b + 80, b + 96)
        if kind == "full":
            idx += [nope, x1, x2, x2, x1]
        elif kind == "nope":
            idx += [nope, np.full(64, -1)]
        else:
            idx += [x1, x2, x2, x1, np.full(64, -1)]
    return np.concatenate(idx)


TM_DENSE = 1024
TM_WIDE = 256


def kernel(x_prompt, x_sample, cache_mla, state_gla, cache_dil_w128, cache_dil_w512, cache_dil_w2048, page_table,
           w_in_a, mla_q_norm, mla_w_uq, mla_kv_norm, mla_w_uk, mla_w_uv, gla_w_gate2, gla_b_gate, gla_norm, w_out_a,
           w_in_c, w_out_c, ffn_w_in, ffn_w_out, ln_g, ln_b):
    batch, seq, _ = x_prompt.shape
    dbatch = x_sample.shape[0]
    assert seq % MLA_TQ == 0 and seq % DIL_TM == 0 and seq >= DIL_PAIRS[-1][0]
    n_even, n_odd = (DEPTH + 1) // 2, DEPTH // 2
    cache_mla_t = jnp.transpose(cache_mla, (0, 1, 3, 2))
    dil_caches_t = [jnp.transpose(c, (0, 1, 3, 4, 5, 2)) for c in (cache_dil_w128, cache_dil_w512, cache_dil_w2048)]
    gla_consts = _gla_constants()
    tabs_p = _rope_tables(jnp.arange(seq, dtype=jnp.int32))
    tabs_s = _rope_tables(jnp.full((dbatch,), PAST_LEN, jnp.int32))
    even_cols = _even_in_cols()
    even_scale = np.ones((1, IN_A_PAD), np.float32)
    even_scale[:, 768:768 + HK] = GLA_DK ** -0.5
    odd_scale = np.ones((1, IN_C), np.float32)
    odd_scale[:, :DIL_GROUPS * MIX_C] = DIL_HD ** -0.5
    even_scale, odd_scale = jnp.asarray(even_scale), jnp.asarray(odd_scale)
    row2 = lambda v: v.reshape(1, -1).astype(F32)
    ln_g3 = ln_g.reshape(2 * DEPTH, 1, D_MODEL)
    ln_b3 = ln_b.reshape(2 * DEPTH, 1, D_MODEL)

    xp = x_prompt.reshape(batch * seq, D_MODEL)
    xs = x_sample.reshape(dbatch, D_MODEL)
    mla_s, gla_p, gla_s = [], [], []
    rows_t, dil_t = None, None
    dil_s = [[] for _ in DIL_PAIRS]

    for layer in range(DEPTH):
        i = layer // 2
        g0, b0, g1, b1 = (ln_g3, 2 * layer), (ln_b3, 2 * layer), (ln_g3, 2 * layer + 1), (ln_b3, 2 * layer + 1)
        if layer % 2 == 0:
            w_in = (_gather_cols(w_in_a[i], even_cols) * even_scale).astype(BF16)
            wq_full = _gather_cols(mla_w_uq[i], _uq_cols("full")).astype(BF16)
            wq_nope = _gather_cols(mla_w_uq[i], _uq_cols("nope")).astype(BF16)
            wq_rope = _gather_cols(mla_w_uq[i], _uq_cols("rope")).astype(BF16)
            wuk_pad = jnp.pad(mla_w_uk[i], ((0, 0), (0, 0), (0, LANES - MLA_NOPE))).reshape(MLA_KV_LORA, -1).astype(BF16)
            wuk_t = jnp.pad(jnp.transpose(mla_w_uk[i], (1, 2, 0)), ((0, 0), (0, LANES - MLA_NOPE), (0, 0))).astype(BF16)
            wuv_t = mla_w_uv[i].reshape(MLA_KV_LORA, -1).T.astype(BF16)
            wuv_bd = (jnp.eye(MLA_HEADS, dtype=F32)[:, None, :, None]
                      * jnp.transpose(mla_w_uv[i], (1, 0, 2))[:, :, None, :]).reshape(
                          MLA_HEADS * MLA_KV_LORA, MLA_HEADS * MLA_V).astype(BF16)
            wg = jnp.pad(gla_w_gate2[i], ((64, LANES - 64 - GLA_GATE_RANK), (0, 0))).astype(BF16)
            bg, gn = row2(gla_b_gate[i]), row2(gla_norm[i])
            qn, kvn = row2(mla_q_norm[i]), row2(mla_kv_norm[i])
            w_out = w_out_a[i].astype(BF16)

            h = _proj(xp, w_in, TM_DENSE)
            rows_t, qf, kf, vt = _mla_pre(h, qn, kvn, wq_full, wuk_pad, wuv_t, tabs_p, seq, rows_t, i, n_even)
            mla_out = _mla_attn(qf, kf, vt, batch, seq)
            gla_out, s_fin = _gla(h, wg, bg, gn, gla_consts, batch, seq)
            gla_p.append(s_fin)
            xp = _mix_ln([mla_out, gla_out], w_out, xp, g0, b0, TM_DENSE)

            h = _proj(xs, w_in, TM_DENSE)
            rows, q_abs = _mla_pre_s(h, qn, kvn, wq_nope, wq_rope, wuk_t, tabs_s)
            lat = _mla_dec(page_table, jnp.transpose(q_abs, (1, 0, 2)), rows, cache_mla_t, i)
            mla_out = _proj(lat.reshape(dbatch, MLA_HEADS * MLA_KV_LORA), wuv_bd, TM_DENSE)
            gla_out, s_fin = _gla_dec(h, state_gla, i, wg, bg, gn)
            mla_s.append(rows.reshape(dbatch, 1, MLA_KV_DIM))
            gla_s.append(s_fin)
            xs = _mix_ln([mla_out, gla_out], w_out, xs, g0, b0, TM_DENSE)
        else:
            w_in = (w_in_c[i] * odd_scale).astype(BF16)
            w_out = w_out_c[i].astype(BF16)

            dil_t, (r1, r2, r3) = _proj_dil(xp, w_in, seq, dil_t, i, n_odd)
            res = [r1.reshape(batch, 1, seq, QKV_G), r2, r3]
            os_, lses = zip(*[_dil_attn(res[g], g) for g in range(DIL_GROUPS)])
            xp = _dil_mix_ln(os_, lses, w_out, xp, g0, b0)

            qkv = _proj(xs, w_in, TM_WIDE).reshape(dbatch, 3 * DIL_GROUPS, DIL_HPG, DIL_HD)
            mix = _dil_dec(jnp.transpose(qkv, (0, 1, 3, 2)), dil_caches_t, i)
            mix = jnp.transpose(mix, (0, 2, 1)).reshape(dbatch, MIX_C)
            for g in range(DIL_GROUPS):
                dil_s[g].append(jnp.stack([qkv[:, DIL_GROUPS + g], qkv[:, 2 * DIL_GROUPS + g]], axis=1)
                                .reshape(dbatch, 1, 2, DIL_HPG, DIL_HD))
            xs = _mix_ln([mix], w_out, xs, g0, b0, TM_DENSE)

        wi, wo = ffn_w_in[layer].astype(BF16), ffn_w_out[layer].astype(BF16)
        xp = _ffn(xp, wi, wo, g1, b1, TM_FFN)
        xs = _ffn(xs, wi, wo, g1, b1, TM_FFN)

    st = jnp.stack
    mla_p = jnp.transpose(rows_t, (0, 1, 3, 2))
    dil_p = [jnp.transpose(p, (0, 1, 5, 2, 3, 4)) for p in dil_t]
    return (xp.reshape(batch, seq, D_MODEL), xs.reshape(dbatch, 1, D_MODEL), mla_p, st(mla_s), st(gla_p), st(gla_s),
            dil_p[0], st(dil_s[0]), dil_p[1], st(dil_s[1]), dil_p[2], st(dil_s[2]))
```
